```python
import math
import jax, jax.numpy as jnp
from jax import lax
import numpy as np


D_MODEL = 4096
BATCH = 1
SEQ = 16384
DEPTH = 4

N_MIXERS = 4
HEAD_DIM = 128
SCALE = HEAD_DIM ** -0.5
GRID_W = 64
Q_BLOCK = 128
EPS = 1e-6

REL_BUCKETS = 32
REL_MAX_DIST = 1024
REL_HEADS = D_MODEL // (2 * HEAD_DIM)

A_HEADS = D_MODEL // (2 * HEAD_DIM)
A_IN = 3 * A_HEADS * 2 * HEAD_DIM
A_OUT = A_HEADS * 2 * HEAD_DIM

B_GROUPS = ((128, 1), (512, 4), (2048, 16))
B_HEADS = D_MODEL // (2 * HEAD_DIM)
B_WIDTH = B_HEADS * HEAD_DIM
B_IN = 3 * len(B_GROUPS) * B_WIDTH

C_HEADS = D_MODEL // HEAD_DIM
C_WIN_R = 8
C_WIN_C = 16
C_IN = 3 * C_HEADS * HEAD_DIM

D_HEADS = D_MODEL // HEAD_DIM
D_KV_HEADS = 8
D_IN = (D_HEADS + 2 * D_KV_HEADS) * HEAD_DIM
ROPE_THETA = 10000.0

D_FF = -(-8 * D_MODEL // (3 * 256)) * 256

kernel_name = "interleaved_hybrid_encoder_4mixer"


def n_layers_of(m):
    return len(range(m, DEPTH, N_MIXERS))


def rmsnorm(x, g):
    xf = x.astype(jnp.float32)
    y = xf * lax.rsqrt(jnp.mean(xf * xf, axis=-1, keepdims=True) + EPS)
    return (y * g.astype(jnp.float32)).astype(x.dtype)


def t5_bucket(rel):
    nb = REL_BUCKETS // 2
    max_exact = nb // 2
    ret = jnp.where(rel > 0, nb, 0)
    n = jnp.abs(rel)
    nf = jnp.maximum(n, 1).astype(jnp.float32)
    large = max_exact + (jnp.log(nf / max_exact) / math.log(REL_MAX_DIST / max_exact)
                         * (nb - max_exact)).astype(jnp.int32)
    large = jnp.minimum(large, nb - 1)
    return ret + jnp.where(n < max_exact, n, large)


def diff_attention(h, w_in, w_out, lam_p, subln_g, rel_bias, layer_idx):
    B_, S, _ = h.shape
    q, k, v = jnp.split(h @ w_in, 3, axis=-1)
    q = q.reshape(B_, S, A_HEADS, 2, HEAD_DIM)
    k = k.reshape(B_, S, A_HEADS, 2, HEAD_DIM)
    v = v.reshape(B_, S, A_HEADS, 2 * HEAD_DIM)
    lam_init = 0.8 - 0.6 * math.exp(-0.3 * layer_idx)
    lp = lam_p.astype(jnp.float32)
    lam = jnp.exp(jnp.sum(lp[0] * lp[1])) - jnp.exp(jnp.sum(lp[2] * lp[3])) + lam_init
    nb = S // Q_BLOCK
    qb = q.reshape(B_, nb, Q_BLOCK, A_HEADS, 2, HEAD_DIM).transpose(1, 0, 2, 3, 4, 5)
    kpos = jnp.arange(S)

    def block(args):
        q_blk, b_idx = args
        qpos = b_idx * Q_BLOCK + jnp.arange(Q_BLOCK)
        bias = rel_bias[t5_bucket(kpos[None, :] - qpos[:, None])]
        bias = bias.transpose(2, 0, 1)[None, :, None].astype(jnp.float32)
        s = jnp.einsum('bqhmd,bkhmd->bhmqk', q_blk, k).astype(jnp.float32) * SCALE + bias
        p = jax.nn.softmax(s, axis=-1)
        a = p[:, :, 0] - lam * p[:, :, 1]
        return jnp.einsum('bhqk,bkhe->bqhe', a.astype(v.dtype), v)

    o = lax.map(block, (qb, jnp.arange(nb)))
    o = o.transpose(1, 0, 2, 3, 4).reshape(B_, S, A_HEADS, 2 * HEAD_DIM)
    o = rmsnorm(o, subln_g) * (1.0 - lam_init)
    return o.reshape(B_, S, A_OUT) @ w_out


def dilated_attention(h, w_in, w_out, rel_bias):
    B_, S, _ = h.shape
    G = len(B_GROUPS)
    qkv = (h @ w_in).reshape(B_, S, 3, G, B_HEADS, HEAD_DIM)
    q, k, v = qkv[:, :, 0], qkv[:, :, 1], qkv[:, :, 2]
    nb = S // Q_BLOCK
    qb = q.reshape(B_, nb, Q_BLOCK, G, B_HEADS, HEAD_DIM).transpose(1, 0, 2, 3, 4, 5)

    def block(args):
        q_blk, b_idx = args
        qpos = b_idx * Q_BLOCK + jnp.arange(Q_BLOCK)
        outs, lses = [], []
        for g, (window, dil) in enumerate(B_GROUPS):
            n_side = window // 2 // dil
            offs = dil * jnp.arange(-n_side, n_side + 1)
            kidx = qpos[:, None] + offs[None, :]
            valid = (kidx >= 0) & (kidx < S)
            kidx = jnp.clip(kidx, 0, S - 1)
            kg = k[:, :, g][:, kidx]
            vg = v[:, :, g][:, kidx]
            bias = rel_bias[t5_bucket(offs)].T.astype(jnp.float32)
            s = jnp.einsum('bqhd,bqkhd->bhqk', q_blk[:, :, g], kg).astype(jnp.float32) * SCALE
            s = jnp.where(valid[None, None], s + bias[None, :, None, :], -jnp.inf)
            m = jnp.max(s, axis=-1, keepdims=True)
            p = jnp.exp(s - m)
            den = jnp.sum(p, axis=-1)
            o = jnp.einsum('bhqk,bqkhd->bqhd', p / den[..., None], vg.astype(jnp.float32))
            outs.append(o)
            lses.append(m[..., 0] + jnp.log(den))
        wgt = jax.nn.softmax(jnp.stack(lses, 0), axis=0)
        o = jnp.einsum('gbhq,gbqhd->bqhd', wgt, jnp.stack(outs, 0))
        return o.astype(h.dtype)

    o = lax.map(block, (qb, jnp.arange(nb)))
    o = o.transpose(1, 0, 2, 3, 4).reshape(B_, S, B_WIDTH)
    return o @ w_out


def neighbourhood_attention(h, w_in, w_out, rpb):
    B_, S, _ = h.shape
    rows = S // GRID_W
    wr = min(C_WIN_R, rows)
    qkv = (h @ w_in).reshape(B_, rows, GRID_W, 3, C_HEADS, HEAD_DIM)
    q, k, v = qkv[:, :, :, 0], qkv[:, :, :, 1], qkv[:, :, :, 2]
    col = jnp.arange(GRID_W)
    c0 = jnp.clip(col - C_WIN_C // 2, 0, GRID_W - C_WIN_C)
    kcols = c0[:, None] + jnp.arange(C_WIN_C)[None, :]
    dc = kcols - col[:, None]

    def row_fn(args):
        q_row, r = args
        r0 = jnp.clip(r - wr // 2, 0, rows - wr)
        k_rows = lax.dynamic_slice_in_dim(k, r0, wr, axis=1)
        v_rows = lax.dynamic_slice_in_dim(v, r0, wr, axis=1)
        kg = k_rows[:, :, kcols]
        vg = v_rows[:, :, kcols]
        dr = r0 + jnp.arange(wr) - r
        bias = rpb[:, dr[:, None, None] + C_WIN_R - 1, dc[None] + C_WIN_C - 1]
        s = jnp.einsum('bchd,brckhd->bhcrk', q_row, kg).astype(jnp.float32) * SCALE
        s = s + bias.transpose(0, 2, 1, 3)[None].astype(jnp.float32)
        p = jax.nn.softmax(s.reshape(s.shape[:3] + (wr * C_WIN_C,)), axis=-1).reshape(s.shape)
        return jnp.einsum('bhcrk,brckhd->bchd', p.astype(vg.dtype), vg)

    o = lax.map(row_fn, (q.transpose(1, 0, 2, 3, 4), jnp.arange(rows)))
    o = o.transpose(1, 0, 2, 3, 4).reshape(B_, S, C_HEADS * HEAD_DIM)
    return o @ w_out


def axial_rope(S):
    pos = jnp.arange(S)
    n_freq = HEAD_DIM // 4
    freqs = ROPE_THETA ** (-jnp.arange(n_freq, dtype=jnp.float32) / n_freq)
    ang = jnp.concatenate([(pos // GRID_W).astype(jnp.float32)[:, None] * freqs,
                           (pos % GRID_W).astype(jnp.float32)[:, None] * freqs], axis=-1)
    return jnp.cos(ang), jnp.sin(ang)


def apply_rope(x, cos, sin):
    xf = x.astype(jnp.float32).reshape(x.shape[:-1] + (HEAD_DIM // 2, 2))
    xe, xo = xf[..., 0], xf[..., 1]
    c, s = cos[None, :, None, :], sin[None, :, None, :]
    out = jnp.stack([xe * c - xo * s, xe * s + xo * c], axis=-1).reshape(x.shape)
    return out.astype(x.dtype)


def gqa_axial(h, w_in, w_out, qk_g):
    B_, S, _ = h.shape
    qkv = h @ w_in
    nq, nkv = D_HEADS * HEAD_DIM, D_KV_HEADS * HEAD_DIM
    q = qkv[..., :nq].reshape(B_, S, D_HEADS, HEAD_DIM)
    k = qkv[..., nq:nq + nkv].reshape(B_, S, D_KV_HEADS, HEAD_DIM)
    v = qkv[..., nq + nkv:].reshape(B_, S, D_KV_HEADS, HEAD_DIM)
    cos, sin = axial_rope(S)
    q = apply_rope(rmsnorm(q, qk_g[0]), cos, sin)
    k = apply_rope(rmsnorm(k, qk_g[1]), cos, sin)
    rep = D_HEADS // D_KV_HEADS
    nb = S // Q_BLOCK
    qb = q.reshape(B_, nb, Q_BLOCK, D_KV_HEADS, rep, HEAD_DIM).transpose(1, 0, 2, 3, 4, 5)

    def block(q_blk):
        s = jnp.einsum('bqgrd,bkgd->bgrqk', q_blk, k).astype(jnp.float32) * SCALE
        p = jax.nn.softmax(s, axis=-1)
        return jnp.einsum('bgrqk,bkgd->bqgrd', p.astype(v.dtype), v)

    o = lax.map(block, qb)
    o = o.transpose(1, 0, 2, 3, 4, 5).reshape(B_, S, nq)
    return o @ w_out


def swiglu(h, w_gate, w_up, w_down):
    return (jax.nn.silu(h @ w_gate) * (h @ w_up)) @ w_down


def _normal(key, shape, scale):
    return jax.random.normal(key, shape, jnp.float32) * scale


def setup_inputs(seed: int = 0) -> dict:
    key = jax.random.key(seed)
    ks = jax.random.split(key, 20)
    nA, nB, nC, nD = (n_layers_of(m) for m in range(N_MIXERS))
    return {
        "x": _normal(ks[0], (BATCH, SEQ, D_MODEL), 1.0),
        "rel_bias": _normal(ks[1], (REL_BUCKETS, REL_HEADS), 0.5),
        "norm_g": 1.0 + _normal(ks[2], (DEPTH, 4, D_MODEL), 0.02),
        "a_w_in": _normal(ks[3], (nA, D_MODEL, A_IN), D_MODEL ** -0.5),
        "a_w_out": _normal(ks[4], (nA, A_OUT, D_MODEL), A_OUT ** -0.5),
        "a_lambda": _normal(ks[5], (nA, 4, HEAD_DIM), 0.1),
        "a_subln": 1.0 + _normal(ks[6], (nA, 2 * HEAD_DIM), 0.02),
        "b_w_in": _normal(ks[7], (nB, D_MODEL, B_IN), D_MODEL ** -0.5),
        "b_w_out": _normal(ks[8], (nB, B_WIDTH, D_MODEL), B_WIDTH ** -0.5),
        "c_w_in": _normal(ks[9], (nC, D_MODEL, C_IN), D_MODEL ** -0.5),
        "c_w_out": _normal(ks[10], (nC, C_HEADS * HEAD_DIM, D_MODEL), (C_HEADS * HEAD_DIM) ** -0.5),
        "c_rpb": _normal(ks[11], (nC, C_HEADS, 2 * C_WIN_R - 1, 2 * C_WIN_C - 1), 0.5),
        "d_w_in": _normal(ks[12], (nD, D_MODEL, D_IN), D_MODEL ** -0.5),
        "d_w_out": _normal(ks[13], (nD, D_HEADS * HEAD_DIM, D_MODEL), (D_HEADS * HEAD_DIM) ** -0.5),
        "d_qk_norm": 1.0 + _normal(ks[14], (nD, 2, HEAD_DIM), 0.02),
        "ffn_w_gate": _normal(ks[15], (DEPTH, D_MODEL, D_FF), D_MODEL ** -0.5),
        "ffn_w_up": _normal(ks[16], (DEPTH, D_MODEL, D_FF), D_MODEL ** -0.5),
        "ffn_w_down": _normal(ks[17], (DEPTH, D_FF, D_MODEL), D_FF ** -0.5),
    }


def reference(x, rel_bias, norm_g, a_w_in, a_w_out, a_lambda, a_subln, b_w_in, b_w_out,
              c_w_in, c_w_out, c_rpb, d_w_in, d_w_out, d_qk_norm,
              ffn_w_gate, ffn_w_up, ffn_w_down):
    for i in range(DEPTH):
        m, j = i % N_MIXERS, i // N_MIXERS
        hn = rmsnorm(x, norm_g[i, 0])
        if m == 0:
            y = diff_attention(hn, a_w_in[j], a_w_out[j], a_lambda[j], a_subln[j], rel_bias, i)
        elif m == 1:
            y = dilated_attention(hn, b_w_in[j], b_w_out[j], rel_bias)
        elif m == 2:
            y = neighbourhood_attention(hn, c_w_in[j], c_w_out[j], c_rpb[j])
        else:
            y = gqa_axial(hn, d_w_in[j], d_w_out[j], d_qk_norm[j])
        x = x + rmsnorm(y, norm_g[i, 1])
        hn = rmsnorm(x, norm_g[i, 2])
        x = x + rmsnorm(swiglu(hn, ffn_w_gate[i], ffn_w_up[i], ffn_w_down[i]), norm_g[i, 3])
    return x
```

```python
import functools
import math

import numpy as np
import jax
import jax.numpy as jnp
from jax import lax
from jax.experimental import pallas as pl
from jax.experimental.pallas import tpu as pltpu

F32 = jnp.float32
BF16 = jnp.bfloat16

HEAD_DIM = 128
SCALE = HEAD_DIM ** -0.5
GRID_W = 64
EPS = 1e-6
REL_BUCKETS = 32
REL_MAX_DIST = 1024
B_GROUPS = ((128, 1), (512, 4), (2048, 16))
C_WIN_R = 8
C_WIN_C = 16
D_KV_HEADS = 8
ROPE_THETA = 10000.0
N_MIXERS = 4

MASKED = -1e30
VMEM_LIMIT_BYTES = 56 * 2 ** 20


def _params(*semantics):
    return pltpu.CompilerParams(dimension_semantics=semantics, vmem_limit_bytes=VMEM_LIMIT_BYTES)


def _rms(x, g):
    ms = jnp.mean(x * x, axis=-1, keepdims=True)
    return x * lax.rsqrt(ms + EPS) * g


def _dot_nt(a, b):
    return lax.dot_general(a, b, (((1,), (1,)), ((), ())), preferred_element_type=F32)


def _prenorm_kernel(x_ref, g_ref, h_ref):
    h_ref[...] = _rms(x_ref[...], g_ref[...]).astype(h_ref.dtype)


def prenorm(x, g, tm=256):
    S, D = x.shape
    row = pl.BlockSpec((tm, D), lambda i: (i, 0))
    vec = pl.BlockSpec((1, D), lambda i: (0, 0))
    return pl.pallas_call(
        _prenorm_kernel, grid=(S // tm,), in_specs=[row, vec], out_specs=row,
        out_shape=jax.ShapeDtypeStruct((S, D), BF16), compiler_params=_params("parallel"),
        name="prenorm")(x, g.reshape(1, D))


def _resid_kernel(x_ref, y_ref, g1_ref, g2_ref, xo_ref, h_ref):
    x = x_ref[...] + _rms(y_ref[...], g1_ref[...])
    xo_ref[...] = x
    h_ref[...] = _rms(x, g2_ref[...]).astype(h_ref.dtype)


def _resid_last_kernel(x_ref, y_ref, g1_ref, xo_ref):
    xo_ref[...] = x_ref[...] + _rms(y_ref[...], g1_ref[...])


def resid(x, y, g_post, g_next=None, tm=256):
    S, D = x.shape
    row = pl.BlockSpec((tm, D), lambda i: (i, 0))
    vec = pl.BlockSpec((1, D), lambda i: (0, 0))
    if g_next is None:
        return pl.pallas_call(
            _resid_last_kernel, grid=(S // tm,), in_specs=[row, row, vec], out_specs=row,
            out_shape=jax.ShapeDtypeStruct((S, D), F32), compiler_params=_params("parallel"),
            name="resid_last")(x, y, g_post.reshape(1, D))
    return pl.pallas_call(
        _resid_kernel, grid=(S // tm,), in_specs=[row, row, vec, vec], out_specs=[row, row],
        out_shape=[jax.ShapeDtypeStruct((S, D), F32), jax.ShapeDtypeStruct((S, D), BF16)],
        compiler_params=_params("parallel"),
        name="resid")(x, y, g_post.reshape(1, D), g_next.reshape(1, D))


def _mm_kernel(a_ref, b_ref, o_ref):
    o_ref[...] = jnp.dot(a_ref[...], b_ref[...], preferred_element_type=F32).astype(o_ref.dtype)


def _mm_acc_kernel(a_ref, b_ref, o_ref, acc_ref):
    k = pl.program_id(2)

    @pl.when(k == 0)
    def _():
        acc_ref[...] = jnp.zeros_like(acc_ref)

    acc_ref[...] += jnp.dot(a_ref[...], b_ref[...], preferred_element_type=F32)

    @pl.when(k == pl.num_programs(2) - 1)
    def _():
        o_ref[...] = acc_ref[...].astype(o_ref.dtype)


def matmul(a, b, out_dtype, tm=1024, tn=512, tk=None):
    M, K = a.shape
    N = b.shape[1]
    tk = K if tk is None else tk
    out_shape = jax.ShapeDtypeStruct((M, N), out_dtype)
    if tk == K:
        return pl.pallas_call(
            _mm_kernel, grid=(M // tm, N // tn),
            in_specs=[pl.BlockSpec((tm, K), lambda i, j: (i, 0)),
                      pl.BlockSpec((K, tn), lambda i, j: (0, j))],
            out_specs=pl.BlockSpec((tm, tn), lambda i, j: (i, j)),
            out_shape=out_shape, compiler_params=_params("parallel", "parallel"),
            name="matmul")(a, b)
    return pl.pallas_call(
        _mm_acc_kernel, grid=(M // tm, N // tn, K // tk),
        in_specs=[pl.BlockSpec((tm, tk), lambda i, j, k: (i, k)),
                  pl.BlockSpec((tk, tn), lambda i, j, k: (k, j))],
        out_specs=pl.BlockSpec((tm, tn), lambda i, j, k: (i, j)),
        out_shape=out_shape, scratch_shapes=[pltpu.VMEM((tm, tn), F32)],
        compiler_params=_params("parallel", "parallel", "arbitrary"),
        name="matmul_acc")(a, b)


def _glu_kernel(a_ref, wg_ref, wu_ref, o_ref):
    a = a_ref[...]
    g = jnp.dot(a, wg_ref[...], preferred_element_type=F32)
    u = jnp.dot(a, wu_ref[...], preferred_element_type=F32)
    o_ref[...] = (g * (1.0 / (1.0 + jnp.exp(-g))) * u).astype(o_ref.dtype)


def glu_up(a, wg, wu, tm=1024, tn=256):
    M, K = a.shape
    N = wg.shape[1]
    wspec = pl.BlockSpec((K, tn), lambda i, j: (0, j))
    return pl.pallas_call(
        _glu_kernel, grid=(M // tm, N // tn),
        in_specs=[pl.BlockSpec((tm, K), lambda i, j: (i, 0)), wspec, wspec],
        out_specs=pl.BlockSpec((tm, tn), lambda i, j: (i, j)),
        out_shape=jax.ShapeDtypeStruct((M, N), BF16),
        compiler_params=_params("parallel", "parallel"), name="glu_up")(a, wg, wu)


def ffn(h, w_gate, w_up, w_down):
    hid = glu_up(h, w_gate.astype(BF16), w_up.astype(BF16))
    d_ff = hid.shape[1]
    tk = d_ff // 2 if (d_ff // 2) % HEAD_DIM == 0 else d_ff
    return matmul(hid, w_down.astype(BF16), F32, tm=1024, tn=512, tk=tk)


def _t5_thresholds():
    nb = REL_BUCKETS // 2
    max_exact = nb // 2
    n = np.arange(0, 4 * REL_MAX_DIST)
    nf = np.maximum(n, 1).astype(np.float32)
    large = max_exact + (np.log(nf / np.float32(max_exact))
                         / np.float32(math.log(REL_MAX_DIST / max_exact))
                         * np.float32(nb - max_exact)).astype(np.int32)
    bucket = np.where(n < max_exact, n, np.minimum(large, nb - 1))
    assert np.all(np.diff(bucket) >= 0) and bucket[-1] == nb - 1
    return tuple(int(np.argmax(bucket >= b)) for b in range(1, nb))


T5_THRESHOLDS = _t5_thresholds()


def _t5_tile_kernel(rb_ref, o_ref, *, offs, dil, half, col_lo, col_hi, n_heads):
    h = pl.program_id(0)
    rows, cols = o_ref.shape[2], o_ref.shape[3]
    i = lax.broadcasted_iota(jnp.int32, (rows, cols), 0)
    j = lax.broadcasted_iota(jnp.int32, (rows, cols), 1)
    nb = REL_BUCKETS // 2
    for v, off in enumerate(offs):
        delta = j - i + off
        rel = delta * dil
        n = jnp.abs(rel)
        vneg = jnp.full((rows, cols), rb_ref[h], F32)
        vpos = jnp.full((rows, cols), rb_ref[nb * n_heads + h], F32)
        for b, t in enumerate(T5_THRESHOLDS, start=1):
            c = n >= t
            vneg = jnp.where(c, rb_ref[b * n_heads + h], vneg)
            vpos = jnp.where(c, rb_ref[(nb + b) * n_heads + h], vpos)
        val = jnp.where(rel > 0, vpos, vneg)
        if half is not None:
            ok = (jnp.abs(delta) <= half) & (j >= col_lo[v]) & (j < col_hi[v])
            val = jnp.where(ok, val, MASKED)
        o_ref[v, 0] = val


def t5_tiles(rel_bias, offs, rows, cols, dil=1, half=None, col_lo=None, col_hi=None):
    n_heads = rel_bias.shape[1]
    nv = len(offs)
    kern = functools.partial(_t5_tile_kernel, offs=tuple(offs), dil=dil, half=half,
                             col_lo=col_lo, col_hi=col_hi, n_heads=n_heads)
    return pl.pallas_call(
        kern, grid=(n_heads,),
        in_specs=[pl.BlockSpec(memory_space=pltpu.SMEM)],
        out_specs=pl.BlockSpec((nv, 1, rows, cols), lambda h: (0, h, 0, 0)),
        out_shape=jax.ShapeDtypeStruct((nv, n_heads, rows, cols), F32),
        compiler_params=_params("parallel"), name="t5_tiles")(rel_bias.reshape(-1))


def _online_step(m_sc, l_sc, acc_sc, idx, s, v):
    m_old = m_sc[idx]
    m_new = jnp.maximum(m_old, jnp.max(s, axis=-1, keepdims=True))
    alpha = jnp.exp(m_old - m_new)
    p = jnp.exp(s - m_new)
    l_sc[idx] = alpha * l_sc[idx] + jnp.sum(p, axis=-1, keepdims=True)
    acc_sc[idx] = alpha * acc_sc[idx] + jnp.dot(p.astype(v.dtype), v, preferred_element_type=F32)
    m_sc[idx] = m_new


def _flash_init(m_sc, l_sc, acc_sc):
    m_sc[...] = jnp.full(m_sc.shape, -jnp.inf, F32)
    l_sc[...] = jnp.zeros(l_sc.shape, F32)
    acc_sc[...] = jnp.zeros(acc_sc.shape, F32)


def _attn_a_kernel(q_ref, k_ref, v_ref, band_ref, lam_ref, g_ref, o_ref, m_sc, l_sc, acc_sc,
                   *, T, nk, band, lam_init):
    qb = pl.program_id(1)
    _flash_init(m_sc, l_sc, acc_sc)
    q = q_ref[...]

    def body(kb, carry):
        off = pl.multiple_of(kb * T, T)
        kblk = k_ref[pl.ds(off, T), :]
        vblk = v_ref[pl.ds(off, T), :]
        bias = band_ref[jnp.clip(kb - qb + band, 0, 2 * band), 0]
        for m in range(2):
            sl = slice(m * HEAD_DIM, (m + 1) * HEAD_DIM)
            _online_step(m_sc, l_sc, acc_sc, m, _dot_nt(q[:, sl], kblk[:, sl]) + bias, vblk)
        return carry

    lax.fori_loop(0, nk, body, 0)
    lp = lam_ref[...]
    lam = (jnp.exp(jnp.sum(lp[0:1] * lp[1:2], axis=-1, keepdims=True))
           - jnp.exp(jnp.sum(lp[2:3] * lp[3:4], axis=-1, keepdims=True)) + lam_init)
    o = acc_sc[0] / l_sc[0] - lam * (acc_sc[1] / l_sc[1])
    o_ref[...] = (_rms(o, g_ref[...]) * (1.0 - lam_init)).astype(o_ref.dtype)


def mixer_a(h, w_in, w_out, lam_p, subln_g, rel_bias, layer_idx, T=512):
    S, D = h.shape
    n_heads = D // (2 * HEAD_DIM)
    width = 2 * HEAD_DIM
    nk = S // T
    colscale = jnp.concatenate([jnp.full((D,), SCALE, F32), jnp.ones((2 * D,), F32)])
    qkv = matmul(h, (w_in * colscale).astype(BF16), BF16)
    far = T5_THRESHOLDS[-1]
    band = -(-(far - 1) // T) + 1
    tiles = t5_tiles(rel_bias, [(d - band) * T for d in range(2 * band + 1)], T, T)
    lam_init = 0.8 - 0.6 * math.exp(-0.3 * layer_idx)
    kern = functools.partial(_attn_a_kernel, T=T, nk=nk, band=band, lam_init=lam_init)
    once = pl.Buffered(1)
    o = pl.pallas_call(
        kern, grid=(n_heads, nk),
        in_specs=[pl.BlockSpec((T, width), lambda hh, qb: (qb, hh)),
                  pl.BlockSpec((S, width), lambda hh, qb: (0, n_heads + hh), pipeline_mode=once),
                  pl.BlockSpec((S, width), lambda hh, qb: (0, 2 * n_heads + hh), pipeline_mode=once),
                  pl.BlockSpec((2 * band + 1, 1, T, T), lambda hh, qb: (0, hh, 0, 0), pipeline_mode=once),
                  pl.BlockSpec((4, HEAD_DIM), lambda hh, qb: (0, 0)),
                  pl.BlockSpec((1, width), lambda hh, qb: (0, 0))],
        out_specs=pl.BlockSpec((T, width), lambda hh, qb: (qb, hh)),
        out_shape=jax.ShapeDtypeStruct((S, D), BF16),
        scratch_shapes=[pltpu.VMEM((2, T, 1), F32), pltpu.VMEM((2, T, 1), F32),
                        pltpu.VMEM((2, T, width), F32)],
        compiler_params=_params("parallel", "arbitrary"),
        name="attn_a")(qkv, qkv, qkv, tiles, lam_p, subln_g.reshape(1, width))
    return matmul(o, w_out.astype(BF16), F32)


def _rope_kernel(x_ref, c_ref, sa_ref, sb_ref, g_ref, o_ref, *, n_q, n_chunks):
    c, sa, sb = c_ref[...], sa_ref[...], sb_ref[...]
    for ch in range(n_chunks):
        sl = slice(ch * HEAD_DIM, (ch + 1) * HEAD_DIM)
        y = _rms(x_ref[:, sl], g_ref[0:1] if ch < n_q else g_ref[1:2])
        y = y * c + pltpu.roll(y, HEAD_DIM - 1, 1) * sa + pltpu.roll(y, 1, 1) * sb
        if ch < n_q:
            y = y * SCALE
        o_ref[:, sl] = y.astype(o_ref.dtype)


def _rope_tables(S):
    pos = jnp.arange(S)
    n_freq = HEAD_DIM // 4
    freqs = ROPE_THETA ** (-jnp.arange(n_freq, dtype=F32) / n_freq)
    ang = jnp.concatenate([(pos // GRID_W).astype(F32)[:, None] * freqs,
                           (pos % GRID_W).astype(F32)[:, None] * freqs], axis=-1)
    cos, sin = jnp.cos(ang), jnp.sin(ang)
    zero = jnp.zeros_like(sin)
    c = jnp.stack([cos, cos], axis=-1).reshape(S, HEAD_DIM)
    sa = jnp.stack([-sin, zero], axis=-1).reshape(S, HEAD_DIM)
    sb = jnp.stack([zero, sin], axis=-1).reshape(S, HEAD_DIM)
    return c, sa, sb


def _attn_d_kernel(q_ref, k_ref, v_ref, o_ref, m_sc, l_sc, acc_sc, *, T, nk, rep):
    _flash_init(m_sc, l_sc, acc_sc)
    q = q_ref[...]

    def body(kb, carry):
        off = pl.multiple_of(kb * T, T)
        kblk = k_ref[pl.ds(off, T), :]
        vblk = v_ref[pl.ds(off, T), :]
        for r in range(rep):
            sl = slice(r * HEAD_DIM, (r + 1) * HEAD_DIM)
            _online_step(m_sc, l_sc, acc_sc, r, _dot_nt(q[:, sl], kblk), vblk)
        return carry

    lax.fori_loop(0, nk, body, 0)
    for r in range(rep):
        o_ref[:, r * HEAD_DIM:(r + 1) * HEAD_DIM] = (acc_sc[r] / l_sc[r]).astype(o_ref.dtype)


def mixer_d(h, w_in, w_out, qk_g, T=512, tm=256):
    S, D = h.shape
    n_q = D // HEAD_DIM
    n_kv = D_KV_HEADS
    rep = n_q // n_kv
    nqk = (n_q + n_kv) * HEAD_DIM
    qk = matmul(h, w_in[:, :nqk].astype(BF16), F32)
    v = matmul(h, w_in[:, nqk:].astype(BF16), BF16)
    c, sa, sb = _rope_tables(S)
    tab = pl.BlockSpec((tm, HEAD_DIM), lambda i: (i, 0))
    qk = pl.pallas_call(
        functools.partial(_rope_kernel, n_q=n_q, n_chunks=n_q + n_kv), grid=(S // tm,),
        in_specs=[pl.BlockSpec((tm, nqk), lambda i: (i, 0)), tab, tab, tab,
                  pl.BlockSpec((2, HEAD_DIM), lambda i: (0, 0))],
        out_specs=pl.BlockSpec((tm, nqk), lambda i: (i, 0)),
        out_shape=jax.ShapeDtypeStruct((S, nqk), BF16),
        compiler_params=_params("parallel"), name="qk_norm_rope")(qk, c, sa, sb, qk_g)
    nk = S // T
    kern = functools.partial(_attn_d_kernel, T=T, nk=nk, rep=rep)
    o = pl.pallas_call(
        kern, grid=(n_kv, nk),
        in_specs=[pl.BlockSpec((T, rep * HEAD_DIM), lambda g, qb: (qb, g)),
                  pl.BlockSpec((S, HEAD_DIM), lambda g, qb: (0, n_q + g)),
                  pl.BlockSpec((S, HEAD_DIM), lambda g, qb: (0, g))],
        out_specs=pl.BlockSpec((T, rep * HEAD_DIM), lambda g, qb: (qb, g)),
        out_shape=jax.ShapeDtypeStruct((S, D), BF16),
        scratch_shapes=[pltpu.VMEM((rep, T, 1), F32), pltpu.VMEM((rep, T, 1), F32),
                        pltpu.VMEM((rep, T, HEAD_DIM), F32)],
        compiler_params=_params("parallel", "arbitrary"), name="attn_d")(qk, qk, v)
    return matmul(o, w_out.astype(BF16), F32)


def _win_attn_kernel(q_ref, kp_ref, kc_ref, kn_ref, vp_ref, vc_ref, vn_ref, bias_ref, o_ref,
                     *lse_ref, n_heads, tq):
    lse_cols = []
    for h in range(n_heads):
        sl = slice(h * HEAD_DIM, (h + 1) * HEAD_DIM)
        q = q_ref[:, sl]
        ks = (kp_ref[:, sl], kc_ref[:, sl], kn_ref[:, sl])
        vs = (vp_ref[:, sl], vc_ref[:, sl], vn_ref[:, sl])
        ss = [_dot_nt(q, ks[j]) + bias_ref[0, h, :, j * tq:(j + 1) * tq] for j in range(3)]
        m = jnp.maximum(jnp.maximum(jnp.max(ss[0], axis=-1, keepdims=True),
                                    jnp.max(ss[1], axis=-1, keepdims=True)),
                        jnp.max(ss[2], axis=-1, keepdims=True))
        ps = [jnp.exp(s - m) for s in ss]
        l = (jnp.sum(ps[0], axis=-1, keepdims=True) + jnp.sum(ps[1], axis=-1, keepdims=True)
             + jnp.sum(ps[2], axis=-1, keepdims=True))
        acc = (jnp.dot(ps[0].astype(BF16), vs[0], preferred_element_type=F32)
               + jnp.dot(ps[1].astype(BF16), vs[1], preferred_element_type=F32)
               + jnp.dot(ps[2].astype(BF16), vs[2], preferred_element_type=F32))
        o_ref[:, sl] = (acc / l).astype(o_ref.dtype)
        lse_cols.append(m + jnp.log(l))
    if lse_ref:
        lse_ref[0][0] = jnp.concatenate(lse_cols, axis=1)


def _edge_variant(i, nblk):
    return jnp.where(i == 0, 0, jnp.where(i == nblk - 1, 2, 1))


def _merge_b_kernel(o0_ref, o1_ref, o2_ref, lse_ref, out_ref, *, n_heads):
    lse = lse_ref[...]
    w = jnp.exp(lse - jnp.max(lse, axis=0, keepdims=True))
    w = w / jnp.sum(w, axis=0, keepdims=True)
    for h in range(n_heads):
        sl = slice(h * HEAD_DIM, (h + 1) * HEAD_DIM)
        out_ref[:, sl] = (w[0, :, h:h + 1] * o0_ref[:, sl] + w[1, :, h:h + 1] * o1_ref[:, sl]
                          + w[2, :, h:h + 1] * o2_ref[:, sl]).astype(out_ref.dtype)


def mixer_b(h, w_in, w_out, rel_bias, tq=128, tm=256):
    S, D = h.shape
    G = len(B_GROUPS)
    n_heads = rel_bias.shape[1]
    width = n_heads * HEAD_DIM
    n_in = 3 * G * width
    ncb = 3 * G
    colscale = jnp.concatenate([jnp.full((G * width,), SCALE, F32), jnp.ones((2 * G * width,), F32)])
    qkv = matmul(h, (w_in * colscale).astype(BF16), BF16)
    outs, lses = [], []
    for g, (window, dil) in enumerate(B_GROUPS):
        half = window // 2 // dil
        assert half <= tq
        L = S // dil
        nblk = L // tq
        assert nblk >= 2
        arr = qkv.reshape(L, dil * n_in)
        bias = t5_tiles(rel_bias, [-tq] * 3, tq, 3 * tq, dil=dil, half=half,
                        col_lo=(tq, 0, 0), col_hi=(3 * tq, 3 * tq, 2 * tq))

        def col(part, r, g=g):
            return r * ncb + part * G + g

        def spec(part, di, nblk=nblk):
            return pl.BlockSpec(
                (tq, width), lambda r, i: (jnp.clip(i + di, 0, nblk - 1), col(part, r)))

        o_g, lse_g = pl.pallas_call(
            functools.partial(_win_attn_kernel, n_heads=n_heads, tq=tq), grid=(dil, nblk),
            in_specs=[spec(0, 0), spec(1, -1), spec(1, 0), spec(1, 1),
                      spec(2, -1), spec(2, 0), spec(2, 1),
                      pl.BlockSpec((1, n_heads, tq, 3 * tq),
                                   lambda r, i, nblk=nblk: (_edge_variant(i, nblk), 0, 0, 0))],
            out_specs=[pl.BlockSpec((tq, width), lambda r, i: (i, r)),
                       pl.BlockSpec((1, tq, n_heads), lambda r, i: (r, i, 0))],
            out_shape=[jax.ShapeDtypeStruct((L, dil * width), F32),
                       jax.ShapeDtypeStruct((dil, L, n_heads), F32)],
            compiler_params=_params("parallel", "parallel"),
            name=f"attn_b{g}")(arr, arr, arr, arr, arr, arr, arr, bias)
        outs.append(o_g.reshape(S, width))
        lses.append(lse_g.transpose(1, 0, 2).reshape(S, n_heads))
    row = pl.BlockSpec((tm, width), lambda i: (i, 0))
    o = pl.pallas_call(
        functools.partial(_merge_b_kernel, n_heads=n_heads), grid=(S // tm,),
        in_specs=[row, row, row, pl.BlockSpec((G, tm, n_heads), lambda i: (0, i, 0))],
        out_specs=row, out_shape=jax.ShapeDtypeStruct((S, width), BF16),
        compiler_params=_params("parallel"), name="merge_b")(*outs, jnp.stack(lses, 0))
    return matmul(o, w_out.astype(BF16), F32)


def _c_bias_kernel(rpb_ref, o_ref, *, rq, n_dr, n_dc):
    h = pl.program_id(0)
    W = GRID_W
    qc = lax.broadcasted_iota(jnp.int32, (W, 2 * W), 0)
    lane = lax.broadcasted_iota(jnp.int32, (W, 2 * W), 1)
    kc = lane & (W - 1)
    dc = kc - qc
    c0 = jnp.clip(qc - C_WIN_C // 2, 0, W - C_WIN_C)
    ok_c = (kc >= c0) & (kc < c0 + C_WIN_C)
    masked = jnp.full((W, 2 * W), MASKED, F32)
    base = h * (n_dr * n_dc)
    sub = []
    for a in range(n_dr):
        val = masked
        for b in range(n_dc):
            val = jnp.where(dc == b - (C_WIN_C - 1), rpb_ref[base + a * n_dc + b], val)
        sub.append(jnp.where(ok_c, val, MASKED))
    for v in range(3):
        for qr in range(rq):
            for jv in range(3 * rq // 2):
                halves = []
                for kr in (2 * jv, 2 * jv + 1):
                    lo, hi = ((rq, 3 * rq - 1), (qr, qr + C_WIN_R - 1), (0, C_WIN_R - 1))[v]
                    halves.append(sub[kr - qr + C_WIN_R - 1 - rq] if lo <= kr <= hi else masked)
                o_ref[v, 0, qr * W:(qr + 1) * W, jv * 2 * W:(jv + 1) * 2 * W] = jnp.where(
                    lane < W, halves[0], halves[1])


def mixer_c(h, w_in, w_out, rpb, heads_per_step=8):
    S, D = h.shape
    n_heads = D // HEAD_DIM
    rows = S // GRID_W
    rq = C_WIN_R // 2
    tq = rq * GRID_W
    nblk = rows // rq
    assert rows >= C_WIN_R and nblk >= 2 and 2 * GRID_W == HEAD_DIM
    n_dr, n_dc = 2 * C_WIN_R - 1, 2 * C_WIN_C - 1
    colscale = jnp.concatenate([jnp.full((D,), SCALE, F32), jnp.ones((2 * D,), F32)])
    qkv = matmul(h, (w_in * colscale).astype(BF16), BF16)
    bias = pl.pallas_call(
        functools.partial(_c_bias_kernel, rq=rq, n_dr=n_dr, n_dc=n_dc), grid=(n_heads,),
        in_specs=[pl.BlockSpec(memory_space=pltpu.SMEM)],
        out_specs=pl.BlockSpec((3, 1, tq, 3 * tq), lambda hh: (0, hh, 0, 0)),
        out_shape=jax.ShapeDtypeStruct((3, n_heads, tq, 3 * tq), F32),
        compiler_params=_params("parallel"), name="c_bias")(rpb.reshape(-1))
    hs = heads_per_step
    n_hg = n_heads // hs
    width = hs * HEAD_DIM

    def spec(part, di):
        return pl.BlockSpec(
            (tq, width), lambda hg, i: (jnp.clip(i + di, 0, nblk - 1), part * n_hg + hg))

    o = pl.pallas_call(
        functools.partial(_win_attn_kernel, n_heads=hs, tq=tq), grid=(n_hg, nblk),
        in_specs=[spec(0, 0), spec(1, -1), spec(1, 0), spec(1, 1),
                  spec(2, -1), spec(2, 0), spec(2, 1),
                  pl.BlockSpec((1, hs, tq, 3 * tq),
                               lambda hg, i: (_edge_variant(i, nblk), hg, 0, 0))],
        out_specs=pl.BlockSpec((tq, width), lambda hg, i: (i, hg)),
        out_shape=jax.ShapeDtypeStruct((S, D), BF16),
        compiler_params=_params("parallel", "parallel"),
        name="attn_c")(qkv, qkv, qkv, qkv, qkv, qkv, qkv, bias)
    return matmul(o, w_out.astype(BF16), F32)


def kernel(x, rel_bias, norm_g, a_w_in, a_w_out, a_lambda, a_subln, b_w_in, b_w_out, c_w_in, c_w_out,
           c_rpb, d_w_in, d_w_out, d_qk_norm, ffn_w_gate, ffn_w_up, ffn_w_down):
    B, S, D = x.shape
    assert B == 1
    depth = norm_g.shape[0]
    xs = x.reshape(S, D)
    hn = prenorm(xs, norm_g[0, 0])
    for i in range(depth):
        m, j = i % N_MIXERS, i // N_MIXERS
        if m == 0:
            y = mixer_a(hn, a_w_in[j], a_w_out[j], a_lambda[j], a_subln[j], rel_bias, i)
        elif m == 1:
            y = mixer_b(hn, b_w_in[j], b_w_out[j], rel_bias)
        elif m == 2:
            y = mixer_c(hn, c_w_in[j], c_w_out[j], c_rpb[j])
        else:
            y = mixer_d(hn, d_w_in[j], d_w_out[j], d_qk_norm[j])
        xs, hn = resid(xs, y, norm_g[i, 1], norm_g[i, 2])
        y = ffn(hn, ffn_w_gate[i], ffn_w_up[i], ffn_w_down[i])
        if i + 1 < depth:
            xs, hn = resid(xs, y, norm_g[i, 3], norm_g[i + 1, 0])
        else:
            xs = resid(xs, y, norm_g[i, 3])
    return xs.reshape(B, S, D)
```

```python
import functools
import math

import numpy as np
import jax
import jax.numpy as jnp
from jax import lax
from jax.experimental import pallas as pl
from jax.experimental.pallas import tpu as pltpu

F32 = jnp.float32
BF16 = jnp.bfloat16

HEAD_DIM = 128
SCALE = HEAD_DIM ** -0.5
LOG2E = math.log2(math.e)
GRID_W = 64
EPS = 1e-6
REL_BUCKETS = 32
REL_MAX_DIST = 1024
B_GROUPS = ((128, 1), (512, 4), (2048, 16))
C_WIN_R = 8
C_WIN_C = 16
D_KV_HEADS = 8
ROPE_THETA = 10000.0
N_MIXERS = 4

MASKED = -1e30
VMEM_LIMIT_BYTES = 56 * 2 ** 20


def _params(*semantics):
    return pltpu.CompilerParams(dimension_semantics=semantics, vmem_limit_bytes=VMEM_LIMIT_BYTES)


def _rms(x, g):
    ms = jnp.mean(x * x, axis=-1, keepdims=True)
    return x * lax.rsqrt(ms + EPS) * g


def _dot_nt(a, b):
    return lax.dot_general(a, b, (((1,), (1,)), ((), ())), preferred_element_type=F32)


def _prenorm_kernel(x_ref, g_ref, h_ref):
    h_ref[...] = _rms(x_ref[...], g_ref[...]).astype(h_ref.dtype)


def prenorm(x, g, tm=256):
    S, D = x.shape
    row = pl.BlockSpec((tm, D), lambda i: (i, 0))
    vec = pl.BlockSpec((1, D), lambda i: (0, 0))
    return pl.pallas_call(
        _prenorm_kernel, grid=(S // tm,), in_specs=[row, vec], out_specs=row,
        out_shape=jax.ShapeDtypeStruct((S, D), BF16), compiler_params=_params("parallel"),
        name="prenorm")(x, g.reshape(1, D))


def _resid_kernel(x_ref, y_ref, g1_ref, g2_ref, xo_ref, h_ref):
    x = x_ref[...] + _rms(y_ref[...], g1_ref[...])
    xo_ref[...] = x
    h_ref[...] = _rms(x, g2_ref[...]).astype(h_ref.dtype)


def _resid_last_kernel(x_ref, y_ref, g1_ref, xo_ref):
    xo_ref[...] = x_ref[...] + _rms(y_ref[...], g1_ref[...])


def resid(x, y, g_post, g_next=None, tm=256):
    S, D = x.shape
    row = pl.BlockSpec((tm, D), lambda i: (i, 0))
    vec = pl.BlockSpec((1, D), lambda i: (0, 0))
    if g_next is None:
        return pl.pallas_call(
            _resid_last_kernel, grid=(S // tm,), in_specs=[row, row, vec], out_specs=row,
            out_shape=jax.ShapeDtypeStruct((S, D), F32), compiler_params=_params("parallel"),
            name="resid_last")(x, y, g_post.reshape(1, D))
    return pl.pallas_call(
        _resid_kernel, grid=(S // tm,), in_specs=[row, row, vec, vec], out_specs=[row, row],
        out_shape=[jax.ShapeDtypeStruct((S, D), F32), jax.ShapeDtypeStruct((S, D), BF16)],
        compiler_params=_params("parallel"),
        name="resid")(x, y, g_post.reshape(1, D), g_next.reshape(1, D))


def _mm_kernel(a_ref, b_ref, o_ref):
    o_ref[...] = jnp.dot(a_ref[...], b_ref[...], preferred_element_type=F32).astype(o_ref.dtype)


def _mm_acc_kernel(a_ref, b_ref, o_ref, acc_ref):
    k = pl.program_id(2)

    @pl.when(k == 0)
    def _():
        acc_ref[...] = jnp.zeros_like(acc_ref)

    acc_ref[...] += jnp.dot(a_ref[...], b_ref[...], preferred_element_type=F32)

    @pl.when(k == pl.num_programs(2) - 1)
    def _():
        o_ref[...] = acc_ref[...].astype(o_ref.dtype)


def matmul(a, b, out_dtype, tm=1024, tn=512, tk=None):
    M, K = a.shape
    N = b.shape[1]
    tk = K if tk is None else tk
    out_shape = jax.ShapeDtypeStruct((M, N), out_dtype)
    if tk == K:
        return pl.pallas_call(
            _mm_kernel, grid=(M // tm, N // tn),
            in_specs=[pl.BlockSpec((tm, K), lambda i, j: (i, 0)),
                      pl.BlockSpec((K, tn), lambda i, j: (0, j))],
            out_specs=pl.BlockSpec((tm, tn), lambda i, j: (i, j)),
            out_shape=out_shape, compiler_params=_params("parallel", "parallel"),
            name="matmul")(a, b)
    return pl.pallas_call(
        _mm_acc_kernel, grid=(M // tm, N // tn, K // tk),
        in_specs=[pl.BlockSpec((tm, tk), lambda i, j, k: (i, k)),
                  pl.BlockSpec((tk, tn), lambda i, j, k: (k, j))],
        out_specs=pl.BlockSpec((tm, tn), lambda i, j, k: (i, j)),
        out_shape=out_shape, scratch_shapes=[pltpu.VMEM((tm, tn), F32)],
        compiler_params=_params("parallel", "parallel", "arbitrary"),
        name="matmul_acc")(a, b)


def _glu_kernel(a_ref, wg_ref, wu_ref, o_ref):
    a = a_ref[...]
    g = jnp.dot(a, wg_ref[...], preferred_element_type=F32)
    u = jnp.dot(a, wu_ref[...], preferred_element_type=F32)
    o_ref[...] = (g * (1.0 / (1.0 + jnp.exp(-g))) * u).astype(o_ref.dtype)


def glu_up(a, wg, wu, tm=1024, tn=256):
    M, K = a.shape
    N = wg.shape[1]
    wspec = pl.BlockSpec((K, tn), lambda i, j: (0, j))
    return pl.pallas_call(
        _glu_kernel, grid=(M // tm, N // tn),
        in_specs=[pl.BlockSpec((tm, K), lambda i, j: (i, 0)), wspec, wspec],
        out_specs=pl.BlockSpec((tm, tn), lambda i, j: (i, j)),
        out_shape=jax.ShapeDtypeStruct((M, N), BF16),
        compiler_params=_params("parallel", "parallel"), name="glu_up")(a, wg, wu)


def ffn(h, w_gate, w_up, w_down):
    hid = glu_up(h, w_gate.astype(BF16), w_up.astype(BF16))
    d_ff = hid.shape[1]
    tk = d_ff // 2 if (d_ff // 2) % HEAD_DIM == 0 else d_ff
    return matmul(hid, w_down.astype(BF16), F32, tm=1024, tn=512, tk=tk)


def _t5_thresholds():
    nb = REL_BUCKETS // 2
    max_exact = nb // 2
    n = np.arange(0, 4 * REL_MAX_DIST)
    nf = np.maximum(n, 1).astype(np.float32)
    large = max_exact + (np.log(nf / np.float32(max_exact))
                         / np.float32(math.log(REL_MAX_DIST / max_exact))
                         * np.float32(nb - max_exact)).astype(np.int32)
    bucket = np.where(n < max_exact, n, np.minimum(large, nb - 1))
    assert np.all(np.diff(bucket) >= 0) and bucket[-1] == nb - 1
    return tuple(int(np.argmax(bucket >= b)) for b in range(1, nb))


T5_THRESHOLDS = _t5_thresholds()


def _t5_tile_kernel(rb_ref, o_ref, *, offs, dil, half, col_lo, col_hi, n_heads, mult):
    h = pl.program_id(0)
    rows, cols = o_ref.shape[2], o_ref.shape[3]
    i = lax.broadcasted_iota(jnp.int32, (rows, cols), 0)
    j = lax.broadcasted_iota(jnp.int32, (rows, cols), 1)
    nb = REL_BUCKETS // 2
    for v, off in enumerate(offs):
        delta = j - i + off
        rel = delta * dil
        n = jnp.abs(rel)
        vneg = jnp.full((rows, cols), rb_ref[h], F32)
        vpos = jnp.full((rows, cols), rb_ref[nb * n_heads + h], F32)
        for b, t in enumerate(T5_THRESHOLDS, start=1):
            c = n >= t
            vneg = jnp.where(c, rb_ref[b * n_heads + h], vneg)
            vpos = jnp.where(c, rb_ref[(nb + b) * n_heads + h], vpos)
        val = jnp.where(rel > 0, vpos, vneg)
        if mult != 1.0:
            val = val * mult
        if half is not None:
            ok = (jnp.abs(delta) <= half) & (j >= col_lo[v]) & (j < col_hi[v])
            val = jnp.where(ok, val, MASKED)
        o_ref[v, 0] = val


def t5_tiles(rel_bias, offs, rows, cols, dil=1, half=None, col_lo=None, col_hi=None, mult=1.0):
    n_heads = rel_bias.shape[1]
    nv = len(offs)
    kern = functools.partial(_t5_tile_kernel, offs=tuple(offs), dil=dil, half=half,
                             col_lo=col_lo, col_hi=col_hi, n_heads=n_heads, mult=mult)
    return pl.pallas_call(
        kern, grid=(n_heads,),
        in_specs=[pl.BlockSpec(memory_space=pltpu.SMEM)],
        out_specs=pl.BlockSpec((nv, 1, rows, cols), lambda h: (0, h, 0, 0)),
        out_shape=jax.ShapeDtypeStruct((nv, n_heads, rows, cols), F32),
        compiler_params=_params("parallel"), name="t5_tiles")(rel_bias.reshape(-1))


def _lane_tiles(x):
    return [x[:, c * HEAD_DIM:(c + 1) * HEAD_DIM] for c in range(x.shape[1] // HEAD_DIM)]


def _online_step(m_sc, l_sc, acc_sc, idx, s_sc, v):
    m_old = m_sc[idx]
    m_new = jnp.maximum(m_old, jnp.max(s_sc[idx], axis=-1, keepdims=True))
    m_sc[idx] = m_new
    alpha = jnp.exp2(m_old - m_new)
    p = jnp.exp2(s_sc[idx] - jnp.tile(m_new, (1, s_sc.shape[2] // HEAD_DIM)))
    if l_sc is not None:
        l_sc[idx] = alpha * l_sc[idx] + functools.reduce(lambda a, b: a + b, _lane_tiles(p))
    pv = jnp.dot(p.astype(v.dtype), v, preferred_element_type=F32)
    acc_sc[idx] = jnp.tile(alpha, (1, pv.shape[1] // HEAD_DIM)) * acc_sc[idx] + pv


def _row_sum(l):
    return jnp.sum(l, axis=-1, keepdims=True)


def _flash_init(m_sc, l_sc, acc_sc):
    m_sc[...] = jnp.full(m_sc.shape, -jnp.inf, F32)
    if l_sc is not None:
        l_sc[...] = jnp.zeros(l_sc.shape, F32)
    acc_sc[...] = jnp.zeros(acc_sc.shape, F32)


def _pipelined_blocks(nk, scores, consume, s0_sc, s1_sc):
    assert nk % 2 == 0
    scores(0, s0_sc)

    def body(j, carry):
        kb = 2 * j
        scores(kb + 1, s1_sc)
        consume(kb, s0_sc)
        scores(kb + 2, s0_sc)
        consume(kb + 1, s1_sc)
        return carry

    lax.fori_loop(0, nk // 2 - 1, body, 0)
    scores(nk - 1, s1_sc)
    consume(nk - 2, s0_sc)
    consume(nk - 1, s1_sc)


def _attn_a_kernel(q_ref, k_ref, v_ref, band_ref, lam_ref, g_ref, o_ref, m_sc, l_sc, acc_sc,
                   s0_sc, s1_sc, *, T, nk, band, lam_init):
    qb = pl.program_id(1)
    _flash_init(m_sc, l_sc, acc_sc)

    def scores(kb, s_sc):
        off = pl.multiple_of(kb * T, T)
        bidx = jnp.clip(kb - qb + band, 0, 2 * band)
        for m in range(2):
            sl = slice(m * HEAD_DIM, (m + 1) * HEAD_DIM)
            s_sc[m] = _dot_nt(q_ref[:, sl], k_ref[pl.ds(off, T), sl]) + band_ref[bidx, 0]

    def consume(kb, s_sc):
        vblk = v_ref[pl.ds(pl.multiple_of(kb * T, T), T), :]
        for m in range(2):
            _online_step(m_sc, l_sc, acc_sc, m, s_sc, vblk)

    _pipelined_blocks(nk, scores, consume, s0_sc, s1_sc)
    lp = lam_ref[...]
    lam = (jnp.exp(jnp.sum(lp[0:1] * lp[1:2], axis=-1, keepdims=True))
           - jnp.exp(jnp.sum(lp[2:3] * lp[3:4], axis=-1, keepdims=True)) + lam_init)
    o = acc_sc[0] / _row_sum(l_sc[0]) - lam * (acc_sc[1] / _row_sum(l_sc[1]))
    o_ref[...] = (_rms(o, g_ref[...]) * (1.0 - lam_init)).astype(o_ref.dtype)


def mixer_a(h, w_in, w_out, lam_p, subln_g, rel_bias, layer_idx, T=512):
    S, D = h.shape
    n_heads = D // (2 * HEAD_DIM)
    width = 2 * HEAD_DIM
    nk = S // T
    colscale = jnp.concatenate([jnp.full((D,), SCALE * LOG2E, F32), jnp.ones((2 * D,), F32)])
    qkv = matmul(h, (w_in * colscale).astype(BF16), BF16)
    far = T5_THRESHOLDS[-1]
    band = -(-(far - 1) // T) + 1
    tiles = t5_tiles(rel_bias, [(d - band) * T for d in range(2 * band + 1)], T, T, mult=LOG2E)
    lam_init = 0.8 - 0.6 * math.exp(-0.3 * layer_idx)
    kern = functools.partial(_attn_a_kernel, T=T, nk=nk, band=band, lam_init=lam_init)
    once = pl.Buffered(1)
    o = pl.pallas_call(
        kern, grid=(n_heads, nk),
        in_specs=[pl.BlockSpec((T, width), lambda hh, qb: (qb, hh)),
                  pl.BlockSpec((S, width), lambda hh, qb: (0, n_heads + hh), pipeline_mode=once),
                  pl.BlockSpec((S, width), lambda hh, qb: (0, 2 * n_heads + hh), pipeline_mode=once),
                  pl.BlockSpec((2 * band + 1, 1, T, T), lambda hh, qb: (0, hh, 0, 0), pipeline_mode=once),
                  pl.BlockSpec((4, HEAD_DIM), lambda hh, qb: (0, 0)),
                  pl.BlockSpec((1, width), lambda hh, qb: (0, 0))],
        out_specs=pl.BlockSpec((T, width), lambda hh, qb: (qb, hh)),
        out_shape=jax.ShapeDtypeStruct((S, D), BF16),
        scratch_shapes=[pltpu.VMEM((2, T, HEAD_DIM), F32), pltpu.VMEM((2, T, HEAD_DIM), F32),
                        pltpu.VMEM((2, T, width), F32),
                        pltpu.VMEM((2, T, T), F32), pltpu.VMEM((2, T, T), F32)],
        compiler_params=_params("parallel", "arbitrary"),
        name="attn_a")(qkv, qkv, qkv, tiles, lam_p, subln_g.reshape(1, width))
    return matmul(o, w_out.astype(BF16), F32)


def _rope_kernel(x_ref, c_ref, sa_ref, sb_ref, g_ref, o_ref, *, n_q, n_chunks):
    c, sa, sb = c_ref[...], sa_ref[...], sb_ref[...]
    for ch in range(n_chunks):
        sl = slice(ch * HEAD_DIM, (ch + 1) * HEAD_DIM)
        y = _rms(x_ref[:, sl], g_ref[0:1] if ch < n_q else g_ref[1:2])
        y = y * c + pltpu.roll(y, HEAD_DIM - 1, 1) * sa + pltpu.roll(y, 1, 1) * sb
        if ch < n_q:
            y = y * (SCALE * LOG2E)
        o_ref[:, sl] = y.astype(o_ref.dtype)


def _rope_tables(S):
    pos = jnp.arange(S)
    n_freq = HEAD_DIM // 4
    freqs = ROPE_THETA ** (-jnp.arange(n_freq, dtype=F32) / n_freq)
    ang = jnp.concatenate([(pos // GRID_W).astype(F32)[:, None] * freqs,
                           (pos % GRID_W).astype(F32)[:, None] * freqs], axis=-1)
    cos, sin = jnp.cos(ang), jnp.sin(ang)
    zero = jnp.zeros_like(sin)
    c = jnp.stack([cos, cos], axis=-1).reshape(S, HEAD_DIM)
    sa = jnp.stack([-sin, zero], axis=-1).reshape(S, HEAD_DIM)
    sb = jnp.stack([zero, sin], axis=-1).reshape(S, HEAD_DIM)
    return c, sa, sb


def _attn_d_kernel(q_ref, k_ref, v_ref, o_ref, m_sc, acc_sc, s0_sc, s1_sc, *, T, nk, rep):
    _flash_init(m_sc, None, acc_sc)

    def scores(kb, s_sc):
        kblk = k_ref[pl.ds(pl.multiple_of(kb * T, T), T), :]
        for r in range(rep):
            s_sc[r] = _dot_nt(q_ref[:, r * HEAD_DIM:(r + 1) * HEAD_DIM], kblk)

    def consume(kb, s_sc):
        vblk = v_ref[pl.ds(pl.multiple_of(kb * T, T), T), :]
        for r in range(rep):
            _online_step(m_sc, None, acc_sc, r, s_sc, vblk)

    _pipelined_blocks(nk, scores, consume, s0_sc, s1_sc)
    for r in range(rep):
        acc = acc_sc[r]
        o_ref[:, r * HEAD_DIM:(r + 1) * HEAD_DIM] = (
            acc[:, :HEAD_DIM] / acc[:, HEAD_DIM:]).astype(o_ref.dtype)


def mixer_d(h, w_in, w_out, qk_g, T=512, tm=256):
    S, D = h.shape
    n_q = D // HEAD_DIM
    n_kv = D_KV_HEADS
    rep = n_q // n_kv
    nqk = (n_q + n_kv) * HEAD_DIM
    qk = matmul(h, w_in[:, :nqk].astype(BF16), F32)
    v = matmul(h, w_in[:, nqk:].astype(BF16), BF16)
    v_ones = jnp.concatenate([v.reshape(S, n_kv, HEAD_DIM), jnp.ones((S, n_kv, HEAD_DIM), BF16)],
                             axis=-1).reshape(S, n_kv * 2 * HEAD_DIM)
    c, sa, sb = _rope_tables(S)
    tab = pl.BlockSpec((tm, HEAD_DIM), lambda i: (i, 0))
    qk = pl.pallas_call(
        functools.partial(_rope_kernel, n_q=n_q, n_chunks=n_q + n_kv), grid=(S // tm,),
        in_specs=[pl.BlockSpec((tm, nqk), lambda i: (i, 0)), tab, tab, tab,
                  pl.BlockSpec((2, HEAD_DIM), lambda i: (0, 0))],
        out_specs=pl.BlockSpec((tm, nqk), lambda i: (i, 0)),
        out_shape=jax.ShapeDtypeStruct((S, nqk), BF16),
        compiler_params=_params("parallel"), name="qk_norm_rope")(qk, c, sa, sb, qk_g)
    nk = S // T
    kern = functools.partial(_attn_d_kernel, T=T, nk=nk, rep=rep)
    o = pl.pallas_call(
        kern, grid=(n_kv, nk),
        in_specs=[pl.BlockSpec((T, rep * HEAD_DIM), lambda g, qb: (qb, g)),
                  pl.BlockSpec((S, HEAD_DIM), lambda g, qb: (0, n_q + g)),
                  pl.BlockSpec((S, 2 * HEAD_DIM), lambda g, qb: (0, g))],
        out_specs=pl.BlockSpec((T, rep * HEAD_DIM), lambda g, qb: (qb, g)),
        out_shape=jax.ShapeDtypeStruct((S, D), BF16),
        scratch_shapes=[pltpu.VMEM((rep, T, HEAD_DIM), F32), pltpu.VMEM((rep, T, 2 * HEAD_DIM), F32),
                        pltpu.VMEM((rep, T, T), F32), pltpu.VMEM((rep, T, T), F32)],
        compiler_params=_params("parallel", "arbitrary"), name="attn_d")(qk, qk, v_ones)
    return matmul(o, w_out.astype(BF16), F32)


def _win_attn_kernel(q_ref, kp_ref, kc_ref, kn_ref, vp_ref, vc_ref, vn_ref, bias_ref, o_ref,
                     *lse_ref, n_heads, tq):
    lse_cols = []
    for h in range(n_heads):
        sl = slice(h * HEAD_DIM, (h + 1) * HEAD_DIM)
        q = q_ref[:, sl]
        ks = (kp_ref[:, sl], kc_ref[:, sl], kn_ref[:, sl])
        vs = (vp_ref[:, sl], vc_ref[:, sl], vn_ref[:, sl])
        ss = [_dot_nt(q, ks[j]) + bias_ref[0, h, :, j * tq:(j + 1) * tq] for j in range(3)]
        m = jnp.maximum(jnp.maximum(jnp.max(ss[0], axis=-1, keepdims=True),
                                    jnp.max(ss[1], axis=-1, keepdims=True)),
                        jnp.max(ss[2], axis=-1, keepdims=True))
        ps = [jnp.exp(s - m) for s in ss]
        l = (jnp.sum(ps[0], axis=-1, keepdims=True) + jnp.sum(ps[1], axis=-1, keepdims=True)
             + jnp.sum(ps[2], axis=-1, keepdims=True))
        acc = (jnp.dot(ps[0].astype(BF16), vs[0], preferred_element_type=F32)
               + jnp.dot(ps[1].astype(BF16), vs[1], preferred_element_type=F32)
               + jnp.dot(ps[2].astype(BF16), vs[2], preferred_element_type=F32))
        o_ref[:, sl] = (acc / l).astype(o_ref.dtype)
        lse_cols.append(m + jnp.log(l))
    if lse_ref:
        lse_ref[0][...] = jnp.concatenate(lse_cols, axis=1)


def _edge_variant(i, nblk):
    return jnp.where(i == 0, 0, jnp.where(i == nblk - 1, 2, 1))


def _merge_b_kernel(o0_ref, o1_ref, o2_ref, lse_ref, out_ref, *, n_heads):
    lse = lse_ref[...]
    w = jnp.exp(lse - jnp.max(lse, axis=0, keepdims=True))
    w = w / jnp.sum(w, axis=0, keepdims=True)
    for h in range(n_heads):
        sl = slice(h * HEAD_DIM, (h + 1) * HEAD_DIM)
        out_ref[:, sl] = (w[0, :, h:h + 1] * o0_ref[:, sl] + w[1, :, h:h + 1] * o1_ref[:, sl]
                          + w[2, :, h:h + 1] * o2_ref[:, sl]).astype(out_ref.dtype)


def mixer_b(h, w_in, w_out, rel_bias, tq=128, tm=256):
    S, D = h.shape
    G = len(B_GROUPS)
    n_heads = rel_bias.shape[1]
    width = n_heads * HEAD_DIM
    colscale = jnp.concatenate([jnp.full((G * width,), SCALE, F32), jnp.ones((2 * G * width,), F32)])
    w = (w_in * colscale).astype(BF16).reshape(D, 3, G, width)
    outs, lses = [], []
    for g, (window, dil) in enumerate(B_GROUPS):
        half = window // 2 // dil
        assert half <= tq
        L = S // dil
        nblk = L // tq
        assert nblk >= 2

        def by_residue(a, dil=dil, L=L):
            return a if dil == 1 else a.reshape(L, dil, -1).transpose(1, 0, 2).reshape(S, -1)

        def by_position(a, dil=dil, L=L):
            return a if dil == 1 else a.reshape(dil, L, -1).transpose(1, 0, 2).reshape(S, -1)

        qkv = matmul(by_residue(h), w[:, :, g].reshape(D, 3 * width), BF16)
        bias = t5_tiles(rel_bias, [-tq] * 3, tq, 3 * tq, dil=dil, half=half,
                        col_lo=(tq, 0, 0), col_hi=(3 * tq, 3 * tq, 2 * tq))

        def spec(part, di, nblk=nblk):
            return pl.BlockSpec(
                (tq, width), lambda r, i: (r * nblk + jnp.clip(i + di, 0, nblk - 1), part))

        o_g, lse_g = pl.pallas_call(
            functools.partial(_win_attn_kernel, n_heads=n_heads, tq=tq), grid=(dil, nblk),
            in_specs=[spec(0, 0), spec(1, -1), spec(1, 0), spec(1, 1),
                      spec(2, -1), spec(2, 0), spec(2, 1),
                      pl.BlockSpec((1, n_heads, tq, 3 * tq),
                                   lambda r, i, nblk=nblk: (_edge_variant(i, nblk), 0, 0, 0))],
            out_specs=[pl.BlockSpec((tq, width), lambda r, i, nblk=nblk: (r * nblk + i, 0)),
                       pl.BlockSpec((tq, n_heads), lambda r, i, nblk=nblk: (r * nblk + i, 0))],
            out_shape=[jax.ShapeDtypeStruct((S, width), F32),
                       jax.ShapeDtypeStruct((S, n_heads), F32)],
            compiler_params=_params("parallel", "parallel"),
            name=f"attn_b{g}")(qkv, qkv, qkv, qkv, qkv, qkv, qkv, bias)
        outs.append(by_position(o_g))
        lses.append(by_position(lse_g))
    row = pl.BlockSpec((tm, width), lambda i: (i, 0))
    o = pl.pallas_call(
        functools.partial(_merge_b_kernel, n_heads=n_heads), grid=(S // tm,),
        in_specs=[row, row, row, pl.BlockSpec((G, tm, n_heads), lambda i: (0, i, 0))],
        out_specs=row, out_shape=jax.ShapeDtypeStruct((S, width), BF16),
        compiler_params=_params("parallel"), name="merge_b")(*outs, jnp.stack(lses, 0))
    return matmul(o, w_out.astype(BF16), F32)


def _c_bias_kernel(rpb_ref, o_ref, *, rq, n_dr, n_dc):
    h = pl.program_id(0)
    W = GRID_W
    qc = lax.broadcasted_iota(jnp.int32, (W, 2 * W), 0)
    lane = lax.broadcasted_iota(jnp.int32, (W, 2 * W), 1)
    kc = lane & (W - 1)
    dc = kc - qc
    c0 = jnp.clip(qc - C_WIN_C // 2, 0, W - C_WIN_C)
    ok_c = (kc >= c0) & (kc < c0 + C_WIN_C)
    masked = jnp.full((W, 2 * W), MASKED, F32)
    base = h * (n_dr * n_dc)
    sub = []
    for a in range(n_dr):
        val = masked
        for b in range(n_dc):
            val = jnp.where(dc == b - (C_WIN_C - 1), rpb_ref[base + a * n_dc + b], val)
        sub.append(jnp.where(ok_c, val, MASKED))
    for v in range(3):
        for qr in range(rq):
            for jv in range(3 * rq // 2):
                halves = []
                for kr in (2 * jv, 2 * jv + 1):
                    lo, hi = ((rq, 3 * rq - 1), (qr, qr + C_WIN_R - 1), (0, C_WIN_R - 1))[v]
                    halves.append(sub[kr - qr + C_WIN_R - 1 - rq] if lo <= kr <= hi else masked)
                o_ref[v, 0, qr * W:(qr + 1) * W, jv * 2 * W:(jv + 1) * 2 * W] = jnp.where(
                    lane < W, halves[0], halves[1])


def mixer_c(h, w_in, w_out, rpb, heads_per_step=8):
    S, D = h.shape
    n_heads = D // HEAD_DIM
    rows = S // GRID_W
    rq = C_WIN_R // 2
    tq = rq * GRID_W
    nblk = rows // rq
    assert rows >= C_WIN_R and nblk >= 2 and 2 * GRID_W == HEAD_DIM
    n_dr, n_dc = 2 * C_WIN_R - 1, 2 * C_WIN_C - 1
    colscale = jnp.concatenate([jnp.full((D,), SCALE, F32), jnp.ones((2 * D,), F32)])
    qkv = matmul(h, (w_in * colscale).astype(BF16), BF16)
    bias = pl.pallas_call(
        functools.partial(_c_bias_kernel, rq=rq, n_dr=n_dr, n_dc=n_dc), grid=(n_heads,),
        in_specs=[pl.BlockSpec(memory_space=pltpu.SMEM)],
        out_specs=pl.BlockSpec((3, 1, tq, 3 * tq), lambda hh: (0, hh, 0, 0)),
        out_shape=jax.ShapeDtypeStruct((3, n_heads, tq, 3 * tq), F32),
        compiler_params=_params("parallel"), name="c_bias")(rpb.reshape(-1))
    hs = heads_per_step
    n_hg = n_heads // hs
    width = hs * HEAD_DIM

    def spec(part, di):
        return pl.BlockSpec(
            (tq, width), lambda hg, i: (jnp.clip(i + di, 0, nblk - 1), part * n_hg + hg))

    o = pl.pallas_call(
        functools.partial(_win_attn_kernel, n_heads=hs, tq=tq), grid=(n_hg, nblk),
        in_specs=[spec(0, 0), spec(1, -1), spec(1, 0), spec(1, 1),
                  spec(2, -1), spec(2, 0), spec(2, 1),
                  pl.BlockSpec((1, hs, tq, 3 * tq),
                               lambda hg, i: (_edge_variant(i, nblk), hg, 0, 0))],
        out_specs=pl.BlockSpec((tq, width), lambda hg, i: (i, hg)),
        out_shape=jax.ShapeDtypeStruct((S, D), BF16),
        compiler_params=_params("parallel", "parallel"),
        name="attn_c")(qkv, qkv, qkv, qkv, qkv, qkv, qkv, bias)
    return matmul(o, w_out.astype(BF16), F32)


def kernel(x, rel_bias, norm_g, a_w_in, a_w_out, a_lambda, a_subln, b_w_in, b_w_out, c_w_in, c_w_out,
           c_rpb, d_w_in, d_w_out, d_qk_norm, ffn_w_gate, ffn_w_up, ffn_w_down):
    B, S, D = x.shape
    assert B == 1
    depth = norm_g.shape[0]
    xs = x.reshape(S, D)
    hn = prenorm(xs, norm_g[0, 0])
    for i in range(depth):
        m, j = i % N_MIXERS, i // N_MIXERS
        if m == 0:
            y = mixer_a(hn, a_w_in[j], a_w_out[j], a_lambda[j], a_subln[j], rel_bias, i)
        elif m == 1:
            y = mixer_b(hn, b_w_in[j], b_w_out[j], rel_bias)
        elif m == 2:
            y = mixer_c(hn, c_w_in[j], c_w_out[j], c_rpb[j])
        else:
            y = mixer_d(hn, d_w_in[j], d_w_out[j], d_qk_norm[j])
        xs, hn = resid(xs, y, norm_g[i, 1], norm_g[i, 2])
        y = ffn(hn, ffn_w_gate[i], ffn_w_up[i], ffn_w_down[i])
        if i + 1 < depth:
            xs, hn = resid(xs, y, norm_g[i, 3], norm_g[i + 1, 0])
        else:
            xs = resid(xs, y, norm_g[i, 3])
    return xs.reshape(B, S, D)
```

```python
import functools
import math

import numpy as np
import jax
import jax.numpy as jnp
from jax import lax
from jax.experimental import pallas as pl
from jax.experimental.pallas import tpu as pltpu

F32 = jnp.float32
BF16 = jnp.bfloat16

HEAD_DIM = 128
SCALE = HEAD_DIM ** -0.5
LOG2E = math.log2(math.e)
GRID_W = 64
EPS = 1e-6
REL_BUCKETS = 32
REL_MAX_DIST = 1024
B_GROUPS = ((128, 1), (512, 4), (2048, 16))
C_WIN_R = 8
C_WIN_C = 16
D_KV_HEADS = 8
ROPE_THETA = 10000.0
N_MIXERS = 4

KEY_CHUNK = 256
MASKED = -1e30
VMEM_LIMIT_BYTES = 56 * 2 ** 20


def _params(*semantics):
    return pltpu.CompilerParams(dimension_semantics=semantics, vmem_limit_bytes=VMEM_LIMIT_BYTES)


def _rms(x, g):
    ms = jnp.mean(x * x, axis=-1, keepdims=True)
    return x * lax.rsqrt(ms + EPS) * g


def _dot_nt(a, b):
    return lax.dot_general(a, b, (((1,), (1,)), ((), ())), preferred_element_type=F32)


def _prenorm_kernel(x_ref, g_ref, h_ref):
    h_ref[...] = _rms(x_ref[...], g_ref[...]).astype(h_ref.dtype)


def prenorm(x, g, tm=256):
    S, D = x.shape
    row = pl.BlockSpec((tm, D), lambda i: (i, 0))
    vec = pl.BlockSpec((1, D), lambda i: (0, 0))
    return pl.pallas_call(
        _prenorm_kernel, grid=(S // tm,), in_specs=[row, vec], out_specs=row,
        out_shape=jax.ShapeDtypeStruct((S, D), BF16), compiler_params=_params("parallel"),
        name="prenorm")(x, g.reshape(1, D))


def _resid_kernel(x_ref, y_ref, g1_ref, g2_ref, xo_ref, h_ref):
    x = x_ref[...] + _rms(y_ref[...], g1_ref[...])
    xo_ref[...] = x
    h_ref[...] = _rms(x, g2_ref[...]).astype(h_ref.dtype)


def _resid_last_kernel(x_ref, y_ref, g1_ref, xo_ref):
    xo_ref[...] = x_ref[...] + _rms(y_ref[...], g1_ref[...])


def resid(x, y, g_post, g_next=None, tm=256):
    S, D = x.shape
    row = pl.BlockSpec((tm, D), lambda i: (i, 0))
    vec = pl.BlockSpec((1, D), lambda i: (0, 0))
    if g_next is None:
        return pl.pallas_call(
            _resid_last_kernel, grid=(S // tm,), in_specs=[row, row, vec], out_specs=row,
            out_shape=jax.ShapeDtypeStruct((S, D), F32), compiler_params=_params("parallel"),
            name="resid_last")(x, y, g_post.reshape(1, D))
    return pl.pallas_call(
        _resid_kernel, grid=(S // tm,), in_specs=[row, row, vec, vec], out_specs=[row, row],
        out_shape=[jax.ShapeDtypeStruct((S, D), F32), jax.ShapeDtypeStruct((S, D), BF16)],
        compiler_params=_params("parallel"),
        name="resid")(x, y, g_post.reshape(1, D), g_next.reshape(1, D))


def _mm_kernel(a_ref, b_ref, o_ref):
    o_ref[...] = jnp.dot(a_ref[...], b_ref[...], preferred_element_type=F32).astype(o_ref.dtype)


def _mm_acc_kernel(a_ref, b_ref, o_ref, acc_ref):
    k = pl.program_id(2)

    @pl.when(k == 0)
    def _():
        acc_ref[...] = jnp.zeros_like(acc_ref)

    acc_ref[...] += jnp.dot(a_ref[...], b_ref[...], preferred_element_type=F32)

    @pl.when(k == pl.num_programs(2) - 1)
    def _():
        o_ref[...] = acc_ref[...].astype(o_ref.dtype)


def matmul(a, b, out_dtype, tm=1024, tn=None, tk=None):
    M, K = a.shape
    N = b.shape[1]
    tk = K if tk is None else tk
    if tn is None:
        tn = 1024 if N % 1024 == 0 else 512
    out_shape = jax.ShapeDtypeStruct((M, N), out_dtype)
    if tk == K:
        return pl.pallas_call(
            _mm_kernel, grid=(M // tm, N // tn),
            in_specs=[pl.BlockSpec((tm, K), lambda i, j: (i, 0)),
                      pl.BlockSpec((K, tn), lambda i, j: (0, j))],
            out_specs=pl.BlockSpec((tm, tn), lambda i, j: (i, j)),
            out_shape=out_shape, compiler_params=_params("parallel", "parallel"),
            name="matmul")(a, b)
    return pl.pallas_call(
        _mm_acc_kernel, grid=(M // tm, N // tn, K // tk),
        in_specs=[pl.BlockSpec((tm, tk), lambda i, j, k: (i, k)),
                  pl.BlockSpec((tk, tn), lambda i, j, k: (k, j))],
        out_specs=pl.BlockSpec((tm, tn), lambda i, j, k: (i, j)),
        out_shape=out_shape, scratch_shapes=[pltpu.VMEM((tm, tn), F32)],
        compiler_params=_params("parallel", "parallel", "arbitrary"),
        name="matmul_acc")(a, b)


def _glu_kernel(a_ref, wg_ref, wu_ref, o_ref):
    a = a_ref[...]
    g = jnp.dot(a, wg_ref[...], preferred_element_type=F32)
    u = jnp.dot(a, wu_ref[...], preferred_element_type=F32)
    o_ref[...] = (g * (1.0 / (1.0 + jnp.exp(-g))) * u).astype(o_ref.dtype)


def glu_up(a, wg, wu, tm=2048, tn=256):
    M, K = a.shape
    N = wg.shape[1]
    wspec = pl.BlockSpec((K, tn), lambda i, j: (0, j))
    return pl.pallas_call(
        _glu_kernel, grid=(M // tm, N // tn),
        in_specs=[pl.BlockSpec((tm, K), lambda i, j: (i, 0)), wspec, wspec],
        out_specs=pl.BlockSpec((tm, tn), lambda i, j: (i, j)),
        out_shape=jax.ShapeDtypeStruct((M, N), BF16),
        compiler_params=_params("parallel", "parallel"), name="glu_up")(a, wg, wu)


def ffn(h, w_gate, w_up, w_down):
    hid = glu_up(h, w_gate.astype(BF16), w_up.astype(BF16))
    d_ff = hid.shape[1]
    tk = d_ff // 2 if (d_ff // 2) % HEAD_DIM == 0 else d_ff
    return matmul(hid, w_down.astype(BF16), F32, tm=1024, tn=512, tk=tk)


def _t5_thresholds():
    nb = REL_BUCKETS // 2
    max_exact = nb // 2
    n = np.arange(0, 4 * REL_MAX_DIST)
    nf = np.maximum(n, 1).astype(np.float32)
    large = max_exact + (np.log(nf / np.float32(max_exact))
                         / np.float32(math.log(REL_MAX_DIST / max_exact))
                         * np.float32(nb - max_exact)).astype(np.int32)
    bucket = np.where(n < max_exact, n, np.minimum(large, nb - 1))
    assert np.all(np.diff(bucket) >= 0) and bucket[-1] == nb - 1
    return tuple(int(np.argmax(bucket >= b)) for b in range(1, nb))


T5_THRESHOLDS = _t5_thresholds()


def _t5_tile_kernel(rb_ref, var_ref, o_ref, *, dil, half, n_heads, mult):
    h = pl.program_id(0)
    v = pl.program_id(1)
    off, col_lo, col_hi = var_ref[3 * v], var_ref[3 * v + 1], var_ref[3 * v + 2]
    rows, cols = o_ref.shape[2], o_ref.shape[3]
    i = lax.broadcasted_iota(jnp.int32, (rows, cols), 0)
    j = lax.broadcasted_iota(jnp.int32, (rows, cols), 1)
    nb = REL_BUCKETS // 2
    delta = j - i + off
    rel = delta * dil
    n = jnp.abs(rel)
    vneg = jnp.full((rows, cols), rb_ref[h], F32)
    vpos = jnp.full((rows, cols), rb_ref[nb * n_heads + h], F32)
    for b, t in enumerate(T5_THRESHOLDS, start=1):
        c = n >= t
        vneg = jnp.where(c, rb_ref[b * n_heads + h], vneg)
        vpos = jnp.where(c, rb_ref[(nb + b) * n_heads + h], vpos)
    val = jnp.where(rel > 0, vpos, vneg)
    if mult != 1.0:
        val = val * mult
    if half is not None:
        ok = (jnp.abs(delta) <= half) & (j >= col_lo) & (j < col_hi)
        val = jnp.where(ok, val, MASKED)
    o_ref[0, 0] = val


def t5_tiles(rel_bias, offs, rows, cols, dil=1, half=None, col_lo=None, col_hi=None, mult=1.0):
    n_heads = rel_bias.shape[1]
    nv = len(offs)
    col_lo = (0,) * nv if col_lo is None else col_lo
    col_hi = (cols,) * nv if col_hi is None else col_hi
    variants = jnp.asarray(np.stack([offs, col_lo, col_hi], axis=1).reshape(-1), jnp.int32)
    kern = functools.partial(_t5_tile_kernel, dil=dil, half=half, n_heads=n_heads, mult=mult)
    smem = pl.BlockSpec(memory_space=pltpu.SMEM)
    return pl.pallas_call(
        kern, grid=(n_heads, nv), in_specs=[smem, smem],
        out_specs=pl.BlockSpec((1, 1, rows, cols), lambda h, v: (v, h, 0, 0)),
        out_shape=jax.ShapeDtypeStruct((nv, n_heads, rows, cols), F32),
        compiler_params=_params("parallel", "parallel"), name="t5_tiles")(rel_bias.reshape(-1), variants)


def _lane_tiles(x):
    return [x[:, c * HEAD_DIM:(c + 1) * HEAD_DIM] for c in range(x.shape[1] // HEAD_DIM)]


def _store_scores(s_sc, mp_sc, idx, s):
    s_sc[idx] = s
    mp_sc[idx] = functools.reduce(jnp.maximum, _lane_tiles(s))


def _online_step(m_sc, l_sc, acc_sc, idx, s_sc, mp_sc, v):
    m_old = m_sc[idx]
    m_new = jnp.maximum(m_old, jnp.max(mp_sc[idx], axis=-1, keepdims=True))
    m_sc[idx] = m_new
    alpha = jnp.exp2(m_old - m_new)
    n_chunks = s_sc.shape[2] // KEY_CHUNK
    m_wide = jnp.tile(m_new, (1, KEY_CHUNK // HEAD_DIM))
    pv, lsum = None, None
    for c in range(n_chunks):
        ks = slice(c * KEY_CHUNK, (c + 1) * KEY_CHUNK)
        p = jnp.exp2(s_sc[idx, :, ks] - m_wide)
        if l_sc is not None:
            part = functools.reduce(lambda a, b: a + b, _lane_tiles(p))
            lsum = part if lsum is None else lsum + part
        d = jnp.dot(p.astype(v.dtype), v[ks], preferred_element_type=F32)
        pv = d if pv is None else pv + d
    if l_sc is not None:
        l_sc[idx] = alpha * l_sc[idx] + lsum
    acc_sc[idx] = jnp.tile(alpha, (1, pv.shape[1] // HEAD_DIM)) * acc_sc[idx] + pv


def _row_sum(l):
    return jnp.sum(l, axis=-1, keepdims=True)


def _flash_init(m_sc, l_sc, acc_sc):
    m_sc[...] = jnp.full(m_sc.shape, -jnp.inf, F32)
    if l_sc is not None:
        l_sc[...] = jnp.zeros(l_sc.shape, F32)
    acc_sc[...] = jnp.zeros(acc_sc.shape, F32)


def _pipelined_blocks(nk, scores, consume, buf0, buf1):
    assert nk % 2 == 0
    scores(0, *buf0)

    def body(j, carry):
        kb = 2 * j
        scores(kb + 1, *buf1)
        consume(kb, *buf0)
        scores(kb + 2, *buf0)
        consume(kb + 1, *buf1)
        return carry

    lax.fori_loop(0, nk // 2 - 1, body, 0)
    scores(nk - 1, *buf1)
    consume(nk - 2, *buf0)
    consume(nk - 1, *buf1)


def _attn_a_kernel(q_ref, k_ref, v_ref, band_ref, lam_ref, g_ref, o_ref, m_sc, l_sc, acc_sc,
                   s0_sc, s1_sc, mp0_sc, mp1_sc, *, T, nk, q_step, k_step, e_lo, e_hi, lam_init):
    qb = pl.program_id(1)
    _flash_init(m_sc, l_sc, acc_sc)

    def scores(kb, s_sc, mp_sc):
        off = pl.multiple_of(kb * T, T)
        bidx = jnp.clip(kb * k_step - qb * q_step, e_lo, e_hi) - e_lo
        for m in range(2):
            sl = slice(m * HEAD_DIM, (m + 1) * HEAD_DIM)
            _store_scores(s_sc, mp_sc, m,
                          _dot_nt(q_ref[:, sl], k_ref[pl.ds(off, T), sl]) + band_ref[bidx, 0])

    def consume(kb, s_sc, mp_sc):
        vblk = v_ref[pl.ds(pl.multiple_of(kb * T, T), T), :]
        for m in range(2):
            _online_step(m_sc, l_sc, acc_sc, m, s_sc, mp_sc, vblk)

    _pipelined_blocks(nk, scores, consume, (s0_sc, mp0_sc), (s1_sc, mp1_sc))
    lp = lam_ref[...]
    lam = (jnp.exp(jnp.sum(lp[0:1] * lp[1:2], axis=-1, keepdims=True))
           - jnp.exp(jnp.sum(lp[2:3] * lp[3:4], axis=-1, keepdims=True)) + lam_init)
    o = acc_sc[0] / _row_sum(l_sc[0]) - lam * (acc_sc[1] / _row_sum(l_sc[1]))
    o_ref[...] = (_rms(o, g_ref[...]) * (1.0 - lam_init)).astype(o_ref.dtype)


def mixer_a(h, w_in, w_out, lam_p, subln_g, rel_bias, layer_idx, T=512, Tk=1024):
    S, D = h.shape
    n_heads = D // (2 * HEAD_DIM)
    width = 2 * HEAD_DIM
    colscale = jnp.concatenate([jnp.full((D,), SCALE * LOG2E, F32), jnp.ones((2 * D,), F32)])
    qkv = matmul(h, (w_in * colscale).astype(BF16), BF16)
    far = T5_THRESHOLDS[-1]
    unit = math.gcd(T, Tk)
    e_hi = -(-(far + T - 1) // unit)
    e_lo = -(-(far + Tk - 1) // unit)
    e_lo = -e_lo
    tiles = t5_tiles(rel_bias, [e * unit for e in range(e_lo, e_hi + 1)], T, Tk, mult=LOG2E)
    n_tiles = e_hi - e_lo + 1
    lam_init = 0.8 - 0.6 * math.exp(-0.3 * layer_idx)
    kern = functools.partial(_attn_a_kernel, T=Tk, nk=S // Tk, q_step=T // unit, k_step=Tk // unit,
                             e_lo=e_lo, e_hi=e_hi, lam_init=lam_init)
    once = pl.Buffered(1)
    o = pl.pallas_call(
        kern, grid=(n_heads, S // T),
        in_specs=[pl.BlockSpec((T, width), lambda hh, qb: (qb, hh)),
                  pl.BlockSpec((S, width), lambda hh, qb: (0, n_heads + hh), pipeline_mode=once),
                  pl.BlockSpec((S, width), lambda hh, qb: (0, 2 * n_heads + hh), pipeline_mode=once),
                  pl.BlockSpec((n_tiles, 1, T, Tk), lambda hh, qb: (0, hh, 0, 0), pipeline_mode=once),
                  pl.BlockSpec((4, HEAD_DIM), lambda hh, qb: (0, 0)),
                  pl.BlockSpec((1, width), lambda hh, qb: (0, 0))],
        out_specs=pl.BlockSpec((T, width), lambda hh, qb: (qb, hh)),
        out_shape=jax.ShapeDtypeStruct((S, D), BF16),
        scratch_shapes=[pltpu.VMEM((2, T, HEAD_DIM), F32), pltpu.VMEM((2, T, HEAD_DIM), F32),
                        pltpu.VMEM((2, T, width), F32),
                        pltpu.VMEM((2, T, Tk), F32), pltpu.VMEM((2, T, Tk), F32),
                        pltpu.VMEM((2, T, HEAD_DIM), F32), pltpu.VMEM((2, T, HEAD_DIM), F32)],
        compiler_params=_params("parallel", "arbitrary"),
        name="attn_a")(qkv, qkv, qkv, tiles, lam_p, subln_g.reshape(1, width))
    return matmul(o, w_out.astype(BF16), F32)


def _rope_kernel(x_ref, c_ref, sa_ref, sb_ref, g_ref, o_ref, *, n_q, n_chunks):
    c, sa, sb = c_ref[...], sa_ref[...], sb_ref[...]
    for ch in range(n_chunks):
        sl = slice(ch * HEAD_DIM, (ch + 1) * HEAD_DIM)
        y = _rms(x_ref[:, sl], g_ref[0:1] if ch < n_q else g_ref[1:2])
        y = y * c + pltpu.roll(y, HEAD_DIM - 1, 1) * sa + pltpu.roll(y, 1, 1) * sb
        if ch < n_q:
            y = y * (SCALE * LOG2E)
        o_ref[:, sl] = y.astype(o_ref.dtype)


def _rope_tables(S):
    pos = jnp.arange(S)
    n_freq = HEAD_DIM // 4
    freqs = ROPE_THETA ** (-jnp.arange(n_freq, dtype=F32) / n_freq)
    ang = jnp.concatenate([(pos // GRID_W).astype(F32)[:, None] * freqs,
                           (pos % GRID_W).astype(F32)[:, None] * freqs], axis=-1)
    cos, sin = jnp.cos(ang), jnp.sin(ang)
    zero = jnp.zeros_like(sin)
    c = jnp.stack([cos, cos], axis=-1).reshape(S, HEAD_DIM)
    sa = jnp.stack([-sin, zero], axis=-1).reshape(S, HEAD_DIM)
    sb = jnp.stack([zero, sin], axis=-1).reshape(S, HEAD_DIM)
    return c, sa, sb


def _attn_d_kernel(q_ref, k_ref, v_ref, o_ref, m_sc, acc_sc, s0_sc, s1_sc, mp0_sc, mp1_sc,
                   *, T, nk, rep):
    _flash_init(m_sc, None, acc_sc)

    def scores(kb, s_sc, mp_sc):
        kblk = k_ref[pl.ds(pl.multiple_of(kb * T, T), T), :]
        for r in range(rep):
            _store_scores(s_sc, mp_sc, r, _dot_nt(q_ref[:, r * HEAD_DIM:(r + 1) * HEAD_DIM], kblk))

    def consume(kb, s_sc, mp_sc):
        vblk = v_ref[pl.ds(pl.multiple_of(kb * T, T), T), :]
        for r in range(rep):
            _online_step(m_sc, None, acc_sc, r, s_sc, mp_sc, vblk)

    _pipelined_blocks(nk, scores, consume, (s0_sc, mp0_sc), (s1_sc, mp1_sc))
    for r in range(rep):
        acc = acc_sc[r]
        o_ref[:, r * HEAD_DIM:(r + 1) * HEAD_DIM] = (
            acc[:, :HEAD_DIM] / acc[:, HEAD_DIM:]).astype(o_ref.dtype)


def mixer_d(h, w_in, w_out, qk_g, T=512, Tk=1024, tm=256):
    S, D = h.shape
    n_q = D // HEAD_DIM
    n_kv = D_KV_HEADS
    rep = n_q // n_kv
    nqk = (n_q + n_kv) * HEAD_DIM
    qk = matmul(h, w_in[:, :nqk].astype(BF16), F32)
    v = matmul(h, w_in[:, nqk:].astype(BF16), BF16)
    v_ones = jnp.concatenate([v.reshape(S, n_kv, HEAD_DIM), jnp.ones((S, n_kv, HEAD_DIM), BF16)],
                             axis=-1).reshape(S, n_kv * 2 * HEAD_DIM)
    c, sa, sb = _rope_tables(S)
    tab = pl.BlockSpec((tm, HEAD_DIM), lambda i: (i, 0))
    qk = pl.pallas_call(
        functools.partial(_rope_kernel, n_q=n_q, n_chunks=n_q + n_kv), grid=(S // tm,),
        in_specs=[pl.BlockSpec((tm, nqk), lambda i: (i, 0)), tab, tab, tab,
                  pl.BlockSpec((2, HEAD_DIM), lambda i: (0, 0))],
        out_specs=pl.BlockSpec((tm, nqk), lambda i: (i, 0)),
        out_shape=jax.ShapeDtypeStruct((S, nqk), BF16),
        compiler_params=_params("parallel"), name="qk_norm_rope")(qk, c, sa, sb, qk_g)
    kern = functools.partial(_attn_d_kernel, T=Tk, nk=S // Tk, rep=rep)
    o = pl.pallas_call(
        kern, grid=(n_kv, S // T),
        in_specs=[pl.BlockSpec((T, rep * HEAD_DIM), lambda g, qb: (qb, g)),
                  pl.BlockSpec((S, HEAD_DIM), lambda g, qb: (0, n_q + g)),
                  pl.BlockSpec((S, 2 * HEAD_DIM), lambda g, qb: (0, g))],
        out_specs=pl.BlockSpec((T, rep * HEAD_DIM), lambda g, qb: (qb, g)),
        out_shape=jax.ShapeDtypeStruct((S, D), BF16),
        scratch_shapes=[pltpu.VMEM((rep, T, HEAD_DIM), F32), pltpu.VMEM((rep, T, 2 * HEAD_DIM), F32),
                        pltpu.VMEM((rep, T, Tk), F32), pltpu.VMEM((rep, T, Tk), F32),
                        pltpu.VMEM((rep, T, HEAD_DIM), F32), pltpu.VMEM((rep, T, HEAD_DIM), F32)],
        compiler_params=_params("parallel", "arbitrary"), name="attn_d")(qk, qk, v_ones)
    return matmul(o, w_out.astype(BF16), F32)


def _win_attn_kernel(q_ref, kp_ref, kc_ref, kn_ref, vp_ref, vc_ref, vn_ref, bias_ref, o_ref,
                     *lse_ref, n_heads, tq):
    lse_cols = []
    for h in range(n_heads):
        sl = slice(h * HEAD_DIM, (h + 1) * HEAD_DIM)
        q = q_ref[:, sl]
        ks = (kp_ref[:, sl], kc_ref[:, sl], kn_ref[:, sl])
        vs = (vp_ref[:, sl], vc_ref[:, sl], vn_ref[:, sl])
        ss = [_dot_nt(q, ks[j]) + bias_ref[0, h, :, j * tq:(j + 1) * tq] for j in range(3)]
        m = jnp.maximum(jnp.maximum(jnp.max(ss[0], axis=-1, keepdims=True),
                                    jnp.max(ss[1], axis=-1, keepdims=True)),
                        jnp.max(ss[2], axis=-1, keepdims=True))
        ps = [jnp.exp(s - m) for s in ss]
        l = (jnp.sum(ps[0], axis=-1, keepdims=True) + jnp.sum(ps[1], axis=-1, keepdims=True)
             + jnp.sum(ps[2], axis=-1, keepdims=True))
        acc = (jnp.dot(ps[0].astype(BF16), vs[0], preferred_element_type=F32)
               + jnp.dot(ps[1].astype(BF16), vs[1], preferred_element_type=F32)
               + jnp.dot(ps[2].astype(BF16), vs[2], preferred_element_type=F32))
        o_ref[:, sl] = (acc / l).astype(o_ref.dtype)
        lse_cols.append(m + jnp.log(l))
    if lse_ref:
        lse_ref[0][...] = jnp.concatenate(lse_cols, axis=1)


def _edge_variant(i, nblk):
    return jnp.where(i == 0, 0, jnp.where(i == nblk - 1, 2, 1))


def _merge_b_kernel(o0_ref, o1_ref, o2_ref, lse_ref, out_ref, *, n_heads):
    lse = lse_ref[...]
    w = jnp.exp(lse - jnp.max(lse, axis=0, keepdims=True))
    w = w / jnp.sum(w, axis=0, keepdims=True)
    for h in range(n_heads):
        sl = slice(h * HEAD_DIM, (h + 1) * HEAD_DIM)
        out_ref[:, sl] = (w[0, :, h:h + 1] * o0_ref[:, sl] + w[1, :, h:h + 1] * o1_ref[:, sl]
                          + w[2, :, h:h + 1] * o2_ref[:, sl]).astype(out_ref.dtype)


def mixer_b(h, w_in, w_out, rel_bias, tq=128, tm=256):
    S, D = h.shape
    G = len(B_GROUPS)
    n_heads = rel_bias.shape[1]
    width = n_heads * HEAD_DIM
    colscale = jnp.concatenate([jnp.full((G * width,), SCALE, F32), jnp.ones((2 * G * width,), F32)])
    w = (w_in * colscale).astype(BF16).reshape(D, 3, G, width)
    outs, lses = [], []
    for g, (window, dil) in enumerate(B_GROUPS):
        half = window // 2 // dil
        assert half <= tq
        L = S // dil
        nblk = L // tq
        assert nblk >= 2

        def by_residue(a, dil=dil, L=L):
            return a if dil == 1 else a.reshape(L, dil, -1).transpose(1, 0, 2).reshape(S, -1)

        def by_position(a, dil=dil, L=L):
            return a if dil == 1 else a.reshape(dil, L, -1).transpose(1, 0, 2).reshape(S, -1)

        qkv = matmul(by_residue(h), w[:, :, g].reshape(D, 3 * width), BF16)
        bias = t5_tiles(rel_bias, [-tq] * 3, tq, 3 * tq, dil=dil, half=half,
                        col_lo=(tq, 0, 0), col_hi=(3 * tq, 3 * tq, 2 * tq))

        def spec(part, di, nblk=nblk):
            return pl.BlockSpec(
                (tq, width), lambda r, i: (r * nblk + jnp.clip(i + di, 0, nblk - 1), part))

        o_g, lse_g = pl.pallas_call(
            functools.partial(_win_attn_kernel, n_heads=n_heads, tq=tq), grid=(dil, nblk),
            in_specs=[spec(0, 0), spec(1, -1), spec(1, 0), spec(1, 1),
                      spec(2, -1), spec(2, 0), spec(2, 1),
                      pl.BlockSpec((1, n_heads, tq, 3 * tq),
                                   lambda r, i, nblk=nblk: (_edge_variant(i, nblk), 0, 0, 0))],
            out_specs=[pl.BlockSpec((tq, width), lambda r, i, nblk=nblk: (r * nblk + i, 0)),
                       pl.BlockSpec((tq, n_heads), lambda r, i, nblk=nblk: (r * nblk + i, 0))],
            out_shape=[jax.ShapeDtypeStruct((S, width), F32),
                       jax.ShapeDtypeStruct((S, n_heads), F32)],
            compiler_params=_params("parallel", "parallel"),
            name=f"attn_b{g}")(qkv, qkv, qkv, qkv, qkv, qkv, qkv, bias)
        outs.append(by_position(o_g))
        lses.append(by_position(lse_g))
    row = pl.BlockSpec((tm, width), lambda i: (i, 0))
    o = pl.pallas_call(
        functools.partial(_merge_b_kernel, n_heads=n_heads), grid=(S // tm,),
        in_specs=[row, row, row, pl.BlockSpec((G, tm, n_heads), lambda i: (0, i, 0))],
        out_specs=row, out_shape=jax.ShapeDtypeStruct((S, width), BF16),
        compiler_params=_params("parallel"), name="merge_b")(*outs, jnp.stack(lses, 0))
    return matmul(o, w_out.astype(BF16), F32)


def _c_bias_kernel(rpb_ref, o_ref, *, rq, n_dr, n_dc):
    h = pl.program_id(0)
    W = GRID_W
    qc = lax.broadcasted_iota(jnp.int32, (W, 2 * W), 0)
    lane = lax.broadcasted_iota(jnp.int32, (W, 2 * W), 1)
    kc = lane & (W - 1)
    dc = kc - qc
    c0 = jnp.clip(qc - C_WIN_C // 2, 0, W - C_WIN_C)
    ok_c = (kc >= c0) & (kc < c0 + C_WIN_C)
    masked = jnp.full((W, 2 * W), MASKED, F32)
    base = h * (n_dr * n_dc)
    sub = []
    for a in range(n_dr):
        val = masked
        for b in range(n_dc):
            val = jnp.where(dc == b - (C_WIN_C - 1), rpb_ref[base + a * n_dc + b], val)
        sub.append(jnp.where(ok_c, val, MASKED))
    for v in range(3):
        for qr in range(rq):
            for jv in range(3 * rq // 2):
                halves = []
                for kr in (2 * jv, 2 * jv + 1):
                    lo, hi = ((rq, 3 * rq - 1), (qr, qr + C_WIN_R - 1), (0, C_WIN_R - 1))[v]
                    halves.append(sub[kr - qr + C_WIN_R - 1 - rq] if lo <= kr <= hi else masked)
                o_ref[v, 0, qr * W:(qr + 1) * W, jv * 2 * W:(jv + 1) * 2 * W] = jnp.where(
                    lane < W, halves[0], halves[1])


def mixer_c(h, w_in, w_out, rpb, heads_per_step=8):
    S, D = h.shape
    n_heads = D // HEAD_DIM
    rows = S // GRID_W
    rq = C_WIN_R // 2
    tq = rq * GRID_W
    nblk = rows // rq
    assert rows >= C_WIN_R and nblk >= 2 and 2 * GRID_W == HEAD_DIM
    n_dr, n_dc = 2 * C_WIN_R - 1, 2 * C_WIN_C - 1
    colscale = jnp.concatenate([jnp.full((D,), SCALE, F32), jnp.ones((2 * D,), F32)])
    qkv = matmul(h, (w_in * colscale).astype(BF16), BF16)
    bias = pl.pallas_call(
        functools.partial(_c_bias_kernel, rq=rq, n_dr=n_dr, n_dc=n_dc), grid=(n_heads,),
        in_specs=[pl.BlockSpec(memory_space=pltpu.SMEM)],
        out_specs=pl.BlockSpec((3, 1, tq, 3 * tq), lambda hh: (0, hh, 0, 0)),
        out_shape=jax.ShapeDtypeStruct((3, n_heads, tq, 3 * tq), F32),
        compiler_params=_params("parallel"), name="c_bias")(rpb.reshape(-1))
    hs = heads_per_step
    n_hg = n_heads // hs
    width = hs * HEAD_DIM

    def spec(part, di):
        return pl.BlockSpec(
            (tq, width), lambda hg, i: (jnp.clip(i + di, 0, nblk - 1), part * n_hg + hg))

    o = pl.pallas_call(
        functools.partial(_win_attn_kernel, n_heads=hs, tq=tq), grid=(n_hg, nblk),
        in_specs=[spec(0, 0), spec(1, -1), spec(1, 0), spec(1, 1),
                  spec(2, -1), spec(2, 0), spec(2, 1),
                  pl.BlockSpec((1, hs, tq, 3 * tq),
                               lambda hg, i: (_edge_variant(i, nblk), hg, 0, 0))],
        out_specs=pl.BlockSpec((tq, width), lambda hg, i: (i, hg)),
        out_shape=jax.ShapeDtypeStruct((S, D), BF16),
        compiler_params=_params("parallel", "parallel"),
        name="attn_c")(qkv, qkv, qkv, qkv, qkv, qkv, qkv, bias)
    return matmul(o, w_out.astype(BF16), F32)


def kernel(x, rel_bias, norm_g, a_w_in, a_w_out, a_lambda, a_subln, b_w_in, b_w_out, c_w_in, c_w_out,
           c_rpb, d_w_in, d_w_out, d_qk_norm, ffn_w_gate, ffn_w_up, ffn_w_down):
    B, S, D = x.shape
    assert B == 1
    depth = norm_g.shape[0]
    xs = x.reshape(S, D)
    hn = prenorm(xs, norm_g[0, 0])
    for i in range(depth):
        m, j = i % N_MIXERS, i // N_MIXERS
        if m == 0:
            y = mixer_a(hn, a_w_in[j], a_w_out[j], a_lambda[j], a_subln[j], rel_bias, i)
        elif m == 1:
            y = mixer_b(hn, b_w_in[j], b_w_out[j], rel_bias)
        elif m == 2:
            y = mixer_c(hn, c_w_in[j], c_w_out[j], c_rpb[j])
        else:
            y = mixer_d(hn, d_w_in[j], d_w_out[j], d_qk_norm[j])
        xs, hn = resid(xs, y, norm_g[i, 1], norm_g[i, 2])
        y = ffn(hn, ffn_w_gate[i], ffn_w_up[i], ffn_w_down[i])
        if i + 1 < depth:
            xs, hn = resid(xs, y, norm_g[i, 3], norm_g[i + 1, 0])
        else:
            xs = resid(xs, y, norm_g[i, 3])
    return xs.reshape(B, S, D)
```

```python
import functools
import math

import numpy as np
import jax
import jax.numpy as jnp
from jax import lax
from jax.experimental import pallas as pl
from jax.experimental.pallas import tpu as pltpu

F32 = jnp.float32
BF16 = jnp.bfloat16

HEAD_DIM = 128
SCALE = HEAD_DIM ** -0.5
LOG2E = math.log2(math.e)
GRID_W = 64
EPS = 1e-6
REL_BUCKETS = 32
REL_MAX_DIST = 1024
B_GROUPS = ((128, 1), (512, 4), (2048, 16))
C_WIN_R = 8
C_WIN_C = 16
D_KV_HEADS = 8
ROPE_THETA = 10000.0
N_MIXERS = 4

KEY_CHUNK = 256
SUBLANES = 8
ONES_ROWS = 2 * SUBLANES
MASKED = -1e30
VMEM_LIMIT_BYTES = 56 * 2 ** 20


def _params(*semantics):
    return pltpu.CompilerParams(dimension_semantics=semantics, vmem_limit_bytes=VMEM_LIMIT_BYTES)


def _rms(x, g):
    ms = jnp.mean(x * x, axis=-1, keepdims=True)
    return x * lax.rsqrt(ms + EPS) * g


def _dot_nt(a, b):
    return lax.dot_general(a, b, (((1,), (1,)), ((), ())), preferred_element_type=F32)


def _prenorm_kernel(x_ref, g_ref, h_ref):
    h_ref[...] = _rms(x_ref[...], g_ref[...]).astype(h_ref.dtype)


def prenorm(x, g, tm=256):
    S, D = x.shape
    row = pl.BlockSpec((tm, D), lambda i: (i, 0))
    vec = pl.BlockSpec((1, D), lambda i: (0, 0))
    return pl.pallas_call(
        _prenorm_kernel, grid=(S // tm,), in_specs=[row, vec], out_specs=row,
        out_shape=jax.ShapeDtypeStruct((S, D), BF16), compiler_params=_params("parallel"),
        name="prenorm")(x, g.reshape(1, D))


def _resid_kernel(x_ref, y_ref, g1_ref, g2_ref, xo_ref, h_ref):
    x = x_ref[...] + _rms(y_ref[...], g1_ref[...])
    xo_ref[...] = x
    h_ref[...] = _rms(x, g2_ref[...]).astype(h_ref.dtype)


def _resid_last_kernel(x_ref, y_ref, g1_ref, xo_ref):
    xo_ref[...] = x_ref[...] + _rms(y_ref[...], g1_ref[...])


def resid(x, y, g_post, g_next=None, tm=256):
    S, D = x.shape
    row = pl.BlockSpec((tm, D), lambda i: (i, 0))
    vec = pl.BlockSpec((1, D), lambda i: (0, 0))
    if g_next is None:
        return pl.pallas_call(
            _resid_last_kernel, grid=(S // tm,), in_specs=[row, row, vec], out_specs=row,
            out_shape=jax.ShapeDtypeStruct((S, D), F32), compiler_params=_params("parallel"),
            name="resid_last")(x, y, g_post.reshape(1, D))
    return pl.pallas_call(
        _resid_kernel, grid=(S // tm,), in_specs=[row, row, vec, vec], out_specs=[row, row],
        out_shape=[jax.ShapeDtypeStruct((S, D), F32), jax.ShapeDtypeStruct((S, D), BF16)],
        compiler_params=_params("parallel"),
        name="resid")(x, y, g_post.reshape(1, D), g_next.reshape(1, D))


def _mm_kernel(a_ref, b_ref, o_ref):
    o_ref[...] = jnp.dot(a_ref[...], b_ref[...], preferred_element_type=F32).astype(o_ref.dtype)


def _mm_acc_kernel(a_ref, b_ref, o_ref, acc_ref):
    k = pl.program_id(2)

    @pl.when(k == 0)
    def _():
        acc_ref[...] = jnp.zeros_like(acc_ref)

    acc_ref[...] += jnp.dot(a_ref[...], b_ref[...], preferred_element_type=F32)

    @pl.when(k == pl.num_programs(2) - 1)
    def _():
        o_ref[...] = acc_ref[...].astype(o_ref.dtype)


def matmul(a, b, out_dtype, tm=1024, tn=None, tk=None):
    M, K = a.shape
    N = b.shape[1]
    tk = K if tk is None else tk
    if tn is None:
        tn = 1024 if N % 1024 == 0 else 512
    out_shape = jax.ShapeDtypeStruct((M, N), out_dtype)
    if tk == K:
        return pl.pallas_call(
            _mm_kernel, grid=(M // tm, N // tn),
            in_specs=[pl.BlockSpec((tm, K), lambda i, j: (i, 0)),
                      pl.BlockSpec((K, tn), lambda i, j: (0, j))],
            out_specs=pl.BlockSpec((tm, tn), lambda i, j: (i, j)),
            out_shape=out_shape, compiler_params=_params("parallel", "parallel"),
            name="matmul")(a, b)
    return pl.pallas_call(
        _mm_acc_kernel, grid=(M // tm, N // tn, K // tk),
        in_specs=[pl.BlockSpec((tm, tk), lambda i, j, k: (i, k)),
                  pl.BlockSpec((tk, tn), lambda i, j, k: (k, j))],
        out_specs=pl.BlockSpec((tm, tn), lambda i, j, k: (i, j)),
        out_shape=out_shape, scratch_shapes=[pltpu.VMEM((tm, tn), F32)],
        compiler_params=_params("parallel", "parallel", "arbitrary"),
        name="matmul_acc")(a, b)


def _glu_kernel(a_ref, wg_ref, wu_ref, o_ref):
    a = a_ref[...]
    g = jnp.dot(a, wg_ref[...], preferred_element_type=F32)
    u = jnp.dot(a, wu_ref[...], preferred_element_type=F32)
    o_ref[...] = (g * (1.0 / (1.0 + jnp.exp(-g))) * u).astype(o_ref.dtype)


def glu_up(a, wg, wu, tm=2048, tn=256):
    M, K = a.shape
    N = wg.shape[1]
    wspec = pl.BlockSpec((K, tn), lambda i, j: (0, j))
    return pl.pallas_call(
        _glu_kernel, grid=(M // tm, N // tn),
        in_specs=[pl.BlockSpec((tm, K), lambda i, j: (i, 0)), wspec, wspec],
        out_specs=pl.BlockSpec((tm, tn), lambda i, j: (i, j)),
        out_shape=jax.ShapeDtypeStruct((M, N), BF16),
        compiler_params=_params("parallel", "parallel"), name="glu_up")(a, wg, wu)


def ffn(h, w_gate, w_up, w_down):
    hid = glu_up(h, w_gate.astype(BF16), w_up.astype(BF16))
    d_ff = hid.shape[1]
    tk = d_ff // 2 if (d_ff // 2) % HEAD_DIM == 0 else d_ff
    return matmul(hid, w_down.astype(BF16), F32, tm=1024, tn=512, tk=tk)


def _t5_thresholds():
    nb = REL_BUCKETS // 2
    max_exact = nb // 2
    n = np.arange(0, 4 * REL_MAX_DIST)
    nf = np.maximum(n, 1).astype(np.float32)
    large = max_exact + (np.log(nf / np.float32(max_exact))
                         / np.float32(math.log(REL_MAX_DIST / max_exact))
                         * np.float32(nb - max_exact)).astype(np.int32)
    bucket = np.where(n < max_exact, n, np.minimum(large, nb - 1))
    assert np.all(np.diff(bucket) >= 0) and bucket[-1] == nb - 1
    return tuple(int(np.argmax(bucket >= b)) for b in range(1, nb))


T5_THRESHOLDS = _t5_thresholds()


def _t5_tile_kernel(rb_ref, var_ref, o_ref, *, dil, half, n_heads, mult, keys_on_rows):
    h = pl.program_id(0)
    v = pl.program_id(1)
    off, col_lo, col_hi = var_ref[3 * v], var_ref[3 * v + 1], var_ref[3 * v + 2]
    rows, cols = o_ref.shape[2], o_ref.shape[3]
    i = lax.broadcasted_iota(jnp.int32, (rows, cols), 0)
    j = lax.broadcasted_iota(jnp.int32, (rows, cols), 1)
    nb = REL_BUCKETS // 2
    delta = (i - j if keys_on_rows else j - i) + off
    rel = delta * dil
    n = jnp.abs(rel)
    vneg = jnp.full((rows, cols), rb_ref[h], F32)
    vpos = jnp.full((rows, cols), rb_ref[nb * n_heads + h], F32)
    for b, t in enumerate(T5_THRESHOLDS, start=1):
        c = n >= t
        vneg = jnp.where(c, rb_ref[b * n_heads + h], vneg)
        vpos = jnp.where(c, rb_ref[(nb + b) * n_heads + h], vpos)
    val = jnp.where(rel > 0, vpos, vneg)
    if mult != 1.0:
        val = val * mult
    if half is not None:
        ok = (jnp.abs(delta) <= half) & (j >= col_lo) & (j < col_hi)
        val = jnp.where(ok, val, MASKED)
    o_ref[0, 0] = val


def t5_tiles(rel_bias, offs, rows, cols, dil=1, half=None, col_lo=None, col_hi=None, mult=1.0,
             keys_on_rows=False):
    n_heads = rel_bias.shape[1]
    nv = len(offs)
    col_lo = (0,) * nv if col_lo is None else col_lo
    col_hi = (cols,) * nv if col_hi is None else col_hi
    variants = jnp.asarray(np.stack([offs, col_lo, col_hi], axis=1).reshape(-1), jnp.int32)
    kern = functools.partial(_t5_tile_kernel, dil=dil, half=half, n_heads=n_heads, mult=mult,
                             keys_on_rows=keys_on_rows)
    smem = pl.BlockSpec(memory_space=pltpu.SMEM)
    return pl.pallas_call(
        kern, grid=(n_heads, nv), in_specs=[smem, smem],
        out_specs=pl.BlockSpec((1, 1, rows, cols), lambda h, v: (v, h, 0, 0)),
        out_shape=jax.ShapeDtypeStruct((nv, n_heads, rows, cols), F32),
        compiler_params=_params("parallel", "parallel"), name="t5_tiles")(rel_bias.reshape(-1), variants)


def _sublane_groups(x):
    return x.reshape(x.shape[0] // SUBLANES, SUBLANES, x.shape[1])


def _store_scores(s_sc, mp_sc, idx, s):
    s_sc[idx] = s
    mp_sc[idx] = jnp.max(_sublane_groups(s), axis=0)


def _online_step(m_sc, acc_sc, idx, s_sc, mp_sc, vt_chunk):
    Tk, Tq = s_sc.shape[1:]
    m_old = m_sc[idx]
    m_cur = jnp.max(mp_sc[idx], axis=0, keepdims=True)
    m_new = jnp.maximum(m_old, jnp.broadcast_to(m_cur, m_old.shape))
    m_sc[idx] = m_new
    alpha = jnp.exp2(m_old - m_new)
    pv = None
    for c in range(Tk // KEY_CHUNK):
        ks = slice(c * KEY_CHUNK, (c + 1) * KEY_CHUNK)
        p = jnp.exp2(_sublane_groups(s_sc[idx, ks, :]) - m_new[None])
        d = jnp.dot(vt_chunk(ks), p.reshape(KEY_CHUNK, Tq).astype(BF16), preferred_element_type=F32)
        pv = d if pv is None else pv + d
    acc_sc[idx] = (_sublane_groups(acc_sc[idx]) * alpha[None]).reshape(acc_sc.shape[1:]) + pv


def _flash_init(m_sc, acc_sc):
    m_sc[...] = jnp.full(m_sc.shape, -jnp.inf, F32)
    acc_sc[...] = jnp.zeros(acc_sc.shape, F32)


def _values_transposed(v, n_heads, Tk):
    S = v.shape[0]
    dv = v.shape[1] // n_heads
    v_t = v.reshape(S // Tk, Tk, n_heads, dv).transpose(2, 0, 3, 1)
    return jnp.concatenate([v_t, jnp.ones((n_heads, S // Tk, ONES_ROWS, Tk), v.dtype)], axis=2)


def _pipelined_blocks(nk, scores, consume, buf0, buf1):
    assert nk % 2 == 0
    scores(0, *buf0)

    def body(j, carry):
        kb = 2 * j
        scores(kb + 1, *buf1)
        consume(kb, *buf0)
        scores(kb + 2, *buf0)
        consume(kb + 1, *buf1)
        return carry

    lax.fori_loop(0, nk // 2 - 1, body, 0)
    scores(nk - 1, *buf1)
    consume(nk - 2, *buf0)
    consume(nk - 1, *buf1)


def _attn_a_kernel(q_ref, k_ref, vt_ref, band_ref, lam_ref, g_ref, o_ref, m_sc, acc_sc,
                   s0_sc, s1_sc, mp0_sc, mp1_sc, *, Tk, nk, q_step, k_step, e_lo, e_hi, lam_init):
    qb = pl.program_id(1)
    dv = 2 * HEAD_DIM
    _flash_init(m_sc, acc_sc)

    def scores(kb, s_sc, mp_sc):
        off = pl.multiple_of(kb * Tk, Tk)
        bidx = jnp.clip(kb * k_step - qb * q_step, e_lo, e_hi) - e_lo
        for m in range(2):
            sl = slice(m * HEAD_DIM, (m + 1) * HEAD_DIM)
            _store_scores(s_sc, mp_sc, m,
                          _dot_nt(k_ref[pl.ds(off, Tk), sl], q_ref[:, sl]) + band_ref[bidx, 0])

    def consume(kb, s_sc, mp_sc):
        for m in range(2):
            _online_step(m_sc, acc_sc, m, s_sc, mp_sc, lambda ks: vt_ref[0, kb, :, ks])

    _pipelined_blocks(nk, scores, consume, (s0_sc, mp0_sc), (s1_sc, mp1_sc))
    lp = lam_ref[...]
    lam = (jnp.exp(jnp.sum(lp[0:1] * lp[1:2], axis=-1, keepdims=True))
           - jnp.exp(jnp.sum(lp[2:3] * lp[3:4], axis=-1, keepdims=True)) + lam_init)
    acc0, acc1 = acc_sc[0], acc_sc[1]
    o_t = acc0[:dv] / acc0[dv:dv + 1] - lam * (acc1[:dv] / acc1[dv:dv + 1])
    o_ref[...] = (_rms(o_t.T, g_ref[...]) * (1.0 - lam_init)).astype(o_ref.dtype)


def mixer_a(h, w_in, w_out, lam_p, subln_g, rel_bias, layer_idx, T=512, Tk=1024):
    S, D = h.shape
    n_heads = D // (2 * HEAD_DIM)
    width = 2 * HEAD_DIM
    colscale = jnp.concatenate([jnp.full((D,), SCALE * LOG2E, F32), jnp.ones((2 * D,), F32)])
    qkv = matmul(h, (w_in * colscale).astype(BF16), BF16)
    far = T5_THRESHOLDS[-1]
    unit = math.gcd(T, Tk)
    e_hi = -(-(far + T - 1) // unit)
    e_lo = -(-(far + Tk - 1) // unit)
    e_lo = -e_lo
    tiles = t5_tiles(rel_bias, [e * unit for e in range(e_lo, e_hi + 1)], Tk, T, mult=LOG2E,
                     keys_on_rows=True)
    n_tiles = e_hi - e_lo + 1
    nk = S // Tk
    v_t = _values_transposed(qkv[:, 2 * D:], n_heads, Tk)
    lam_init = 0.8 - 0.6 * math.exp(-0.3 * layer_idx)
    kern = functools.partial(_attn_a_kernel, Tk=Tk, nk=nk, q_step=T // unit, k_step=Tk // unit,
                             e_lo=e_lo, e_hi=e_hi, lam_init=lam_init)
    once = pl.Buffered(1)
    o = pl.pallas_call(
        kern, grid=(n_heads, S // T),
        in_specs=[pl.BlockSpec((T, width), lambda hh, qb: (qb, hh)),
                  pl.BlockSpec((S, width), lambda hh, qb: (0, n_heads + hh), pipeline_mode=once),
                  pl.BlockSpec((1, nk, width + ONES_ROWS, Tk), lambda hh, qb: (hh, 0, 0, 0),
                               pipeline_mode=once),
                  pl.BlockSpec((n_tiles, 1, Tk, T), lambda hh, qb: (0, hh, 0, 0), pipeline_mode=once),
                  pl.BlockSpec((4, HEAD_DIM), lambda hh, qb: (0, 0)),
                  pl.BlockSpec((1, width), lambda hh, qb: (0, 0))],
        out_specs=pl.BlockSpec((T, width), lambda hh, qb: (qb, hh)),
        out_shape=jax.ShapeDtypeStruct((S, D), BF16),
        scratch_shapes=[pltpu.VMEM((2, SUBLANES, T), F32), pltpu.VMEM((2, width + ONES_ROWS, T), F32),
                        pltpu.VMEM((2, Tk, T), F32), pltpu.VMEM((2, Tk, T), F32),
                        pltpu.VMEM((2, SUBLANES, T), F32), pltpu.VMEM((2, SUBLANES, T), F32)],
        compiler_params=_params("parallel", "arbitrary"),
        name="attn_a")(qkv, qkv, v_t, tiles, lam_p, subln_g.reshape(1, width))
    return matmul(o, w_out.astype(BF16), F32)


def _rope_kernel(x_ref, c_ref, sa_ref, sb_ref, g_ref, o_ref, *, n_q, n_chunks):
    c, sa, sb = c_ref[...], sa_ref[...], sb_ref[...]
    for ch in range(n_chunks):
        sl = slice(ch * HEAD_DIM, (ch + 1) * HEAD_DIM)
        y = _rms(x_ref[:, sl], g_ref[0:1] if ch < n_q else g_ref[1:2])
        y = y * c + pltpu.roll(y, HEAD_DIM - 1, 1) * sa + pltpu.roll(y, 1, 1) * sb
        if ch < n_q:
            y = y * (SCALE * LOG2E)
        o_ref[:, sl] = y.astype(o_ref.dtype)


def _rope_tables(S):
    pos = jnp.arange(S)
    n_freq = HEAD_DIM // 4
    freqs = ROPE_THETA ** (-jnp.arange(n_freq, dtype=F32) / n_freq)
    ang = jnp.concatenate([(pos // GRID_W).astype(F32)[:, None] * freqs,
                           (pos % GRID_W).astype(F32)[:, None] * freqs], axis=-1)
    cos, sin = jnp.cos(ang), jnp.sin(ang)
    zero = jnp.zeros_like(sin)
    c = jnp.stack([cos, cos], axis=-1).reshape(S, HEAD_DIM)
    sa = jnp.stack([-sin, zero], axis=-1).reshape(S, HEAD_DIM)
    sb = jnp.stack([zero, sin], axis=-1).reshape(S, HEAD_DIM)
    return c, sa, sb


def _attn_d_kernel(q_ref, k_ref, vt_ref, o_ref, m_sc, acc_sc, s0_sc, s1_sc, mp0_sc, mp1_sc,
                   *, Tk, nk, rep):
    _flash_init(m_sc, acc_sc)

    def scores(kb, s_sc, mp_sc):
        kblk = k_ref[pl.ds(pl.multiple_of(kb * Tk, Tk), Tk), :]
        for r in range(rep):
            _store_scores(s_sc, mp_sc, r, _dot_nt(kblk, q_ref[:, r * HEAD_DIM:(r + 1) * HEAD_DIM]))

    def consume(kb, s_sc, mp_sc):
        for r in range(rep):
            _online_step(m_sc, acc_sc, r, s_sc, mp_sc, lambda ks: vt_ref[0, kb, :, ks])

    _pipelined_blocks(nk, scores, consume, (s0_sc, mp0_sc), (s1_sc, mp1_sc))
    for r in range(rep):
        acc = acc_sc[r]
        o_t = acc[:HEAD_DIM] / acc[HEAD_DIM:HEAD_DIM + 1]
        o_ref[:, r * HEAD_DIM:(r + 1) * HEAD_DIM] = o_t.T.astype(o_ref.dtype)


def mixer_d(h, w_in, w_out, qk_g, T=512, Tk=1024, tm=256):
    S, D = h.shape
    n_q = D // HEAD_DIM
    n_kv = D_KV_HEADS
    rep = n_q // n_kv
    nqk = (n_q + n_kv) * HEAD_DIM
    qk = matmul(h, w_in[:, :nqk].astype(BF16), F32)
    v = matmul(h, w_in[:, nqk:].astype(BF16), BF16)
    c, sa, sb = _rope_tables(S)
    tab = pl.BlockSpec((tm, HEAD_DIM), lambda i: (i, 0))
    qk = pl.pallas_call(
        functools.partial(_rope_kernel, n_q=n_q, n_chunks=n_q + n_kv), grid=(S // tm,),
        in_specs=[pl.BlockSpec((tm, nqk), lambda i: (i, 0)), tab, tab, tab,
                  pl.BlockSpec((2, HEAD_DIM), lambda i: (0, 0))],
        out_specs=pl.BlockSpec((tm, nqk), lambda i: (i, 0)),
        out_shape=jax.ShapeDtypeStruct((S, nqk), BF16),
        compiler_params=_params("parallel"), name="qk_norm_rope")(qk, c, sa, sb, qk_g)
    nk = S // Tk
    v_t = _values_transposed(v, n_kv, Tk)
    vt_rows = HEAD_DIM + ONES_ROWS
    kern = functools.partial(_attn_d_kernel, Tk=Tk, nk=nk, rep=rep)
    o = pl.pallas_call(
        kern, grid=(n_kv, S // T),
        in_specs=[pl.BlockSpec((T, rep * HEAD_DIM), lambda g, qb: (qb, g)),
                  pl.BlockSpec((S, HEAD_DIM), lambda g, qb: (0, n_q + g)),
                  pl.BlockSpec((1, nk, vt_rows, Tk), lambda g, qb: (g, 0, 0, 0))],
        out_specs=pl.BlockSpec((T, rep * HEAD_DIM), lambda g, qb: (qb, g)),
        out_shape=jax.ShapeDtypeStruct((S, D), BF16),
        scratch_shapes=[pltpu.VMEM((rep, SUBLANES, T), F32),
                        pltpu.VMEM((rep, vt_rows, T), F32),
                        pltpu.VMEM((rep, Tk, T), F32), pltpu.VMEM((rep, Tk, T), F32),
                        pltpu.VMEM((rep, SUBLANES, T), F32), pltpu.VMEM((rep, SUBLANES, T), F32)],
        compiler_params=_params("parallel", "arbitrary"), name="attn_d")(qk, qk, v_t)
    return matmul(o, w_out.astype(BF16), F32)


def _win_attn_kernel(q_ref, kp_ref, kc_ref, kn_ref, vp_ref, vc_ref, vn_ref, bias_ref, o_ref,
                     *lse_ref, n_heads, tq):
    lse_cols = []
    for h in range(n_heads):
        sl = slice(h * HEAD_DIM, (h + 1) * HEAD_DIM)
        q = q_ref[:, sl]
        ks = (kp_ref[:, sl], kc_ref[:, sl], kn_ref[:, sl])
        vs = (vp_ref[:, sl], vc_ref[:, sl], vn_ref[:, sl])
        ss = [_dot_nt(q, ks[j]) + bias_ref[0, h, :, j * tq:(j + 1) * tq] for j in range(3)]
        m = jnp.maximum(jnp.maximum(jnp.max(ss[0], axis=-1, keepdims=True),
                                    jnp.max(ss[1], axis=-1, keepdims=True)),
                        jnp.max(ss[2], axis=-1, keepdims=True))
        ps = [jnp.exp(s - m) for s in ss]
        l = (jnp.sum(ps[0], axis=-1, keepdims=True) + jnp.sum(ps[1], axis=-1, keepdims=True)
             + jnp.sum(ps[2], axis=-1, keepdims=True))
        acc = (jnp.dot(ps[0].astype(BF16), vs[0], preferred_element_type=F32)
               + jnp.dot(ps[1].astype(BF16), vs[1], preferred_element_type=F32)
               + jnp.dot(ps[2].astype(BF16), vs[2], preferred_element_type=F32))
        o_ref[:, sl] = (acc / l).astype(o_ref.dtype)
        lse_cols.append(m + jnp.log(l))
    if lse_ref:
        lse_ref[0][...] = jnp.concatenate(lse_cols, axis=1)


def _edge_variant(i, nblk):
    return jnp.where(i == 0, 0, jnp.where(i == nblk - 1, 2, 1))


def _merge_b_kernel(o0_ref, o1_ref, o2_ref, lse_ref, out_ref, *, n_heads):
    lse = lse_ref[...]
    w = jnp.exp(lse - jnp.max(lse, axis=0, keepdims=True))
    w = w / jnp.sum(w, axis=0, keepdims=True)
    for h in range(n_heads):
        sl = slice(h * HEAD_DIM, (h + 1) * HEAD_DIM)
        out_ref[:, sl] = (w[0, :, h:h + 1] * o0_ref[:, sl] + w[1, :, h:h + 1] * o1_ref[:, sl]
                          + w[2, :, h:h + 1] * o2_ref[:, sl]).astype(out_ref.dtype)


def mixer_b(h, w_in, w_out, rel_bias, tq=128, tm=256):
    S, D = h.shape
    G = len(B_GROUPS)
    n_heads = rel_bias.shape[1]
    width = n_heads * HEAD_DIM
    colscale = jnp.concatenate([jnp.full((G * width,), SCALE, F32), jnp.ones((2 * G * width,), F32)])
    w = (w_in * colscale).astype(BF16).reshape(D, 3, G, width)
    outs, lses = [], []
    for g, (window, dil) in enumerate(B_GROUPS):
        half = window // 2 // dil
        assert half <= tq
        L = S // dil
        nblk = L // tq
        assert nblk >= 2

        def by_residue(a, dil=dil, L=L):
            return a if dil == 1 else a.reshape(L, dil, -1).transpose(1, 0, 2).reshape(S, -1)

        def by_position(a, dil=dil, L=L):
            return a if dil == 1 else a.reshape(dil, L, -1).transpose(1, 0, 2).reshape(S, -1)

        qkv = matmul(by_residue(h), w[:, :, g].reshape(D, 3 * width), BF16)
        bias = t5_tiles(rel_bias, [-tq] * 3, tq, 3 * tq, dil=dil, half=half,
                        col_lo=(tq, 0, 0), col_hi=(3 * tq, 3 * tq, 2 * tq))

        def spec(part, di, nblk=nblk):
            return pl.BlockSpec(
                (tq, width), lambda r, i: (r * nblk + jnp.clip(i + di, 0, nblk - 1), part))

        o_g, lse_g = pl.pallas_call(
            functools.partial(_win_attn_kernel, n_heads=n_heads, tq=tq), grid=(dil, nblk),
            in_specs=[spec(0, 0), spec(1, -1), spec(1, 0), spec(1, 1),
                      spec(2, -1), spec(2, 0), spec(2, 1),
                      pl.BlockSpec((1, n_heads, tq, 3 * tq),
                                   lambda r, i, nblk=nblk: (_edge_variant(i, nblk), 0, 0, 0))],
            out_specs=[pl.BlockSpec((tq, width), lambda r, i, nblk=nblk: (r * nblk + i, 0)),
                       pl.BlockSpec((tq, n_heads), lambda r, i, nblk=nblk: (r * nblk + i, 0))],
            out_shape=[jax.ShapeDtypeStruct((S, width), F32),
                       jax.ShapeDtypeStruct((S, n_heads), F32)],
            compiler_params=_params("parallel", "parallel"),
            name=f"attn_b{g}")(qkv, qkv, qkv, qkv, qkv, qkv, qkv, bias)
        outs.append(by_position(o_g))
        lses.append(by_position(lse_g))
    row = pl.BlockSpec((tm, width), lambda i: (i, 0))
    o = pl.pallas_call(
        functools.partial(_merge_b_kernel, n_heads=n_heads), grid=(S // tm,),
        in_specs=[row, row, row, pl.BlockSpec((G, tm, n_heads), lambda i: (0, i, 0))],
        out_specs=row, out_shape=jax.ShapeDtypeStruct((S, width), BF16),
        compiler_params=_params("parallel"), name="merge_b")(*outs, jnp.stack(lses, 0))
    return matmul(o, w_out.astype(BF16), F32)


def _c_bias_kernel(rpb_ref, o_ref, *, rq, n_dr, n_dc):
    h = pl.program_id(0)
    W = GRID_W
    qc = lax.broadcasted_iota(jnp.int32, (W, 2 * W), 0)
    lane = lax.broadcasted_iota(jnp.int32, (W, 2 * W), 1)
    kc = lane & (W - 1)
    dc = kc - qc
    c0 = jnp.clip(qc - C_WIN_C // 2, 0, W - C_WIN_C)
    ok_c = (kc >= c0) & (kc < c0 + C_WIN_C)
    masked = jnp.full((W, 2 * W), MASKED, F32)
    base = h * (n_dr * n_dc)
    sub = []
    for a in range(n_dr):
        val = masked
        for b in range(n_dc):
            val = jnp.where(dc == b - (C_WIN_C - 1), rpb_ref[base + a * n_dc + b], val)
        sub.append(jnp.where(ok_c, val, MASKED))
    for v in range(3):
        for qr in range(rq):
            for jv in range(3 * rq // 2):
                halves = []
                for kr in (2 * jv, 2 * jv + 1):
                    lo, hi = ((rq, 3 * rq - 1), (qr, qr + C_WIN_R - 1), (0, C_WIN_R - 1))[v]
                    halves.append(sub[kr - qr + C_WIN_R - 1 - rq] if lo <= kr <= hi else masked)
                o_ref[v, 0, qr * W:(qr + 1) * W, jv * 2 * W:(jv + 1) * 2 * W] = jnp.where(
                    lane < W, halves[0], halves[1])


def mixer_c(h, w_in, w_out, rpb, heads_per_step=8):
    S, D = h.shape
    n_heads = D // HEAD_DIM
    rows = S // GRID_W
    rq = C_WIN_R // 2
    tq = rq * GRID_W
    nblk = rows // rq
    assert rows >= C_WIN_R and nblk >= 2 and 2 * GRID_W == HEAD_DIM
    n_dr, n_dc = 2 * C_WIN_R - 1, 2 * C_WIN_C - 1
    colscale = jnp.concatenate([jnp.full((D,), SCALE, F32), jnp.ones((2 * D,), F32)])
    qkv = matmul(h, (w_in * colscale).astype(BF16), BF16)
    bias = pl.pallas_call(
        functools.partial(_c_bias_kernel, rq=rq, n_dr=n_dr, n_dc=n_dc), grid=(n_heads,),
        in_specs=[pl.BlockSpec(memory_space=pltpu.SMEM)],
        out_specs=pl.BlockSpec((3, 1, tq, 3 * tq), lambda hh: (0, hh, 0, 0)),
        out_shape=jax.ShapeDtypeStruct((3, n_heads, tq, 3 * tq), F32),
        compiler_params=_params("parallel"), name="c_bias")(rpb.reshape(-1))
    hs = heads_per_step
    n_hg = n_heads // hs
    width = hs * HEAD_DIM

    def spec(part, di):
        return pl.BlockSpec(
            (tq, width), lambda hg, i: (jnp.clip(i + di, 0, nblk - 1), part * n_hg + hg))

    o = pl.pallas_call(
        functools.partial(_win_attn_kernel, n_heads=hs, tq=tq), grid=(n_hg, nblk),
        in_specs=[spec(0, 0), spec(1, -1), spec(1, 0), spec(1, 1),
                  spec(2, -1), spec(2, 0), spec(2, 1),
                  pl.BlockSpec((1, hs, tq, 3 * tq),
                               lambda hg, i: (_edge_variant(i, nblk), hg, 0, 0))],
        out_specs=pl.BlockSpec((tq, width), lambda hg, i: (i, hg)),
        out_shape=jax.ShapeDtypeStruct((S, D), BF16),
        compiler_params=_params("parallel", "parallel"),
        name="attn_c")(qkv, qkv, qkv, qkv, qkv, qkv, qkv, bias)
    return matmul(o, w_out.astype(BF16), F32)


def kernel(x, rel_bias, norm_g, a_w_in, a_w_out, a_lambda, a_subln, b_w_in, b_w_out, c_w_in, c_w_out,
           c_rpb, d_w_in, d_w_out, d_qk_norm, ffn_w_gate, ffn_w_up, ffn_w_down):
    B, S, D = x.shape
    assert B == 1
    depth = norm_g.shape[0]
    xs = x.reshape(S, D)
    hn = prenorm(xs, norm_g[0, 0])
    for i in range(depth):
        m, j = i % N_MIXERS, i // N_MIXERS
        if m == 0:
            y = mixer_a(hn, a_w_in[j], a_w_out[j], a_lambda[j], a_subln[j], rel_bias, i)
        elif m == 1:
            y = mixer_b(hn, b_w_in[j], b_w_out[j], rel_bias)
        elif m == 2:
            y = mixer_c(hn, c_w_in[j], c_w_out[j], c_rpb[j])
        else:
            y = mixer_d(hn, d_w_in[j], d_w_out[j], d_qk_norm[j])
        xs, hn = resid(xs, y, norm_g[i, 1], norm_g[i, 2])
        y = ffn(hn, ffn_w_gate[i], ffn_w_up[i], ffn_w_down[i])
        if i + 1 < depth:
            xs, hn = resid(xs, y, norm_g[i, 3], norm_g[i + 1, 0])
        else:
            xs = resid(xs, y, norm_g[i, 3])
    return xs.reshape(B, S, D)
```

```python
import functools
import math

import numpy as np
import jax
import jax.numpy as jnp
from jax import lax
from jax.experimental import pallas as pl
from jax.experimental.pallas import tpu as pltpu

F32 = jnp.float32
BF16 = jnp.bfloat16

HEAD_DIM = 128
SCALE = HEAD_DIM ** -0.5
LOG2E = math.log2(math.e)
GRID_W = 64
EPS = 1e-6
REL_BUCKETS = 32
REL_MAX_DIST = 1024
B_GROUPS = ((128, 1), (512, 4), (2048, 16))
C_WIN_R = 8
C_WIN_C = 16
D_KV_HEADS = 8
ROPE_THETA = 10000.0
N_MIXERS = 4

KEY_CHUNK = 256
SUBLANES = 8
ONES_ROWS = 2 * SUBLANES
MASKED = -1e30
VMEM_LIMIT_BYTES = 56 * 2 ** 20


def _params(*semantics):
    return pltpu.CompilerParams(dimension_semantics=semantics, vmem_limit_bytes=VMEM_LIMIT_BYTES)


def _rms(x, g):
    ms = jnp.mean(x * x, axis=-1, keepdims=True)
    return x * lax.rsqrt(ms + EPS) * g


def _dot_nt(a, b):
    return lax.dot_general(a, b, (((1,), (1,)), ((), ())), preferred_element_type=F32)


def _prenorm_kernel(x_ref, g_ref, h_ref):
    h_ref[...] = _rms(x_ref[...], g_ref[...]).astype(h_ref.dtype)


def prenorm(x, g, tm=256):
    S, D = x.shape
    row = pl.BlockSpec((tm, D), lambda i: (i, 0))
    vec = pl.BlockSpec((1, D), lambda i: (0, 0))
    return pl.pallas_call(
        _prenorm_kernel, grid=(S // tm,), in_specs=[row, vec], out_specs=row,
        out_shape=jax.ShapeDtypeStruct((S, D), BF16), compiler_params=_params("parallel"),
        name="prenorm")(x, g.reshape(1, D))


def _resid_kernel(x_ref, y_ref, g1_ref, g2_ref, xo_ref, h_ref):
    x = x_ref[...] + _rms(y_ref[...], g1_ref[...])
    xo_ref[...] = x
    h_ref[...] = _rms(x, g2_ref[...]).astype(h_ref.dtype)


def _resid_last_kernel(x_ref, y_ref, g1_ref, xo_ref):
    xo_ref[...] = x_ref[...] + _rms(y_ref[...], g1_ref[...])


def resid(x, y, g_post, g_next=None, tm=256):
    S, D = x.shape
    row = pl.BlockSpec((tm, D), lambda i: (i, 0))
    vec = pl.BlockSpec((1, D), lambda i: (0, 0))
    if g_next is None:
        return pl.pallas_call(
            _resid_last_kernel, grid=(S // tm,), in_specs=[row, row, vec], out_specs=row,
            out_shape=jax.ShapeDtypeStruct((S, D), F32), compiler_params=_params("parallel"),
            name="resid_last")(x, y, g_post.reshape(1, D))
    return pl.pallas_call(
        _resid_kernel, grid=(S // tm,), in_specs=[row, row, vec, vec], out_specs=[row, row],
        out_shape=[jax.ShapeDtypeStruct((S, D), F32), jax.ShapeDtypeStruct((S, D), BF16)],
        compiler_params=_params("parallel"),
        name="resid")(x, y, g_post.reshape(1, D), g_next.reshape(1, D))


def _mm_kernel(a_ref, b_ref, o_ref):
    o_ref[...] = jnp.dot(a_ref[...], b_ref[...], preferred_element_type=F32).astype(o_ref.dtype)


def _mm_acc_kernel(a_ref, b_ref, o_ref, acc_ref):
    k = pl.program_id(2)

    @pl.when(k == 0)
    def _():
        acc_ref[...] = jnp.zeros_like(acc_ref)

    acc_ref[...] += jnp.dot(a_ref[...], b_ref[...], preferred_element_type=F32)

    @pl.when(k == pl.num_programs(2) - 1)
    def _():
        o_ref[...] = acc_ref[...].astype(o_ref.dtype)


def matmul(a, b, out_dtype, tm=1024, tn=None, tk=None):
    M, K = a.shape
    N = b.shape[1]
    tk = K if tk is None else tk
    if tn is None:
        tn = 1024 if N % 1024 == 0 else 512
    out_shape = jax.ShapeDtypeStruct((M, N), out_dtype)
    if tk == K:
        return pl.pallas_call(
            _mm_kernel, grid=(M // tm, N // tn),
            in_specs=[pl.BlockSpec((tm, K), lambda i, j: (i, 0)),
                      pl.BlockSpec((K, tn), lambda i, j: (0, j))],
            out_specs=pl.BlockSpec((tm, tn), lambda i, j: (i, j)),
            out_shape=out_shape, compiler_params=_params("parallel", "parallel"),
            name="matmul")(a, b)
    return pl.pallas_call(
        _mm_acc_kernel, grid=(M // tm, N // tn, K // tk),
        in_specs=[pl.BlockSpec((tm, tk), lambda i, j, k: (i, k)),
                  pl.BlockSpec((tk, tn), lambda i, j, k: (k, j))],
        out_specs=pl.BlockSpec((tm, tn), lambda i, j, k: (i, j)),
        out_shape=out_shape, scratch_shapes=[pltpu.VMEM((tm, tn), F32)],
        compiler_params=_params("parallel", "parallel", "arbitrary"),
        name="matmul_acc")(a, b)


def _glu_kernel(a_ref, wg_ref, wu_ref, o_ref):
    a = a_ref[...]
    g = jnp.dot(a, wg_ref[...], preferred_element_type=F32)
    u = jnp.dot(a, wu_ref[...], preferred_element_type=F32)
    o_ref[...] = (g * (1.0 / (1.0 + jnp.exp(-g))) * u).astype(o_ref.dtype)


def glu_up(a, wg, wu, tm=2048, tn=256):
    M, K = a.shape
    N = wg.shape[1]
    wspec = pl.BlockSpec((K, tn), lambda i, j: (0, j))
    return pl.pallas_call(
        _glu_kernel, grid=(M // tm, N // tn),
        in_specs=[pl.BlockSpec((tm, K), lambda i, j: (i, 0)), wspec, wspec],
        out_specs=pl.BlockSpec((tm, tn), lambda i, j: (i, j)),
        out_shape=jax.ShapeDtypeStruct((M, N), BF16),
        compiler_params=_params("parallel", "parallel"), name="glu_up")(a, wg, wu)


def ffn(h, w_gate, w_up, w_down):
    hid = glu_up(h, w_gate.astype(BF16), w_up.astype(BF16))
    d_ff = hid.shape[1]
    tk = d_ff // 2 if (d_ff // 2) % HEAD_DIM == 0 else d_ff
    return matmul(hid, w_down.astype(BF16), F32, tm=1024, tn=512, tk=tk)


def _t5_thresholds():
    nb = REL_BUCKETS // 2
    max_exact = nb // 2
    n = np.arange(0, 4 * REL_MAX_DIST)
    nf = np.maximum(n, 1).astype(np.float32)
    large = max_exact + (np.log(nf / np.float32(max_exact))
                         / np.float32(math.log(REL_MAX_DIST / max_exact))
                         * np.float32(nb - max_exact)).astype(np.int32)
    bucket = np.where(n < max_exact, n, np.minimum(large, nb - 1))
    assert np.all(np.diff(bucket) >= 0) and bucket[-1] == nb - 1
    return tuple(int(np.argmax(bucket >= b)) for b in range(1, nb))


T5_THRESHOLDS = _t5_thresholds()


def _t5_tile_kernel(rb_ref, var_ref, o_ref, *, dil, half, n_heads, mult, keys_on_rows):
    h = pl.program_id(0)
    v = pl.program_id(1)
    off, col_lo, col_hi = var_ref[3 * v], var_ref[3 * v + 1], var_ref[3 * v + 2]
    rows, cols = o_ref.shape[2], o_ref.shape[3]
    i = lax.broadcasted_iota(jnp.int32, (rows, cols), 0)
    j = lax.broadcasted_iota(jnp.int32, (rows, cols), 1)
    nb = REL_BUCKETS // 2
    delta = (i - j if keys_on_rows else j - i) + off
    rel = delta * dil
    n = jnp.abs(rel)
    vneg = jnp.full((rows, cols), rb_ref[h], F32)
    vpos = jnp.full((rows, cols), rb_ref[nb * n_heads + h], F32)
    for b, t in enumerate(T5_THRESHOLDS, start=1):
        c = n >= t
        vneg = jnp.where(c, rb_ref[b * n_heads + h], vneg)
        vpos = jnp.where(c, rb_ref[(nb + b) * n_heads + h], vpos)
    val = jnp.where(rel > 0, vpos, vneg)
    if mult != 1.0:
        val = val * mult
    if half is not None:
        ok = (jnp.abs(delta) <= half) & (j >= col_lo) & (j < col_hi)
        val = jnp.where(ok, val, MASKED)
    o_ref[0, 0] = val


def t5_tiles(rel_bias, offs, rows, cols, dil=1, half=None, col_lo=None, col_hi=None, mult=1.0,
             keys_on_rows=False):
    n_heads = rel_bias.shape[1]
    nv = len(offs)
    col_lo = (0,) * nv if col_lo is None else col_lo
    col_hi = (cols,) * nv if col_hi is None else col_hi
    variants = jnp.asarray(np.stack([offs, col_lo, col_hi], axis=1).reshape(-1), jnp.int32)
    kern = functools.partial(_t5_tile_kernel, dil=dil, half=half, n_heads=n_heads, mult=mult,
                             keys_on_rows=keys_on_rows)
    smem = pl.BlockSpec(memory_space=pltpu.SMEM)
    return pl.pallas_call(
        kern, grid=(n_heads, nv), in_specs=[smem, smem],
        out_specs=pl.BlockSpec((1, 1, rows, cols), lambda h, v: (v, h, 0, 0)),
        out_shape=jax.ShapeDtypeStruct((nv, n_heads, rows, cols), F32),
        compiler_params=_params("parallel", "parallel"), name="t5_tiles")(rel_bias.reshape(-1), variants)


def _sublane_groups(x):
    return x.reshape(x.shape[0] // SUBLANES, SUBLANES, x.shape[1])


def _store_scores(s_sc, mp_sc, idx, s):
    s_sc[idx] = s
    mp_sc[idx] = jnp.max(_sublane_groups(s), axis=0)


def _online_step(m_sc, acc_sc, idx, s_sc, mp_sc, vt_chunk):
    Tk, Tq = s_sc.shape[1:]
    m_old = m_sc[idx]
    m_cur = jnp.max(mp_sc[idx], axis=0, keepdims=True)
    m_new = jnp.maximum(m_old, jnp.broadcast_to(m_cur, m_old.shape))
    m_sc[idx] = m_new
    alpha = jnp.exp2(m_old - m_new)
    pv = None
    for c in range(Tk // KEY_CHUNK):
        ks = slice(c * KEY_CHUNK, (c + 1) * KEY_CHUNK)
        p = jnp.exp2(_sublane_groups(s_sc[idx, ks, :]) - m_new[None])
        d = jnp.dot(vt_chunk(ks), p.reshape(KEY_CHUNK, Tq).astype(BF16), preferred_element_type=F32)
        pv = d if pv is None else pv + d
    acc_sc[idx] = (_sublane_groups(acc_sc[idx]) * alpha[None]).reshape(acc_sc.shape[1:]) + pv


def _flash_init(m_sc, acc_sc):
    m_sc[...] = jnp.full(m_sc.shape, -jnp.inf, F32)
    acc_sc[...] = jnp.zeros(acc_sc.shape, F32)


def _values_transposed(v, n_heads, Tk):
    S = v.shape[0]
    dv = v.shape[1] // n_heads
    v_t = v.reshape(S // Tk, Tk, n_heads, dv).transpose(2, 0, 3, 1)
    return jnp.concatenate([v_t, jnp.ones((n_heads, S // Tk, ONES_ROWS, Tk), v.dtype)], axis=2)


def _pipelined_blocks(nk, scores, consume, buf0, buf1):
    assert nk % 2 == 0
    scores(0, *buf0)

    def body(j, carry):
        kb = 2 * j
        scores(kb + 1, *buf1)
        consume(kb, *buf0)
        scores(kb + 2, *buf0)
        consume(kb + 1, *buf1)
        return carry

    lax.fori_loop(0, nk // 2 - 1, body, 0)
    scores(nk - 1, *buf1)
    consume(nk - 2, *buf0)
    consume(nk - 1, *buf1)


def _attn_a_kernel(q_ref, k_ref, vt_ref, band_ref, lam_ref, g_ref, o_ref, m_sc, acc_sc,
                   s0_sc, s1_sc, mp0_sc, mp1_sc, *, Tk, nk, q_step, k_step, e_lo, e_hi, lam_init):
    qb = pl.program_id(1)
    dv = 2 * HEAD_DIM
    _flash_init(m_sc, acc_sc)

    def scores(kb, s_sc, mp_sc):
        off = pl.multiple_of(kb * Tk, Tk)
        bidx = jnp.clip(kb * k_step - qb * q_step, e_lo, e_hi) - e_lo
        for m in range(2):
            sl = slice(m * HEAD_DIM, (m + 1) * HEAD_DIM)
            _store_scores(s_sc, mp_sc, m,
                          _dot_nt(k_ref[pl.ds(off, Tk), sl], q_ref[:, sl]) + band_ref[bidx, 0])

    def consume(kb, s_sc, mp_sc):
        for m in range(2):
            _online_step(m_sc, acc_sc, m, s_sc, mp_sc, lambda ks: vt_ref[0, kb, :, ks])

    _pipelined_blocks(nk, scores, consume, (s0_sc, mp0_sc), (s1_sc, mp1_sc))
    lp = lam_ref[...]
    lam = (jnp.exp(jnp.sum(lp[0:1] * lp[1:2], axis=-1, keepdims=True))
           - jnp.exp(jnp.sum(lp[2:3] * lp[3:4], axis=-1, keepdims=True)) + lam_init)
    acc0, acc1 = acc_sc[0], acc_sc[1]
    o_t = acc0[:dv] / acc0[dv:dv + 1] - lam * (acc1[:dv] / acc1[dv:dv + 1])
    o_ref[...] = (_rms(o_t.T, g_ref[...]) * (1.0 - lam_init)).astype(o_ref.dtype)


def mixer_a(h, w_in, w_out, lam_p, subln_g, rel_bias, layer_idx, T=512, Tk=1024):
    S, D = h.shape
    n_heads = D // (2 * HEAD_DIM)
    width = 2 * HEAD_DIM
    colscale = jnp.concatenate([jnp.full((D,), SCALE * LOG2E, F32), jnp.ones((2 * D,), F32)])
    qkv = matmul(h, (w_in * colscale).astype(BF16), BF16)
    far = T5_THRESHOLDS[-1]
    unit = math.gcd(T, Tk)
    e_hi = -(-(far + T - 1) // unit)
    e_lo = -(-(far + Tk - 1) // unit)
    e_lo = -e_lo
    tiles = t5_tiles(rel_bias, [e * unit for e in range(e_lo, e_hi + 1)], Tk, T, mult=LOG2E,
                     keys_on_rows=True)
    n_tiles = e_hi - e_lo + 1
    nk = S // Tk
    v_t = _values_transposed(qkv[:, 2 * D:], n_heads, Tk)
    lam_init = 0.8 - 0.6 * math.exp(-0.3 * layer_idx)
    kern = functools.partial(_attn_a_kernel, Tk=Tk, nk=nk, q_step=T // unit, k_step=Tk // unit,
                             e_lo=e_lo, e_hi=e_hi, lam_init=lam_init)
    once = pl.Buffered(1)
    o = pl.pallas_call(
        kern, grid=(n_heads, S // T),
        in_specs=[pl.BlockSpec((T, width), lambda hh, qb: (qb, hh)),
                  pl.BlockSpec((S, width), lambda hh, qb: (0, n_heads + hh), pipeline_mode=once),
                  pl.BlockSpec((1, nk, width + ONES_ROWS, Tk), lambda hh, qb: (hh, 0, 0, 0),
                               pipeline_mode=once),
                  pl.BlockSpec((n_tiles, 1, Tk, T), lambda hh, qb: (0, hh, 0, 0), pipeline_mode=once),
                  pl.BlockSpec((4, HEAD_DIM), lambda hh, qb: (0, 0)),
                  pl.BlockSpec((1, width), lambda hh, qb: (0, 0))],
        out_specs=pl.BlockSpec((T, width), lambda hh, qb: (qb, hh)),
        out_shape=jax.ShapeDtypeStruct((S, D), BF16),
        scratch_shapes=[pltpu.VMEM((2, SUBLANES, T), F32), pltpu.VMEM((2, width + ONES_ROWS, T), F32),
                        pltpu.VMEM((2, Tk, T), F32), pltpu.VMEM((2, Tk, T), F32),
                        pltpu.VMEM((2, SUBLANES, T), F32), pltpu.VMEM((2, SUBLANES, T), F32)],
        compiler_params=_params("parallel", "arbitrary"),
        name="attn_a")(qkv, qkv, v_t, tiles, lam_p, subln_g.reshape(1, width))
    return matmul(o, w_out.astype(BF16), F32)


def _rope_kernel(x_ref, c_ref, sa_ref, sb_ref, g_ref, o_ref, *, n_q, n_chunks):
    c, sa, sb = c_ref[...], sa_ref[...], sb_ref[...]
    for ch in range(n_chunks):
        sl = slice(ch * HEAD_DIM, (ch + 1) * HEAD_DIM)
        y = _rms(x_ref[:, sl], g_ref[0:1] if ch < n_q else g_ref[1:2])
        y = y * c + pltpu.roll(y, HEAD_DIM - 1, 1) * sa + pltpu.roll(y, 1, 1) * sb
        if ch < n_q:
            y = y * (SCALE * LOG2E)
        o_ref[:, sl] = y.astype(o_ref.dtype)


def _rope_tables(S):
    pos = jnp.arange(S)
    n_freq = HEAD_DIM // 4
    freqs = ROPE_THETA ** (-jnp.arange(n_freq, dtype=F32) / n_freq)
    ang = jnp.concatenate([(pos // GRID_W).astype(F32)[:, None] * freqs,
                           (pos % GRID_W).astype(F32)[:, None] * freqs], axis=-1)
    cos, sin = jnp.cos(ang), jnp.sin(ang)
    zero = jnp.zeros_like(sin)
    c = jnp.stack([cos, cos], axis=-1).reshape(S, HEAD_DIM)
    sa = jnp.stack([-sin, zero], axis=-1).reshape(S, HEAD_DIM)
    sb = jnp.stack([zero, sin], axis=-1).reshape(S, HEAD_DIM)
    return c, sa, sb


def _attn_d_kernel(q_ref, k_ref, vt_ref, o_ref, m_sc, acc_sc, s0_sc, s1_sc, mp0_sc, mp1_sc,
                   *, Tk, nk, rep):
    _flash_init(m_sc, acc_sc)

    def scores(kb, s_sc, mp_sc):
        kblk = k_ref[pl.ds(pl.multiple_of(kb * Tk, Tk), Tk), :]
        for r in range(rep):
            _store_scores(s_sc, mp_sc, r, _dot_nt(kblk, q_ref[:, r * HEAD_DIM:(r + 1) * HEAD_DIM]))

    def consume(kb, s_sc, mp_sc):
        for r in range(rep):
            _online_step(m_sc, acc_sc, r, s_sc, mp_sc, lambda ks: vt_ref[0, kb, :, ks])

    _pipelined_blocks(nk, scores, consume, (s0_sc, mp0_sc), (s1_sc, mp1_sc))
    for r in range(rep):
        acc = acc_sc[r]
        o_t = acc[:HEAD_DIM] / acc[HEAD_DIM:HEAD_DIM + 1]
        o_ref[:, r * HEAD_DIM:(r + 1) * HEAD_DIM] = o_t.T.astype(o_ref.dtype)


def mixer_d(h, w_in, w_out, qk_g, T=512, Tk=1024, tm=256):
    S, D = h.shape
    n_q = D // HEAD_DIM
    n_kv = D_KV_HEADS
    rep = n_q // n_kv
    nqk = (n_q + n_kv) * HEAD_DIM
    qk = matmul(h, w_in[:, :nqk].astype(BF16), F32)
    v = matmul(h, w_in[:, nqk:].astype(BF16), BF16)
    c, sa, sb = _rope_tables(S)
    tab = pl.BlockSpec((tm, HEAD_DIM), lambda i: (i, 0))
    qk = pl.pallas_call(
        functools.partial(_rope_kernel, n_q=n_q, n_chunks=n_q + n_kv), grid=(S // tm,),
        in_specs=[pl.BlockSpec((tm, nqk), lambda i: (i, 0)), tab, tab, tab,
                  pl.BlockSpec((2, HEAD_DIM), lambda i: (0, 0))],
        out_specs=pl.BlockSpec((tm, nqk), lambda i: (i, 0)),
        out_shape=jax.ShapeDtypeStruct((S, nqk), BF16),
        compiler_params=_params("parallel"), name="qk_norm_rope")(qk, c, sa, sb, qk_g)
    nk = S // Tk
    v_t = _values_transposed(v, n_kv, Tk)
    vt_rows = HEAD_DIM + ONES_ROWS
    kern = functools.partial(_attn_d_kernel, Tk=Tk, nk=nk, rep=rep)
    o = pl.pallas_call(
        kern, grid=(n_kv, S // T),
        in_specs=[pl.BlockSpec((T, rep * HEAD_DIM), lambda g, qb: (qb, g)),
                  pl.BlockSpec((S, HEAD_DIM), lambda g, qb: (0, n_q + g)),
                  pl.BlockSpec((1, nk, vt_rows, Tk), lambda g, qb: (g, 0, 0, 0))],
        out_specs=pl.BlockSpec((T, rep * HEAD_DIM), lambda g, qb: (qb, g)),
        out_shape=jax.ShapeDtypeStruct((S, D), BF16),
        scratch_shapes=[pltpu.VMEM((rep, SUBLANES, T), F32),
                        pltpu.VMEM((rep, vt_rows, T), F32),
                        pltpu.VMEM((rep, Tk, T), F32), pltpu.VMEM((rep, Tk, T), F32),
                        pltpu.VMEM((rep, SUBLANES, T), F32), pltpu.VMEM((rep, SUBLANES, T), F32)],
        compiler_params=_params("parallel", "arbitrary"), name="attn_d")(qk, qk, v_t)
    return matmul(o, w_out.astype(BF16), F32)


def _lane_tiles(x):
    return [x[:, c * HEAD_DIM:(c + 1) * HEAD_DIM] for c in range(x.shape[1] // HEAD_DIM)]


def _win_attn_kernel(q_ref, kp_ref, kc_ref, kn_ref, vp_ref, vc_ref, vn_ref, bias_ref, o_ref,
                     *lse_ref, n_heads, tq, lead):
    lse_cols = []

    def scores(h):
        sl = slice(h * HEAD_DIM, (h + 1) * HEAD_DIM)
        q = q_ref[:, sl]
        ks = (kp_ref[:, sl], kc_ref[:, sl], kn_ref[:, sl])
        return [_dot_nt(q, ks[j]) + bias_ref[0, h, :, j * tq:(j + 1) * tq] for j in range(3)]

    def finish(h, ss):
        sl = slice(h * HEAD_DIM, (h + 1) * HEAD_DIM)
        vs = (vp_ref[:, sl], vc_ref[:, sl], vn_ref[:, sl])
        m = jnp.max(functools.reduce(jnp.maximum, [t for s in ss for t in _lane_tiles(s)]),
                    axis=-1, keepdims=True)
        ps = [jnp.exp(s - m) for s in ss]
        l = jnp.sum(functools.reduce(lambda a, b: a + b, [t for p in ps for t in _lane_tiles(p)]),
                    axis=-1, keepdims=True)
        acc = (jnp.dot(ps[0].astype(BF16), vs[0], preferred_element_type=F32)
               + jnp.dot(ps[1].astype(BF16), vs[1], preferred_element_type=F32)
               + jnp.dot(ps[2].astype(BF16), vs[2], preferred_element_type=F32))
        o_ref[:, sl] = (acc / l).astype(o_ref.dtype)
        lse_cols.append(m + jnp.log(l))

    pending = [scores(h) for h in range(min(lead, n_heads))]
    for h in range(n_heads):
        if h + lead < n_heads:
            pending.append(scores(h + lead))
        finish(h, pending.pop(0))
    if lse_ref:
        lse_ref[0][...] = jnp.concatenate(lse_cols, axis=1)


def _edge_variant(i, nblk):
    return jnp.where(i == 0, 0, jnp.where(i == nblk - 1, 2, 1))


def _merge_b_kernel(o0_ref, o1_ref, o2_ref, lse_ref, out_ref, *, n_heads):
    lse = lse_ref[...]
    w = jnp.exp(lse - jnp.max(lse, axis=0, keepdims=True))
    w = w / jnp.sum(w, axis=0, keepdims=True)
    for h in range(n_heads):
        sl = slice(h * HEAD_DIM, (h + 1) * HEAD_DIM)
        out_ref[:, sl] = (w[0, :, h:h + 1] * o0_ref[:, sl] + w[1, :, h:h + 1] * o1_ref[:, sl]
                          + w[2, :, h:h + 1] * o2_ref[:, sl]).astype(out_ref.dtype)


def mixer_b(h, w_in, w_out, rel_bias, tq=128, tm=256):
    S, D = h.shape
    G = len(B_GROUPS)
    n_heads = rel_bias.shape[1]
    width = n_heads * HEAD_DIM
    colscale = jnp.concatenate([jnp.full((G * width,), SCALE, F32), jnp.ones((2 * G * width,), F32)])
    w = (w_in * colscale).astype(BF16).reshape(D, 3, G, width)
    outs, lses = [], []
    for g, (window, dil) in enumerate(B_GROUPS):
        half = window // 2 // dil
        assert half <= tq
        L = S // dil
        nblk = L // tq
        assert nblk >= 2

        def by_residue(a, dil=dil, L=L):
            return a if dil == 1 else a.reshape(L, dil, -1).transpose(1, 0, 2).reshape(S, -1)

        def by_position(a, dil=dil, L=L):
            return a if dil == 1 else a.reshape(dil, L, -1).transpose(1, 0, 2).reshape(S, -1)

        qkv = matmul(by_residue(h), w[:, :, g].reshape(D, 3 * width), BF16)
        bias = t5_tiles(rel_bias, [-tq] * 3, tq, 3 * tq, dil=dil, half=half,
                        col_lo=(tq, 0, 0), col_hi=(3 * tq, 3 * tq, 2 * tq))

        def spec(part, di, nblk=nblk):
            return pl.BlockSpec(
                (tq, width), lambda r, i: (r * nblk + jnp.clip(i + di, 0, nblk - 1), part))

        o_g, lse_g = pl.pallas_call(
            functools.partial(_win_attn_kernel, n_heads=n_heads, tq=tq, lead=4), grid=(dil, nblk),
            in_specs=[spec(0, 0), spec(1, -1), spec(1, 0), spec(1, 1),
                      spec(2, -1), spec(2, 0), spec(2, 1),
                      pl.BlockSpec((1, n_heads, tq, 3 * tq),
                                   lambda r, i, nblk=nblk: (_edge_variant(i, nblk), 0, 0, 0))],
            out_specs=[pl.BlockSpec((tq, width), lambda r, i, nblk=nblk: (r * nblk + i, 0)),
                       pl.BlockSpec((tq, n_heads), lambda r, i, nblk=nblk: (r * nblk + i, 0))],
            out_shape=[jax.ShapeDtypeStruct((S, width), F32),
                       jax.ShapeDtypeStruct((S, n_heads), F32)],
            compiler_params=_params("parallel", "parallel"),
            name=f"attn_b{g}")(qkv, qkv, qkv, qkv, qkv, qkv, qkv, bias)
        outs.append(by_position(o_g))
        lses.append(by_position(lse_g))
    row = pl.BlockSpec((tm, width), lambda i: (i, 0))
    o = pl.pallas_call(
        functools.partial(_merge_b_kernel, n_heads=n_heads), grid=(S // tm,),
        in_specs=[row, row, row, pl.BlockSpec((G, tm, n_heads), lambda i: (0, i, 0))],
        out_specs=row, out_shape=jax.ShapeDtypeStruct((S, width), BF16),
        compiler_params=_params("parallel"), name="merge_b")(*outs, jnp.stack(lses, 0))
    return matmul(o, w_out.astype(BF16), F32)


def _c_bias_kernel(rpb_ref, o_ref, *, rq, n_dr, n_dc):
    h = pl.program_id(0)
    W = GRID_W
    qc = lax.broadcasted_iota(jnp.int32, (W, 2 * W), 0)
    lane = lax.broadcasted_iota(jnp.int32, (W, 2 * W), 1)
    kc = lane & (W - 1)
    dc = kc - qc
    c0 = jnp.clip(qc - C_WIN_C // 2, 0, W - C_WIN_C)
    ok_c = (kc >= c0) & (kc < c0 + C_WIN_C)
    masked = jnp.full((W, 2 * W), MASKED, F32)
    base = h * (n_dr * n_dc)
    sub = []
    for a in range(n_dr):
        val = masked
        for b in range(n_dc):
            val = jnp.where(dc == b - (C_WIN_C - 1), rpb_ref[base + a * n_dc + b], val)
        sub.append(jnp.where(ok_c, val, MASKED))
    for v in range(3):
        for qr in range(rq):
            for jv in range(3 * rq // 2):
                halves = []
                for kr in (2 * jv, 2 * jv + 1):
                    lo, hi = ((rq, 3 * rq - 1), (qr, qr + C_WIN_R - 1), (0, C_WIN_R - 1))[v]
                    halves.append(sub[kr - qr + C_WIN_R - 1 - rq] if lo <= kr <= hi else masked)
                o_ref[v, 0, qr * W:(qr + 1) * W, jv * 2 * W:(jv + 1) * 2 * W] = jnp.where(
                    lane < W, halves[0], halves[1])


def mixer_c(h, w_in, w_out, rpb, heads_per_step=8):
    S, D = h.shape
    n_heads = D // HEAD_DIM
    rows = S // GRID_W
    rq = C_WIN_R // 2
    tq = rq * GRID_W
    nblk = rows // rq
    assert rows >= C_WIN_R and nblk >= 2 and 2 * GRID_W == HEAD_DIM
    n_dr, n_dc = 2 * C_WIN_R - 1, 2 * C_WIN_C - 1
    colscale = jnp.concatenate([jnp.full((D,), SCALE, F32), jnp.ones((2 * D,), F32)])
    qkv = matmul(h, (w_in * colscale).astype(BF16), BF16)
    bias = pl.pallas_call(
        functools.partial(_c_bias_kernel, rq=rq, n_dr=n_dr, n_dc=n_dc), grid=(n_heads,),
        in_specs=[pl.BlockSpec(memory_space=pltpu.SMEM)],
        out_specs=pl.BlockSpec((3, 1, tq, 3 * tq), lambda hh: (0, hh, 0, 0)),
        out_shape=jax.ShapeDtypeStruct((3, n_heads, tq, 3 * tq), F32),
        compiler_params=_params("parallel"), name="c_bias")(rpb.reshape(-1))
    hs = heads_per_step
    n_hg = n_heads // hs
    width = hs * HEAD_DIM

    def spec(part, di):
        return pl.BlockSpec(
            (tq, width), lambda hg, i: (jnp.clip(i + di, 0, nblk - 1), part * n_hg + hg))

    o = pl.pallas_call(
        functools.partial(_win_attn_kernel, n_heads=hs, tq=tq, lead=2), grid=(n_hg, nblk),
        in_specs=[spec(0, 0), spec(1, -1), spec(1, 0), spec(1, 1),
                  spec(2, -1), spec(2, 0), spec(2, 1),
                  pl.BlockSpec((1, hs, tq, 3 * tq),
                               lambda hg, i: (_edge_variant(i, nblk), hg, 0, 0))],
        out_specs=pl.BlockSpec((tq, width), lambda hg, i: (i, hg)),
        out_shape=jax.ShapeDtypeStruct((S, D), BF16),
        compiler_params=_params("parallel", "parallel"),
        name="attn_c")(qkv, qkv, qkv, qkv, qkv, qkv, qkv, bias)
    return matmul(o, w_out.astype(BF16), F32)


def kernel(x, rel_bias, norm_g, a_w_in, a_w_out, a_lambda, a_subln, b_w_in, b_w_out, c_w_in, c_w_out,
           c_rpb, d_w_in, d_w_out, d_qk_norm, ffn_w_gate, ffn_w_up, ffn_w_down):
    B, S, D = x.shape
    assert B == 1
    depth = norm_g.shape[0]
    xs = x.reshape(S, D)
    hn = prenorm(xs, norm_g[0, 0])
    for i in range(depth):
        m, j = i % N_MIXERS, i // N_MIXERS
        if m == 0:
            y = mixer_a(hn, a_w_in[j], a_w_out[j], a_lambda[j], a_subln[j], rel_bias, i)
        elif m == 1:
            y = mixer_b(hn, b_w_in[j], b_w_out[j], rel_bias)
        elif m == 2:
            y = mixer_c(hn, c_w_in[j], c_w_out[j], c_rpb[j])
        else:
            y = mixer_d(hn, d_w_in[j], d_w_out[j], d_qk_norm[j])
        xs, hn = resid(xs, y, norm_g[i, 1], norm_g[i, 2])
        y = ffn(hn, ffn_w_gate[i], ffn_w_up[i], ffn_w_down[i])
        if i + 1 < depth:
            xs, hn = resid(xs, y, norm_g[i, 3], norm_g[i + 1, 0])
        else:
            xs = resid(xs, y, norm_g[i, 3])
    return xs.reshape(B, S, D)
```

```python
import functools
import math

import numpy as np
import jax
import jax.numpy as jnp
from jax import lax
from jax.experimental import pallas as pl
from jax.experimental.pallas import tpu as pltpu

F32 = jnp.float32
BF16 = jnp.bfloat16

HEAD_DIM = 128
SCALE = HEAD_DIM ** -0.5
LOG2E = math.log2(math.e)
GRID_W = 64
EPS = 1e-6
REL_BUCKETS = 32
REL_MAX_DIST = 1024
B_GROUPS = ((128, 1), (512, 4), (2048, 16))
C_WIN_R = 8
C_WIN_C = 16
D_KV_HEADS = 8
ROPE_THETA = 10000.0
N_MIXERS = 4

KEY_CHUNK = 256
SUBLANES = 8
ONES_ROWS = 2 * SUBLANES
MASKED = -1e30
VMEM_LIMIT_BYTES = 56 * 2 ** 20


def _params(*semantics):
    return pltpu.CompilerParams(dimension_semantics=semantics, vmem_limit_bytes=VMEM_LIMIT_BYTES)


def _rms(x, g):
    ms = jnp.mean(x * x, axis=-1, keepdims=True)
    return x * lax.rsqrt(ms + EPS) * g


def _dot_nt(a, b):
    return lax.dot_general(a, b, (((1,), (1,)), ((), ())), preferred_element_type=F32)


def _prenorm_kernel(x_ref, g_ref, h_ref):
    h_ref[...] = _rms(x_ref[...], g_ref[...]).astype(h_ref.dtype)


def prenorm(x, g, tm=256):
    S, D = x.shape
    row = pl.BlockSpec((tm, D), lambda i: (i, 0))
    vec = pl.BlockSpec((1, D), lambda i: (0, 0))
    return pl.pallas_call(
        _prenorm_kernel, grid=(S // tm,), in_specs=[row, vec], out_specs=row,
        out_shape=jax.ShapeDtypeStruct((S, D), BF16), compiler_params=_params("parallel"),
        name="prenorm")(x, g.reshape(1, D))


def _resid_kernel(x_ref, y_ref, g1_ref, g2_ref, xo_ref, h_ref):
    x = x_ref[...] + _rms(y_ref[...], g1_ref[...])
    xo_ref[...] = x
    h_ref[...] = _rms(x, g2_ref[...]).astype(h_ref.dtype)


def _resid_last_kernel(x_ref, y_ref, g1_ref, xo_ref):
    xo_ref[...] = x_ref[...] + _rms(y_ref[...], g1_ref[...])


def resid(x, y, g_post, g_next=None, tm=256):
    S, D = x.shape
    row = pl.BlockSpec((tm, D), lambda i: (i, 0))
    vec = pl.BlockSpec((1, D), lambda i: (0, 0))
    if g_next is None:
        return pl.pallas_call(
            _resid_last_kernel, grid=(S // tm,), in_specs=[row, row, vec], out_specs=row,
            out_shape=jax.ShapeDtypeStruct((S, D), F32), compiler_params=_params("parallel"),
            name="resid_last")(x, y, g_post.reshape(1, D))
    return pl.pallas_call(
        _resid_kernel, grid=(S // tm,), in_specs=[row, row, vec, vec], out_specs=[row, row],
        out_shape=[jax.ShapeDtypeStruct((S, D), F32), jax.ShapeDtypeStruct((S, D), BF16)],
        compiler_params=_params("parallel"),
        name="resid")(x, y, g_post.reshape(1, D), g_next.reshape(1, D))


def _mm_kernel(a_ref, b_ref, o_ref):
    o_ref[...] = jnp.dot(a_ref[...], b_ref[...], preferred_element_type=F32).astype(o_ref.dtype)


def _mm_acc_kernel(a_ref, b_ref, o_ref, acc_ref):
    k = pl.program_id(2)

    @pl.when(k == 0)
    def _():
        acc_ref[...] = jnp.zeros_like(acc_ref)

    acc_ref[...] += jnp.dot(a_ref[...], b_ref[...], preferred_element_type=F32)

    @pl.when(k == pl.num_programs(2) - 1)
    def _():
        o_ref[...] = acc_ref[...].astype(o_ref.dtype)


def matmul(a, b, out_dtype, tm=1024, tn=None, tk=None):
    M, K = a.shape
    N = b.shape[1]
    tk = K if tk is None else tk
    if tn is None:
        tn = 1024 if N % 1024 == 0 else 512
    out_shape = jax.ShapeDtypeStruct((M, N), out_dtype)
    if tk == K:
        return pl.pallas_call(
            _mm_kernel, grid=(M // tm, N // tn),
            in_specs=[pl.BlockSpec((tm, K), lambda i, j: (i, 0)),
                      pl.BlockSpec((K, tn), lambda i, j: (0, j))],
            out_specs=pl.BlockSpec((tm, tn), lambda i, j: (i, j)),
            out_shape=out_shape, compiler_params=_params("parallel", "parallel"),
            name="matmul")(a, b)
    return pl.pallas_call(
        _mm_acc_kernel, grid=(M // tm, N // tn, K // tk),
        in_specs=[pl.BlockSpec((tm, tk), lambda i, j, k: (i, k)),
                  pl.BlockSpec((tk, tn), lambda i, j, k: (k, j))],
        out_specs=pl.BlockSpec((tm, tn), lambda i, j, k: (i, j)),
        out_shape=out_shape, scratch_shapes=[pltpu.VMEM((tm, tn), F32)],
        compiler_params=_params("parallel", "parallel", "arbitrary"),
        name="matmul_acc")(a, b)


def _glu_kernel(a_ref, wg_ref, wu_ref, o_ref):
    a = a_ref[...]
    g = jnp.dot(a, wg_ref[...], preferred_element_type=F32)
    u = jnp.dot(a, wu_ref[...], preferred_element_type=F32)
    o_ref[...] = (g * (1.0 / (1.0 + jnp.exp(-g))) * u).astype(o_ref.dtype)


def glu_up(a, wg, wu, tm=2048, tn=256):
    M, K = a.shape
    N = wg.shape[1]
    wspec = pl.BlockSpec((K, tn), lambda i, j: (0, j))
    return pl.pallas_call(
        _glu_kernel, grid=(M // tm, N // tn),
        in_specs=[pl.BlockSpec((tm, K), lambda i, j: (i, 0)), wspec, wspec],
        out_specs=pl.BlockSpec((tm, tn), lambda i, j: (i, j)),
        out_shape=jax.ShapeDtypeStruct((M, N), BF16),
        compiler_params=_params("parallel", "parallel"), name="glu_up")(a, wg, wu)


def ffn(h, w_gate, w_up, w_down):
    hid = glu_up(h, w_gate.astype(BF16), w_up.astype(BF16))
    d_ff = hid.shape[1]
    tk = d_ff // 2 if (d_ff // 2) % HEAD_DIM == 0 else d_ff
    return matmul(hid, w_down.astype(BF16), F32, tm=1024, tn=512, tk=tk)


def _t5_thresholds():
    nb = REL_BUCKETS // 2
    max_exact = nb // 2
    n = np.arange(0, 4 * REL_MAX_DIST)
    nf = np.maximum(n, 1).astype(np.float32)
    large = max_exact + (np.log(nf / np.float32(max_exact))
                         / np.float32(math.log(REL_MAX_DIST / max_exact))
                         * np.float32(nb - max_exact)).astype(np.int32)
    bucket = np.where(n < max_exact, n, np.minimum(large, nb - 1))
    assert np.all(np.diff(bucket) >= 0) and bucket[-1] == nb - 1
    return tuple(int(np.argmax(bucket >= b)) for b in range(1, nb))


T5_THRESHOLDS = _t5_thresholds()


def _t5_tile_kernel(rb_ref, var_ref, o_ref, *, dil, half, n_heads, mult, keys_on_rows):
    h = pl.program_id(0)
    v = pl.program_id(1)
    off, col_lo, col_hi = var_ref[3 * v], var_ref[3 * v + 1], var_ref[3 * v + 2]
    rows, cols = o_ref.shape[2], o_ref.shape[3]
    i = lax.broadcasted_iota(jnp.int32, (rows, cols), 0)
    j = lax.broadcasted_iota(jnp.int32, (rows, cols), 1)
    nb = REL_BUCKETS // 2
    delta = (i - j if keys_on_rows else j - i) + off
    rel = delta * dil
    n = jnp.abs(rel)
    vneg = jnp.full((rows, cols), rb_ref[h], F32)
    vpos = jnp.full((rows, cols), rb_ref[nb * n_heads + h], F32)
    for b, t in enumerate(T5_THRESHOLDS, start=1):
        c = n >= t
        vneg = jnp.where(c, rb_ref[b * n_heads + h], vneg)
        vpos = jnp.where(c, rb_ref[(nb + b) * n_heads + h], vpos)
    val = jnp.where(rel > 0, vpos, vneg)
    if mult != 1.0:
        val = val * mult
    if half is not None:
        ok = (jnp.abs(delta) <= half) & (j >= col_lo) & (j < col_hi)
        val = jnp.where(ok, val, MASKED)
    o_ref[0, 0] = val


def t5_tiles(rel_bias, offs, rows, cols, dil=1, half=None, col_lo=None, col_hi=None, mult=1.0,
             keys_on_rows=False):
    n_heads = rel_bias.shape[1]
    nv = len(offs)
    col_lo = (0,) * nv if col_lo is None else col_lo
    col_hi = (cols,) * nv if col_hi is None else col_hi
    variants = jnp.asarray(np.stack([offs, col_lo, col_hi], axis=1).reshape(-1), jnp.int32)
    kern = functools.partial(_t5_tile_kernel, dil=dil, half=half, n_heads=n_heads, mult=mult,
                             keys_on_rows=keys_on_rows)
    smem = pl.BlockSpec(memory_space=pltpu.SMEM)
    return pl.pallas_call(
        kern, grid=(n_heads, nv), in_specs=[smem, smem],
        out_specs=pl.BlockSpec((1, 1, rows, cols), lambda h, v: (v, h, 0, 0)),
        out_shape=jax.ShapeDtypeStruct((nv, n_heads, rows, cols), F32),
        compiler_params=_params("parallel", "parallel"), name="t5_tiles")(rel_bias.reshape(-1), variants)


def _sublane_groups(x):
    return x.reshape(x.shape[0] // SUBLANES, SUBLANES, x.shape[1])


def _store_scores(s_sc, mp_sc, idx, s):
    s_sc[idx] = s
    mp_sc[idx] = jnp.max(_sublane_groups(s), axis=0)


def _online_step(m_sc, acc_sc, idx, s_sc, mp_sc, vt_chunk):
    Tk, Tq = s_sc.shape[1:]
    m_old = m_sc[idx]
    m_cur = jnp.max(mp_sc[idx], axis=0, keepdims=True)
    m_new = jnp.maximum(m_old, jnp.broadcast_to(m_cur, m_old.shape))
    m_sc[idx] = m_new
    alpha = jnp.exp2(m_old - m_new)
    pv = None
    for c in range(Tk // KEY_CHUNK):
        ks = slice(c * KEY_CHUNK, (c + 1) * KEY_CHUNK)
        p = jnp.exp2(_sublane_groups(s_sc[idx, ks, :]) - m_new[None])
        d = jnp.dot(vt_chunk(ks), p.reshape(KEY_CHUNK, Tq).astype(BF16), preferred_element_type=F32)
        pv = d if pv is None else pv + d
    acc_sc[idx] = (_sublane_groups(acc_sc[idx]) * alpha[None]).reshape(acc_sc.shape[1:]) + pv


def _flash_init(m_sc, acc_sc):
    m_sc[...] = jnp.full(m_sc.shape, -jnp.inf, F32)
    acc_sc[...] = jnp.zeros(acc_sc.shape, F32)


def _values_transposed(v, n_heads, Tk):
    S = v.shape[0]
    dv = v.shape[1] // n_heads
    v_t = v.reshape(S // Tk, Tk, n_heads, dv).transpose(2, 0, 3, 1)
    return jnp.concatenate([v_t, jnp.ones((n_heads, S // Tk, ONES_ROWS, Tk), v.dtype)], axis=2)


def _pipelined_blocks(nk, parts, scores, consume, buf0, buf1):
    assert nk % 2 == 0
    for part in parts:
        scores(0, *buf0, part)

    def step(kb, cur, nxt):
        for part in parts:
            scores(kb + 1, *nxt, part)
            consume(kb, *cur, part)

    def body(j, carry):
        step(2 * j, buf0, buf1)
        step(2 * j + 1, buf1, buf0)
        return carry

    lax.fori_loop(0, nk // 2 - 1, body, 0)
    step(nk - 2, buf0, buf1)
    for part in parts:
        consume(nk - 1, *buf1, part)


def _attn_a_kernel(q_ref, k_ref, vt_ref, band_ref, lam_ref, g_ref, o_ref, m_sc, acc_sc,
                   s0_sc, s1_sc, mp0_sc, mp1_sc, *, Tk, nk, q_step, k_step, e_lo, e_hi, lam_init):
    qb = pl.program_id(1)
    dv = 2 * HEAD_DIM
    _flash_init(m_sc, acc_sc)

    def scores(kb, s_sc, mp_sc, part):
        off = pl.multiple_of(kb * Tk, Tk)
        bidx = jnp.clip(kb * k_step - qb * q_step, e_lo, e_hi) - e_lo
        for m in part:
            sl = slice(m * HEAD_DIM, (m + 1) * HEAD_DIM)
            _store_scores(s_sc, mp_sc, m,
                          _dot_nt(k_ref[0, pl.ds(off, Tk), sl], q_ref[:, sl]) + band_ref[bidx, 0])

    def consume(kb, s_sc, mp_sc, part):
        for m in part:
            _online_step(m_sc, acc_sc, m, s_sc, mp_sc, lambda ks: vt_ref[0, kb, :, ks])

    _pipelined_blocks(nk, ((0, 1),), scores, consume, (s0_sc, mp0_sc), (s1_sc, mp1_sc))
    lp = lam_ref[...]
    lam = (jnp.exp(jnp.sum(lp[0:1] * lp[1:2], axis=-1, keepdims=True))
           - jnp.exp(jnp.sum(lp[2:3] * lp[3:4], axis=-1, keepdims=True)) + lam_init)
    acc0, acc1 = acc_sc[0], acc_sc[1]
    o_t = acc0[:dv] / acc0[dv:dv + 1] - lam * (acc1[:dv] / acc1[dv:dv + 1])
    o_ref[...] = (_rms(o_t.T, g_ref[...]) * (1.0 - lam_init)).astype(o_ref.dtype)


def mixer_a(h, w_in, w_out, lam_p, subln_g, rel_bias, layer_idx, T=512, Tk=1024):
    S, D = h.shape
    n_heads = D // (2 * HEAD_DIM)
    width = 2 * HEAD_DIM
    colscale = jnp.concatenate([jnp.full((D,), SCALE * LOG2E, F32), jnp.ones((2 * D,), F32)])
    qkv = matmul(h, (w_in * colscale).astype(BF16), BF16)
    far = T5_THRESHOLDS[-1]
    unit = math.gcd(T, Tk)
    e_hi = -(-(far + T - 1) // unit)
    e_lo = -(-(far + Tk - 1) // unit)
    e_lo = -e_lo
    tiles = t5_tiles(rel_bias, [e * unit for e in range(e_lo, e_hi + 1)], Tk, T, mult=LOG2E,
                     keys_on_rows=True)
    n_tiles = e_hi - e_lo + 1
    nk = S // Tk
    v_t = _values_transposed(qkv[:, 2 * D:], n_heads, Tk)
    k_hm = qkv[:, D:2 * D].reshape(S, n_heads, width).transpose(1, 0, 2)
    lam_init = 0.8 - 0.6 * math.exp(-0.3 * layer_idx)
    kern = functools.partial(_attn_a_kernel, Tk=Tk, nk=nk, q_step=T // unit, k_step=Tk // unit,
                             e_lo=e_lo, e_hi=e_hi, lam_init=lam_init)
    once = pl.Buffered(1)
    o = pl.pallas_call(
        kern, grid=(n_heads, S // T),
        in_specs=[pl.BlockSpec((T, width), lambda hh, qb: (qb, hh)),
                  pl.BlockSpec((1, S, width), lambda hh, qb: (hh, 0, 0), pipeline_mode=once),
                  pl.BlockSpec((1, nk, width + ONES_ROWS, Tk), lambda hh, qb: (hh, 0, 0, 0),
                               pipeline_mode=once),
                  pl.BlockSpec((n_tiles, 1, Tk, T), lambda hh, qb: (0, hh, 0, 0), pipeline_mode=once),
                  pl.BlockSpec((4, HEAD_DIM), lambda hh, qb: (0, 0)),
                  pl.BlockSpec((1, width), lambda hh, qb: (0, 0))],
        out_specs=pl.BlockSpec((T, width), lambda hh, qb: (qb, hh)),
        out_shape=jax.ShapeDtypeStruct((S, D), BF16),
        scratch_shapes=[pltpu.VMEM((2, SUBLANES, T), F32), pltpu.VMEM((2, width + ONES_ROWS, T), F32),
                        pltpu.VMEM((2, Tk, T), F32), pltpu.VMEM((2, Tk, T), F32),
                        pltpu.VMEM((2, SUBLANES, T), F32), pltpu.VMEM((2, SUBLANES, T), F32)],
        compiler_params=_params("parallel", "arbitrary"),
        name="attn_a")(qkv, k_hm, v_t, tiles, lam_p, subln_g.reshape(1, width))
    return matmul(o, w_out.astype(BF16), F32)


def _rope_kernel(x_ref, c_ref, sa_ref, sb_ref, g_ref, oq_ref, ok_ref, *, n_q, n_chunks):
    c, sa, sb = c_ref[...], sa_ref[...], sb_ref[...]
    for ch in range(n_chunks):
        sl = slice(ch * HEAD_DIM, (ch + 1) * HEAD_DIM)
        y = _rms(x_ref[:, sl], g_ref[0:1] if ch < n_q else g_ref[1:2])
        y = y * c + pltpu.roll(y, HEAD_DIM - 1, 1) * sa + pltpu.roll(y, 1, 1) * sb
        if ch < n_q:
            oq_ref[:, sl] = (y * (SCALE * LOG2E)).astype(oq_ref.dtype)
        else:
            ok_ref[ch - n_q] = y.astype(ok_ref.dtype)


def _rope_tables(S):
    pos = jnp.arange(S)
    n_freq = HEAD_DIM // 4
    freqs = ROPE_THETA ** (-jnp.arange(n_freq, dtype=F32) / n_freq)
    ang = jnp.concatenate([(pos // GRID_W).astype(F32)[:, None] * freqs,
                           (pos % GRID_W).astype(F32)[:, None] * freqs], axis=-1)
    cos, sin = jnp.cos(ang), jnp.sin(ang)
    zero = jnp.zeros_like(sin)
    c = jnp.stack([cos, cos], axis=-1).reshape(S, HEAD_DIM)
    sa = jnp.stack([-sin, zero], axis=-1).reshape(S, HEAD_DIM)
    sb = jnp.stack([zero, sin], axis=-1).reshape(S, HEAD_DIM)
    return c, sa, sb


def _attn_d_kernel(q_ref, k_ref, vt_ref, o_ref, m_sc, acc_sc, s0_sc, s1_sc, mp0_sc, mp1_sc,
                   *, Tk, nk, rep):
    _flash_init(m_sc, acc_sc)

    def scores(kb, s_sc, mp_sc, part):
        kblk = k_ref[0, pl.ds(pl.multiple_of(kb * Tk, Tk), Tk), :]
        for r in part:
            _store_scores(s_sc, mp_sc, r, _dot_nt(kblk, q_ref[:, r * HEAD_DIM:(r + 1) * HEAD_DIM]))

    def consume(kb, s_sc, mp_sc, part):
        for r in part:
            _online_step(m_sc, acc_sc, r, s_sc, mp_sc, lambda ks: vt_ref[0, kb, :, ks])

    parts = tuple(tuple(range(r, min(r + 2, rep))) for r in range(0, rep, 2))
    _pipelined_blocks(nk, parts, scores, consume, (s0_sc, mp0_sc), (s1_sc, mp1_sc))
    for r in range(rep):
        acc = acc_sc[r]
        o_t = acc[:HEAD_DIM] / acc[HEAD_DIM:HEAD_DIM + 1]
        o_ref[:, r * HEAD_DIM:(r + 1) * HEAD_DIM] = o_t.T.astype(o_ref.dtype)


def mixer_d(h, w_in, w_out, qk_g, T=512, Tk=1024, tm=256):
    S, D = h.shape
    n_q = D // HEAD_DIM
    n_kv = D_KV_HEADS
    rep = n_q // n_kv
    nqk = (n_q + n_kv) * HEAD_DIM
    qk = matmul(h, w_in[:, :nqk].astype(BF16), F32)
    v = matmul(h, w_in[:, nqk:].astype(BF16), BF16)
    c, sa, sb = _rope_tables(S)
    tab = pl.BlockSpec((tm, HEAD_DIM), lambda i: (i, 0))
    q, k = pl.pallas_call(
        functools.partial(_rope_kernel, n_q=n_q, n_chunks=n_q + n_kv), grid=(S // tm,),
        in_specs=[pl.BlockSpec((tm, nqk), lambda i: (i, 0)), tab, tab, tab,
                  pl.BlockSpec((2, HEAD_DIM), lambda i: (0, 0))],
        out_specs=[pl.BlockSpec((tm, D), lambda i: (i, 0)),
                   pl.BlockSpec((n_kv, tm, HEAD_DIM), lambda i: (0, i, 0))],
        out_shape=[jax.ShapeDtypeStruct((S, D), BF16),
                   jax.ShapeDtypeStruct((n_kv, S, HEAD_DIM), BF16)],
        compiler_params=_params("parallel"), name="qk_norm_rope")(qk, c, sa, sb, qk_g)
    nk = S // Tk
    v_t = _values_transposed(v, n_kv, Tk)
    vt_rows = HEAD_DIM + ONES_ROWS
    kern = functools.partial(_attn_d_kernel, Tk=Tk, nk=nk, rep=rep)
    o = pl.pallas_call(
        kern, grid=(n_kv, S // T),
        in_specs=[pl.BlockSpec((T, rep * HEAD_DIM), lambda g, qb: (qb, g)),
                  pl.BlockSpec((1, S, HEAD_DIM), lambda g, qb: (g, 0, 0)),
                  pl.BlockSpec((1, nk, vt_rows, Tk), lambda g, qb: (g, 0, 0, 0))],
        out_specs=pl.BlockSpec((T, rep * HEAD_DIM), lambda g, qb: (qb, g)),
        out_shape=jax.ShapeDtypeStruct((S, D), BF16),
        scratch_shapes=[pltpu.VMEM((rep, SUBLANES, T), F32),
                        pltpu.VMEM((rep, vt_rows, T), F32),
                        pltpu.VMEM((rep, Tk, T), F32), pltpu.VMEM((rep, Tk, T), F32),
                        pltpu.VMEM((rep, SUBLANES, T), F32), pltpu.VMEM((rep, SUBLANES, T), F32)],
        compiler_params=_params("parallel", "arbitrary"), name="attn_d")(q, k, v_t)
    return matmul(o, w_out.astype(BF16), F32)


def _lane_tiles(x):
    return [x[:, c * HEAD_DIM:(c + 1) * HEAD_DIM] for c in range(x.shape[1] // HEAD_DIM)]


def _win_attn_kernel(q_ref, kp_ref, kc_ref, kn_ref, vp_ref, vc_ref, vn_ref, bias_ref, o_ref,
                     *lse_ref, n_heads, tq, lead):
    lse_cols = []

    def scores(h):
        sl = slice(h * HEAD_DIM, (h + 1) * HEAD_DIM)
        q = q_ref[:, sl]
        ks = (kp_ref[:, sl], kc_ref[:, sl], kn_ref[:, sl])
        return [_dot_nt(q, ks[j]) + bias_ref[0, h, :, j * tq:(j + 1) * tq] for j in range(3)]

    def finish(h, ss):
        sl = slice(h * HEAD_DIM, (h + 1) * HEAD_DIM)
        vs = (vp_ref[:, sl], vc_ref[:, sl], vn_ref[:, sl])
        m = jnp.max(functools.reduce(jnp.maximum, [t for s in ss for t in _lane_tiles(s)]),
                    axis=-1, keepdims=True)
        ps = [jnp.exp(s - m) for s in ss]
        l = jnp.sum(functools.reduce(lambda a, b: a + b, [t for p in ps for t in _lane_tiles(p)]),
                    axis=-1, keepdims=True)
        acc = (jnp.dot(ps[0].astype(BF16), vs[0], preferred_element_type=F32)
               + jnp.dot(ps[1].astype(BF16), vs[1], preferred_element_type=F32)
               + jnp.dot(ps[2].astype(BF16), vs[2], preferred_element_type=F32))
        o_ref[:, sl] = (acc / l).astype(o_ref.dtype)
        lse_cols.append(m + jnp.log(l))

    pending = [scores(h) for h in range(min(lead, n_heads))]
    for h in range(n_heads):
        if h + lead < n_heads:
            pending.append(scores(h + lead))
        finish(h, pending.pop(0))
    if lse_ref:
        lse_ref[0][...] = jnp.concatenate(lse_cols, axis=1)


def _edge_variant(i, nblk):
    return jnp.where(i == 0, 0, jnp.where(i == nblk - 1, 2, 1))


def _merge_b_kernel(o0_ref, o1_ref, o2_ref, lse_ref, out_ref, *, n_heads):
    lse = lse_ref[...]
    w = jnp.exp(lse - jnp.max(lse, axis=0, keepdims=True))
    w = w / jnp.sum(w, axis=0, keepdims=True)
    for h in range(n_heads):
        sl = slice(h * HEAD_DIM, (h + 1) * HEAD_DIM)
        out_ref[:, sl] = (w[0, :, h:h + 1] * o0_ref[:, sl] + w[1, :, h:h + 1] * o1_ref[:, sl]
                          + w[2, :, h:h + 1] * o2_ref[:, sl]).astype(out_ref.dtype)


def mixer_b(h, w_in, w_out, rel_bias, tq=128, tm=256):
    S, D = h.shape
    G = len(B_GROUPS)
    n_heads = rel_bias.shape[1]
    width = n_heads * HEAD_DIM
    colscale = jnp.concatenate([jnp.full((G * width,), SCALE, F32), jnp.ones((2 * G * width,), F32)])
    w = (w_in * colscale).astype(BF16).reshape(D, 3, G, width)
    outs, lses = [], []
    for g, (window, dil) in enumerate(B_GROUPS):
        half = window // 2 // dil
        assert half <= tq
        L = S // dil
        nblk = L // tq
        assert nblk >= 2

        def by_residue(a, dil=dil, L=L):
            return a if dil == 1 else a.reshape(L, dil, -1).transpose(1, 0, 2).reshape(S, -1)

        def by_position(a, dil=dil, L=L):
            return a if dil == 1 else a.reshape(dil, L, -1).transpose(1, 0, 2).reshape(S, -1)

        qkv = matmul(by_residue(h), w[:, :, g].reshape(D, 3 * width), BF16)
        bias = t5_tiles(rel_bias, [-tq] * 3, tq, 3 * tq, dil=dil, half=half,
                        col_lo=(tq, 0, 0), col_hi=(3 * tq, 3 * tq, 2 * tq))

        def spec(part, di, nblk=nblk):
            return pl.BlockSpec(
                (tq, width), lambda r, i: (r * nblk + jnp.clip(i + di, 0, nblk - 1), part))

        o_g, lse_g = pl.pallas_call(
            functools.partial(_win_attn_kernel, n_heads=n_heads, tq=tq, lead=4), grid=(dil, nblk),
            in_specs=[spec(0, 0), spec(1, -1), spec(1, 0), spec(1, 1),
                      spec(2, -1), spec(2, 0), spec(2, 1),
                      pl.BlockSpec((1, n_heads, tq, 3 * tq),
                                   lambda r, i, nblk=nblk: (_edge_variant(i, nblk), 0, 0, 0))],
            out_specs=[pl.BlockSpec((tq, width), lambda r, i, nblk=nblk: (r * nblk + i, 0)),
                       pl.BlockSpec((tq, n_heads), lambda r, i, nblk=nblk: (r * nblk + i, 0))],
            out_shape=[jax.ShapeDtypeStruct((S, width), F32),
                       jax.ShapeDtypeStruct((S, n_heads), F32)],
            compiler_params=_params("parallel", "parallel"),
            name=f"attn_b{g}")(qkv, qkv, qkv, qkv, qkv, qkv, qkv, bias)
        outs.append(by_position(o_g))
        lses.append(by_position(lse_g))
    row = pl.BlockSpec((tm, width), lambda i: (i, 0))
    o = pl.pallas_call(
        functools.partial(_merge_b_kernel, n_heads=n_heads), grid=(S // tm,),
        in_specs=[row, row, row, pl.BlockSpec((G, tm, n_heads), lambda i: (0, i, 0))],
        out_specs=row, out_shape=jax.ShapeDtypeStruct((S, width), BF16),
        compiler_params=_params("parallel"), name="merge_b")(*outs, jnp.stack(lses, 0))
    return matmul(o, w_out.astype(BF16), F32)


def _c_bias_kernel(rpb_ref, o_ref, *, rq, n_dr, n_dc):
    h = pl.program_id(0)
    W = GRID_W
    qc = lax.broadcasted_iota(jnp.int32, (W, 2 * W), 0)
    lane = lax.broadcasted_iota(jnp.int32, (W, 2 * W), 1)
    kc = lane & (W - 1)
    dc = kc - qc
    c0 = jnp.clip(qc - C_WIN_C // 2, 0, W - C_WIN_C)
    ok_c = (kc >= c0) & (kc < c0 + C_WIN_C)
    masked = jnp.full((W, 2 * W), MASKED, F32)
    base = h * (n_dr * n_dc)
    sub = []
    for a in range(n_dr):
        val = masked
        for b in range(n_dc):
            val = jnp.where(dc == b - (C_WIN_C - 1), rpb_ref[base + a * n_dc + b], val)
        sub.append(jnp.where(ok_c, val, MASKED))
    for v in range(3):
        for qr in range(rq):
            for jv in range(3 * rq // 2):
                halves = []
                for kr in (2 * jv, 2 * jv + 1):
                    lo, hi = ((rq, 3 * rq - 1), (qr, qr + C_WIN_R - 1), (0, C_WIN_R - 1))[v]
                    halves.append(sub[kr - qr + C_WIN_R - 1 - rq] if lo <= kr <= hi else masked)
                o_ref[v, 0, qr * W:(qr + 1) * W, jv * 2 * W:(jv + 1) * 2 * W] = jnp.where(
                    lane < W, halves[0], halves[1])


def mixer_c(h, w_in, w_out, rpb, heads_per_step=8):
    S, D = h.shape
    n_heads = D // HEAD_DIM
    rows = S // GRID_W
    rq = C_WIN_R // 2
    tq = rq * GRID_W
    nblk = rows // rq
    assert rows >= C_WIN_R and nblk >= 2 and 2 * GRID_W == HEAD_DIM
    n_dr, n_dc = 2 * C_WIN_R - 1, 2 * C_WIN_C - 1
    colscale = jnp.concatenate([jnp.full((D,), SCALE, F32), jnp.ones((2 * D,), F32)])
    qkv = matmul(h, (w_in * colscale).astype(BF16), BF16)
    bias = pl.pallas_call(
        functools.partial(_c_bias_kernel, rq=rq, n_dr=n_dr, n_dc=n_dc), grid=(n_heads,),
        in_specs=[pl.BlockSpec(memory_space=pltpu.SMEM)],
        out_specs=pl.BlockSpec((3, 1, tq, 3 * tq), lambda hh: (0, hh, 0, 0)),
        out_shape=jax.ShapeDtypeStruct((3, n_heads, tq, 3 * tq), F32),
        compiler_params=_params("parallel"), name="c_bias")(rpb.reshape(-1))
    hs = heads_per_step
    n_hg = n_heads // hs
    width = hs * HEAD_DIM

    def spec(part, di):
        return pl.BlockSpec(
            (tq, width), lambda hg, i: (jnp.clip(i + di, 0, nblk - 1), part * n_hg + hg))

    o = pl.pallas_call(
        functools.partial(_win_attn_kernel, n_heads=hs, tq=tq, lead=2), grid=(n_hg, nblk),
        in_specs=[spec(0, 0), spec(1, -1), spec(1, 0), spec(1, 1),
                  spec(2, -1), spec(2, 0), spec(2, 1),
                  pl.BlockSpec((1, hs, tq, 3 * tq),
                               lambda hg, i: (_edge_variant(i, nblk), hg, 0, 0))],
        out_specs=pl.BlockSpec((tq, width), lambda hg, i: (i, hg)),
        out_shape=jax.ShapeDtypeStruct((S, D), BF16),
        compiler_params=_params("parallel", "parallel"),
        name="attn_c")(qkv, qkv, qkv, qkv, qkv, qkv, qkv, bias)
    return matmul(o, w_out.astype(BF16), F32)


def kernel(x, rel_bias, norm_g, a_w_in, a_w_out, a_lambda, a_subln, b_w_in, b_w_out, c_w_in, c_w_out,
           c_rpb, d_w_in, d_w_out, d_qk_norm, ffn_w_gate, ffn_w_up, ffn_w_down):
    B, S, D = x.shape
    assert B == 1
    depth = norm_g.shape[0]
    xs = x.reshape(S, D)
    hn = prenorm(xs, norm_g[0, 0])
    for i in range(depth):
        m, j = i % N_MIXERS, i // N_MIXERS
        if m == 0:
            y = mixer_a(hn, a_w_in[j], a_w_out[j], a_lambda[j], a_subln[j], rel_bias, i)
        elif m == 1:
            y = mixer_b(hn, b_w_in[j], b_w_out[j], rel_bias)
        elif m == 2:
            y = mixer_c(hn, c_w_in[j], c_w_out[j], c_rpb[j])
        else:
            y = mixer_d(hn, d_w_in[j], d_w_out[j], d_qk_norm[j])
        xs, hn = resid(xs, y, norm_g[i, 1], norm_g[i, 2])
        y = ffn(hn, ffn_w_gate[i], ffn_w_up[i], ffn_w_down[i])
        if i + 1 < depth:
            xs, hn = resid(xs, y, norm_g[i, 3], norm_g[i + 1, 0])
        else:
            xs = resid(xs, y, norm_g[i, 3])
    return xs.reshape(B, S, D)
```

```python
import functools
import math

import numpy as np
import jax
import jax.numpy as jnp
from jax import lax
from jax.experimental import pallas as pl
from jax.experimental.pallas import tpu as pltpu

F32 = jnp.float32
BF16 = jnp.bfloat16

HEAD_DIM = 128
SCALE = HEAD_DIM ** -0.5
LOG2E = math.log2(math.e)
GRID_W = 64
EPS = 1e-6
REL_BUCKETS = 32
REL_MAX_DIST = 1024
B_GROUPS = ((128, 1), (512, 4), (2048, 16))
C_WIN_R = 8
C_WIN_C = 16
D_KV_HEADS = 8
ROPE_THETA = 10000.0
N_MIXERS = 4

KEY_CHUNK = 256
SUBLANES = 8
ONES_ROWS = 2 * SUBLANES
MASKED = -1e30
VMEM_LIMIT_BYTES = 56 * 2 ** 20


def _params(*semantics):
    return pltpu.CompilerParams(dimension_semantics=semantics, vmem_limit_bytes=VMEM_LIMIT_BYTES)


def _rms(x, g):
    ms = jnp.mean(x * x, axis=-1, keepdims=True)
    return x * lax.rsqrt(ms + EPS) * g


def _dot_nt(a, b):
    return lax.dot_general(a, b, (((1,), (1,)), ((), ())), preferred_element_type=F32)


def _prenorm_kernel(x_ref, g_ref, h_ref):
    h_ref[...] = _rms(x_ref[...], g_ref[...]).astype(h_ref.dtype)


def prenorm(x, g, tm=256):
    S, D = x.shape
    row = pl.BlockSpec((tm, D), lambda i: (i, 0))
    vec = pl.BlockSpec((1, D), lambda i: (0, 0))
    return pl.pallas_call(
        _prenorm_kernel, grid=(S // tm,), in_specs=[row, vec], out_specs=row,
        out_shape=jax.ShapeDtypeStruct((S, D), BF16), compiler_params=_params("parallel"),
        name="prenorm")(x, g.reshape(1, D))


def _resid_kernel(x_ref, y_ref, g1_ref, g2_ref, xo_ref, h_ref):
    x = x_ref[...] + _rms(y_ref[...], g1_ref[...])
    xo_ref[...] = x
    h_ref[...] = _rms(x, g2_ref[...]).astype(h_ref.dtype)


def _resid_last_kernel(x_ref, y_ref, g1_ref, xo_ref):
    xo_ref[...] = x_ref[...] + _rms(y_ref[...], g1_ref[...])


def resid(x, y, g_post, g_next=None, tm=256):
    S, D = x.shape
    row = pl.BlockSpec((tm, D), lambda i: (i, 0))
    vec = pl.BlockSpec((1, D), lambda i: (0, 0))
    if g_next is None:
        return pl.pallas_call(
            _resid_last_kernel, grid=(S // tm,), in_specs=[row, row, vec], out_specs=row,
            out_shape=jax.ShapeDtypeStruct((S, D), F32), compiler_params=_params("parallel"),
            name="resid_last")(x, y, g_post.reshape(1, D))
    return pl.pallas_call(
        _resid_kernel, grid=(S // tm,), in_specs=[row, row, vec, vec], out_specs=[row, row],
        out_shape=[jax.ShapeDtypeStruct((S, D), F32), jax.ShapeDtypeStruct((S, D), BF16)],
        compiler_params=_params("parallel"),
        name="resid")(x, y, g_post.reshape(1, D), g_next.reshape(1, D))


def _mm_kernel(a_ref, b_ref, o_ref):
    o_ref[...] = jnp.dot(a_ref[...], b_ref[...], preferred_element_type=F32).astype(o_ref.dtype)


def _mm_acc_kernel(a_ref, b_ref, o_ref, acc_ref):
    k = pl.program_id(2)

    @pl.when(k == 0)
    def _():
        acc_ref[...] = jnp.zeros_like(acc_ref)

    acc_ref[...] += jnp.dot(a_ref[...], b_ref[...], preferred_element_type=F32)

    @pl.when(k == pl.num_programs(2) - 1)
    def _():
        o_ref[...] = acc_ref[...].astype(o_ref.dtype)


def matmul(a, b, out_dtype, tm=1024, tn=None, tk=None):
    M, K = a.shape
    N = b.shape[1]
    tk = K if tk is None else tk
    if tn is None:
        tn = 1024 if N % 1024 == 0 else 512
    out_shape = jax.ShapeDtypeStruct((M, N), out_dtype)
    if tk == K:
        return pl.pallas_call(
            _mm_kernel, grid=(M // tm, N // tn),
            in_specs=[pl.BlockSpec((tm, K), lambda i, j: (i, 0)),
                      pl.BlockSpec((K, tn), lambda i, j: (0, j))],
            out_specs=pl.BlockSpec((tm, tn), lambda i, j: (i, j)),
            out_shape=out_shape, compiler_params=_params("parallel", "parallel"),
            name="matmul")(a, b)
    return pl.pallas_call(
        _mm_acc_kernel, grid=(M // tm, N // tn, K // tk),
        in_specs=[pl.BlockSpec((tm, tk), lambda i, j, k: (i, k)),
                  pl.BlockSpec((tk, tn), lambda i, j, k: (k, j))],
        out_specs=pl.BlockSpec((tm, tn), lambda i, j, k: (i, j)),
        out_shape=out_shape, scratch_shapes=[pltpu.VMEM((tm, tn), F32)],
        compiler_params=_params("parallel", "parallel", "arbitrary"),
        name="matmul_acc")(a, b)


def _glu_kernel(a_ref, wg_ref, wu_ref, *rest):
    o_ref = rest[len(rest) // 2]
    a = a_ref[...]
    g = jnp.dot(a, wg_ref[...], preferred_element_type=F32)
    u = jnp.dot(a, wu_ref[...], preferred_element_type=F32)
    o_ref[...] = (g * (1.0 / (1.0 + jnp.exp(-g))) * u).astype(o_ref.dtype)
    n_cast = len(rest) // 2
    for src, dst in zip(rest[:n_cast], rest[n_cast + 1:]):
        dst[...] = src[...].astype(dst.dtype)


def glu_up(a, wg, wu, cast_next=None, tm=2048, tn=256):
    M, K = a.shape
    N = wg.shape[1]
    gi, gj = M // tm, N // tn
    wspec = pl.BlockSpec((K, tn), lambda i, j: (0, j))
    in_specs = [pl.BlockSpec((tm, K), lambda i, j: (i, 0)), wspec, wspec]
    out_specs = [pl.BlockSpec((tm, tn), lambda i, j: (i, j))]
    out_shape = [jax.ShapeDtypeStruct((M, N), BF16)]
    extra = []
    if cast_next is not None:
        layer, ng, nu, nd = cast_next
        D = nd.shape[2]
        up_spec = pl.BlockSpec((None, K // gi, tn), lambda i, j: (layer, i, j))
        down_spec = pl.BlockSpec((None, tn, D // gi), lambda i, j: (layer, j, i))
        in_specs += [up_spec, up_spec, down_spec]
        out_specs += [pl.BlockSpec((K // gi, tn), lambda i, j: (i, j)),
                      pl.BlockSpec((K // gi, tn), lambda i, j: (i, j)),
                      pl.BlockSpec((tn, D // gi), lambda i, j: (j, i))]
        out_shape += [jax.ShapeDtypeStruct((K, N), BF16), jax.ShapeDtypeStruct((K, N), BF16),
                      jax.ShapeDtypeStruct((N, D), BF16)]
        extra = [ng, nu, nd]
    res = pl.pallas_call(
        _glu_kernel, grid=(gi, gj), in_specs=in_specs, out_specs=out_specs, out_shape=out_shape,
        compiler_params=_params("parallel", "parallel"), name="glu_up")(a, wg, wu, *extra)
    return res[0], tuple(res[1:])


def ffn(h, w_gate, w_up, w_down, cast_next=None):
    hid, nxt = glu_up(h, w_gate, w_up, cast_next)
    d_ff = hid.shape[1]
    tk = d_ff // 2 if (d_ff // 2) % HEAD_DIM == 0 else d_ff
    return matmul(hid, w_down, F32, tm=1024, tn=512, tk=tk), nxt


def _t5_thresholds():
    nb = REL_BUCKETS // 2
    max_exact = nb // 2
    n = np.arange(0, 4 * REL_MAX_DIST)
    nf = np.maximum(n, 1).astype(np.float32)
    large = max_exact + (np.log(nf / np.float32(max_exact))
                         / np.float32(math.log(REL_MAX_DIST / max_exact))
                         * np.float32(nb - max_exact)).astype(np.int32)
    bucket = np.where(n < max_exact, n, np.minimum(large, nb - 1))
    assert np.all(np.diff(bucket) >= 0) and bucket[-1] == nb - 1
    return tuple(int(np.argmax(bucket >= b)) for b in range(1, nb))


T5_THRESHOLDS = _t5_thresholds()


def _t5_tile_kernel(rb_ref, var_ref, o_ref, *, dil, half, n_heads, mult, keys_on_rows):
    h = pl.program_id(0)
    v = pl.program_id(1)
    off, col_lo, col_hi = var_ref[3 * v], var_ref[3 * v + 1], var_ref[3 * v + 2]
    rows, cols = o_ref.shape[2], o_ref.shape[3]
    i = lax.broadcasted_iota(jnp.int32, (rows, cols), 0)
    j = lax.broadcasted_iota(jnp.int32, (rows, cols), 1)
    nb = REL_BUCKETS // 2
    delta = (i - j if keys_on_rows else j - i) + off
    rel = delta * dil
    n = jnp.abs(rel)
    vneg = jnp.full((rows, cols), rb_ref[h], F32)
    vpos = jnp.full((rows, cols), rb_ref[nb * n_heads + h], F32)
    for b, t in enumerate(T5_THRESHOLDS, start=1):
        c = n >= t
        vneg = jnp.where(c, rb_ref[b * n_heads + h], vneg)
        vpos = jnp.where(c, rb_ref[(nb + b) * n_heads + h], vpos)
    val = jnp.where(rel > 0, vpos, vneg)
    if mult != 1.0:
        val = val * mult
    if half is not None:
        ok = (jnp.abs(delta) <= half) & (j >= col_lo) & (j < col_hi)
        val = jnp.where(ok, val, MASKED)
    o_ref[0, 0] = val


def t5_tiles(rel_bias, offs, rows, cols, dil=1, half=None, col_lo=None, col_hi=None, mult=1.0,
             keys_on_rows=False):
    n_heads = rel_bias.shape[1]
    nv = len(offs)
    col_lo = (0,) * nv if col_lo is None else col_lo
    col_hi = (cols,) * nv if col_hi is None else col_hi
    variants = jnp.asarray(np.stack([offs, col_lo, col_hi], axis=1).reshape(-1), jnp.int32)
    kern = functools.partial(_t5_tile_kernel, dil=dil, half=half, n_heads=n_heads, mult=mult,
                             keys_on_rows=keys_on_rows)
    smem = pl.BlockSpec(memory_space=pltpu.SMEM)
    return pl.pallas_call(
        kern, grid=(n_heads, nv), in_specs=[smem, smem],
        out_specs=pl.BlockSpec((1, 1, rows, cols), lambda h, v: (v, h, 0, 0)),
        out_shape=jax.ShapeDtypeStruct((nv, n_heads, rows, cols), F32),
        compiler_params=_params("parallel", "parallel"), name="t5_tiles")(rel_bias.reshape(-1), variants)


def _sublane_groups(x):
    return x.reshape(x.shape[0] // SUBLANES, SUBLANES, x.shape[1])


def _store_scores(s_sc, mp_sc, idx, s):
    s_sc[idx] = s
    mp_sc[idx] = jnp.max(_sublane_groups(s), axis=0)


def _online_step(m_sc, acc_sc, idx, s_sc, mp_sc, vt_chunk):
    Tk, Tq = s_sc.shape[1:]
    m_old = m_sc[idx]
    m_cur = jnp.max(mp_sc[idx], axis=0, keepdims=True)
    m_new = jnp.maximum(m_old, jnp.broadcast_to(m_cur, m_old.shape))
    m_sc[idx] = m_new
    alpha = jnp.exp2(m_old - m_new)
    pv = None
    for c in range(Tk // KEY_CHUNK):
        ks = slice(c * KEY_CHUNK, (c + 1) * KEY_CHUNK)
        p = jnp.exp2(_sublane_groups(s_sc[idx, ks, :]) - m_new[None])
        d = jnp.dot(vt_chunk(ks), p.reshape(KEY_CHUNK, Tq).astype(BF16), preferred_element_type=F32)
        pv = d if pv is None else pv + d
    acc_sc[idx] = (_sublane_groups(acc_sc[idx]) * alpha[None]).reshape(acc_sc.shape[1:]) + pv


def _flash_init(m_sc, acc_sc):
    m_sc[...] = jnp.full(m_sc.shape, -jnp.inf, F32)
    acc_sc[...] = jnp.zeros(acc_sc.shape, F32)


def _values_transposed(v, n_heads, Tk):
    S = v.shape[0]
    dv = v.shape[1] // n_heads
    v_t = v.reshape(S // Tk, Tk, n_heads, dv).transpose(2, 0, 3, 1)
    return jnp.concatenate([v_t, jnp.ones((n_heads, S // Tk, ONES_ROWS, Tk), v.dtype)], axis=2)


def _pipelined_blocks(nk, parts, scores, consume, buf0, buf1):
    assert nk % 2 == 0
    for part in parts:
        scores(0, *buf0, part)

    def step(kb, cur, nxt):
        for part in parts:
            scores(kb + 1, *nxt, part)
            consume(kb, *cur, part)

    def body(j, carry):
        step(2 * j, buf0, buf1)
        step(2 * j + 1, buf1, buf0)
        return carry

    lax.fori_loop(0, nk // 2 - 1, body, 0)
    step(nk - 2, buf0, buf1)
    for part in parts:
        consume(nk - 1, *buf1, part)


def _attn_a_kernel(q_ref, k_ref, vt_ref, band_ref, lam_ref, g_ref, o_ref, m_sc, acc_sc,
                   s0_sc, s1_sc, mp0_sc, mp1_sc, *, Tk, nk, q_step, k_step, e_lo, e_hi, lam_init):
    qb = pl.program_id(1)
    dv = 2 * HEAD_DIM
    _flash_init(m_sc, acc_sc)

    def scores(kb, s_sc, mp_sc, part):
        off = pl.multiple_of(kb * Tk, Tk)
        bidx = jnp.clip(kb * k_step - qb * q_step, e_lo, e_hi) - e_lo
        for m in part:
            sl = slice(m * HEAD_DIM, (m + 1) * HEAD_DIM)
            _store_scores(s_sc, mp_sc, m,
                          _dot_nt(k_ref[0, pl.ds(off, Tk), sl], q_ref[:, sl]) + band_ref[bidx, 0])

    def consume(kb, s_sc, mp_sc, part):
        for m in part:
            _online_step(m_sc, acc_sc, m, s_sc, mp_sc, lambda ks: vt_ref[0, kb, :, ks])

    _pipelined_blocks(nk, ((0, 1),), scores, consume, (s0_sc, mp0_sc), (s1_sc, mp1_sc))
    lp = lam_ref[...]
    lam = (jnp.exp(jnp.sum(lp[0:1] * lp[1:2], axis=-1, keepdims=True))
           - jnp.exp(jnp.sum(lp[2:3] * lp[3:4], axis=-1, keepdims=True)) + lam_init)
    acc0, acc1 = acc_sc[0], acc_sc[1]
    o_t = acc0[:dv] / acc0[dv:dv + 1] - lam * (acc1[:dv] / acc1[dv:dv + 1])
    o_ref[...] = (_rms(o_t.T, g_ref[...]) * (1.0 - lam_init)).astype(o_ref.dtype)


def mixer_a(h, w_in, w_out, lam_p, subln_g, rel_bias, layer_idx, T=512, Tk=1024):
    S, D = h.shape
    n_heads = D // (2 * HEAD_DIM)
    width = 2 * HEAD_DIM
    colscale = jnp.concatenate([jnp.full((D,), SCALE * LOG2E, F32), jnp.ones((2 * D,), F32)])
    qkv = matmul(h, (w_in * colscale).astype(BF16), BF16)
    far = T5_THRESHOLDS[-1]
    unit = math.gcd(T, Tk)
    e_hi = -(-(far + T - 1) // unit)
    e_lo = -(-(far + Tk - 1) // unit)
    e_lo = -e_lo
    tiles = t5_tiles(rel_bias, [e * unit for e in range(e_lo, e_hi + 1)], Tk, T, mult=LOG2E,
                     keys_on_rows=True)
    n_tiles = e_hi - e_lo + 1
    nk = S // Tk
    v_t = _values_transposed(qkv[:, 2 * D:], n_heads, Tk)
    k_hm = qkv[:, D:2 * D].reshape(S, n_heads, width).transpose(1, 0, 2)
    lam_init = 0.8 - 0.6 * math.exp(-0.3 * layer_idx)
    kern = functools.partial(_attn_a_kernel, Tk=Tk, nk=nk, q_step=T // unit, k_step=Tk // unit,
                             e_lo=e_lo, e_hi=e_hi, lam_init=lam_init)
    once = pl.Buffered(1)
    o = pl.pallas_call(
        kern, grid=(n_heads, S // T),
        in_specs=[pl.BlockSpec((T, width), lambda hh, qb: (qb, hh)),
                  pl.BlockSpec((1, S, width), lambda hh, qb: (hh, 0, 0), pipeline_mode=once),
                  pl.BlockSpec((1, nk, width + ONES_ROWS, Tk), lambda hh, qb: (hh, 0, 0, 0),
                               pipeline_mode=once),
                  pl.BlockSpec((n_tiles, 1, Tk, T), lambda hh, qb: (0, hh, 0, 0), pipeline_mode=once),
                  pl.BlockSpec((4, HEAD_DIM), lambda hh, qb: (0, 0)),
                  pl.BlockSpec((1, width), lambda hh, qb: (0, 0))],
        out_specs=pl.BlockSpec((T, width), lambda hh, qb: (qb, hh)),
        out_shape=jax.ShapeDtypeStruct((S, D), BF16),
        scratch_shapes=[pltpu.VMEM((2, SUBLANES, T), F32), pltpu.VMEM((2, width + ONES_ROWS, T), F32),
                        pltpu.VMEM((2, Tk, T), F32), pltpu.VMEM((2, Tk, T), F32),
                        pltpu.VMEM((2, SUBLANES, T), F32), pltpu.VMEM((2, SUBLANES, T), F32)],
        compiler_params=_params("parallel", "arbitrary"),
        name="attn_a")(qkv, k_hm, v_t, tiles, lam_p, subln_g.reshape(1, width))
    return matmul(o, w_out.astype(BF16), F32)


def _rope_kernel(x_ref, c_ref, sa_ref, sb_ref, g_ref, oq_ref, ok_ref, *, n_q, n_chunks):
    c, sa, sb = c_ref[...], sa_ref[...], sb_ref[...]
    for ch in range(n_chunks):
        sl = slice(ch * HEAD_DIM, (ch + 1) * HEAD_DIM)
        y = _rms(x_ref[:, sl], g_ref[0:1] if ch < n_q else g_ref[1:2])
        y = y * c + pltpu.roll(y, HEAD_DIM - 1, 1) * sa + pltpu.roll(y, 1, 1) * sb
        if ch < n_q:
            oq_ref[:, sl] = (y * (SCALE * LOG2E)).astype(oq_ref.dtype)
        else:
            ok_ref[ch - n_q] = y.astype(ok_ref.dtype)


def _rope_tables(S):
    pos = jnp.arange(S)
    n_freq = HEAD_DIM // 4
    freqs = ROPE_THETA ** (-jnp.arange(n_freq, dtype=F32) / n_freq)
    ang = jnp.concatenate([(pos // GRID_W).astype(F32)[:, None] * freqs,
                           (pos % GRID_W).astype(F32)[:, None] * freqs], axis=-1)
    cos, sin = jnp.cos(ang), jnp.sin(ang)
    zero = jnp.zeros_like(sin)
    c = jnp.stack([cos, cos], axis=-1).reshape(S, HEAD_DIM)
    sa = jnp.stack([-sin, zero], axis=-1).reshape(S, HEAD_DIM)
    sb = jnp.stack([zero, sin], axis=-1).reshape(S, HEAD_DIM)
    return c, sa, sb


def _attn_d_kernel(q_ref, k_ref, vt_ref, o_ref, m_sc, acc_sc, s0_sc, s1_sc, mp0_sc, mp1_sc,
                   *, Tk, nk, rep):
    _flash_init(m_sc, acc_sc)

    def scores(kb, s_sc, mp_sc, part):
        kblk = k_ref[0, pl.ds(pl.multiple_of(kb * Tk, Tk), Tk), :]
        for r in part:
            _store_scores(s_sc, mp_sc, r, _dot_nt(kblk, q_ref[:, r * HEAD_DIM:(r + 1) * HEAD_DIM]))

    def consume(kb, s_sc, mp_sc, part):
        for r in part:
            _online_step(m_sc, acc_sc, r, s_sc, mp_sc, lambda ks: vt_ref[0, kb, :, ks])

    parts = tuple(tuple(range(r, min(r + 2, rep))) for r in range(0, rep, 2))
    _pipelined_blocks(nk, parts, scores, consume, (s0_sc, mp0_sc), (s1_sc, mp1_sc))
    for r in range(rep):
        acc = acc_sc[r]
        o_t = acc[:HEAD_DIM] / acc[HEAD_DIM:HEAD_DIM + 1]
        o_ref[:, r * HEAD_DIM:(r + 1) * HEAD_DIM] = o_t.T.astype(o_ref.dtype)


def mixer_d(h, w_in, w_out, qk_g, T=512, Tk=1024, tm=256):
    S, D = h.shape
    n_q = D // HEAD_DIM
    n_kv = D_KV_HEADS
    rep = n_q // n_kv
    nqk = (n_q + n_kv) * HEAD_DIM
    qk = matmul(h, w_in[:, :nqk].astype(BF16), F32)
    v = matmul(h, w_in[:, nqk:].astype(BF16), BF16)
    c, sa, sb = _rope_tables(S)
    tab = pl.BlockSpec((tm, HEAD_DIM), lambda i: (i, 0))
    q, k = pl.pallas_call(
        functools.partial(_rope_kernel, n_q=n_q, n_chunks=n_q + n_kv), grid=(S // tm,),
        in_specs=[pl.BlockSpec((tm, nqk), lambda i: (i, 0)), tab, tab, tab,
                  pl.BlockSpec((2, HEAD_DIM), lambda i: (0, 0))],
        out_specs=[pl.BlockSpec((tm, D), lambda i: (i, 0)),
                   pl.BlockSpec((n_kv, tm, HEAD_DIM), lambda i: (0, i, 0))],
        out_shape=[jax.ShapeDtypeStruct((S, D), BF16),
                   jax.ShapeDtypeStruct((n_kv, S, HEAD_DIM), BF16)],
        compiler_params=_params("parallel"), name="qk_norm_rope")(qk, c, sa, sb, qk_g)
    nk = S // Tk
    v_t = _values_transposed(v, n_kv, Tk)
    vt_rows = HEAD_DIM + ONES_ROWS
    kern = functools.partial(_attn_d_kernel, Tk=Tk, nk=nk, rep=rep)
    o = pl.pallas_call(
        kern, grid=(n_kv, S // T),
        in_specs=[pl.BlockSpec((T, rep * HEAD_DIM), lambda g, qb: (qb, g)),
                  pl.BlockSpec((1, S, HEAD_DIM), lambda g, qb: (g, 0, 0)),
                  pl.BlockSpec((1, nk, vt_rows, Tk), lambda g, qb: (g, 0, 0, 0))],
        out_specs=pl.BlockSpec((T, rep * HEAD_DIM), lambda g, qb: (qb, g)),
        out_shape=jax.ShapeDtypeStruct((S, D), BF16),
        scratch_shapes=[pltpu.VMEM((rep, SUBLANES, T), F32),
                        pltpu.VMEM((rep, vt_rows, T), F32),
                        pltpu.VMEM((rep, Tk, T), F32), pltpu.VMEM((rep, Tk, T), F32),
                        pltpu.VMEM((rep, SUBLANES, T), F32), pltpu.VMEM((rep, SUBLANES, T), F32)],
        compiler_params=_params("parallel", "arbitrary"), name="attn_d")(q, k, v_t)
    return matmul(o, w_out.astype(BF16), F32)


def _lane_tiles(x):
    return [x[:, c * HEAD_DIM:(c + 1) * HEAD_DIM] for c in range(x.shape[1] // HEAD_DIM)]


def _win_attn_kernel(q_ref, kp_ref, kc_ref, kn_ref, vp_ref, vc_ref, vn_ref, bias_ref, o_ref,
                     *lse_ref, n_heads, tq, lead):
    lse_cols = []

    def scores(h):
        sl = slice(h * HEAD_DIM, (h + 1) * HEAD_DIM)
        q = q_ref[:, sl]
        ks = (kp_ref[:, sl], kc_ref[:, sl], kn_ref[:, sl])
        return [_dot_nt(q, ks[j]) + bias_ref[0, h, :, j * tq:(j + 1) * tq] for j in range(3)]

    def finish(h, ss):
        sl = slice(h * HEAD_DIM, (h + 1) * HEAD_DIM)
        vs = (vp_ref[:, sl], vc_ref[:, sl], vn_ref[:, sl])
        m = jnp.max(functools.reduce(jnp.maximum, [t for s in ss for t in _lane_tiles(s)]),
                    axis=-1, keepdims=True)
        ps = [jnp.exp(s - m) for s in ss]
        l = jnp.sum(functools.reduce(lambda a, b: a + b, [t for p in ps for t in _lane_tiles(p)]),
                    axis=-1, keepdims=True)
        acc = (jnp.dot(ps[0].astype(BF16), vs[0], preferred_element_type=F32)
               + jnp.dot(ps[1].astype(BF16), vs[1], preferred_element_type=F32)
               + jnp.dot(ps[2].astype(BF16), vs[2], preferred_element_type=F32))
        o_ref[:, sl] = (acc / l).astype(o_ref.dtype)
        lse_cols.append(m + jnp.log(l))

    pending = [scores(h) for h in range(min(lead, n_heads))]
    for h in range(n_heads):
        if h + lead < n_heads:
            pending.append(scores(h + lead))
        finish(h, pending.pop(0))
    if lse_ref:
        lse_ref[0][...] = jnp.concatenate(lse_cols, axis=1)


def _edge_variant(i, nblk):
    return jnp.where(i == 0, 0, jnp.where(i == nblk - 1, 2, 1))


def _merge_b_kernel(o0_ref, o1_ref, o2_ref, lse_ref, out_ref, *, n_heads):
    lse = lse_ref[...]
    w = jnp.exp(lse - jnp.max(lse, axis=0, keepdims=True))
    w = w / jnp.sum(w, axis=0, keepdims=True)
    for h in range(n_heads):
        sl = slice(h * HEAD_DIM, (h + 1) * HEAD_DIM)
        out_ref[:, sl] = (w[0, :, h:h + 1] * o0_ref[:, sl] + w[1, :, h:h + 1] * o1_ref[:, sl]
                          + w[2, :, h:h + 1] * o2_ref[:, sl]).astype(out_ref.dtype)


def mixer_b(h, w_in, w_out, rel_bias, tq=128, tm=256):
    S, D = h.shape
    G = len(B_GROUPS)
    n_heads = rel_bias.shape[1]
    width = n_heads * HEAD_DIM
    colscale = jnp.concatenate([jnp.full((G * width,), SCALE, F32), jnp.ones((2 * G * width,), F32)])
    w = (w_in * colscale).astype(BF16).reshape(D, 3, G, width)
    outs, lses = [], []
    for g, (window, dil) in enumerate(B_GROUPS):
        half = window // 2 // dil
        assert half <= tq
        L = S // dil
        nblk = L // tq
        assert nblk >= 2

        def by_residue(a, dil=dil, L=L):
            return a if dil == 1 else a.reshape(L, dil, -1).transpose(1, 0, 2).reshape(S, -1)

        def by_position(a, dil=dil, L=L):
            return a if dil == 1 else a.reshape(dil, L, -1).transpose(1, 0, 2).reshape(S, -1)

        qkv = matmul(by_residue(h), w[:, :, g].reshape(D, 3 * width), BF16)
        bias = t5_tiles(rel_bias, [-tq] * 3, tq, 3 * tq, dil=dil, half=half,
                        col_lo=(tq, 0, 0), col_hi=(3 * tq, 3 * tq, 2 * tq))

        def spec(part, di, nblk=nblk):
            return pl.BlockSpec(
                (tq, width), lambda r, i: (r * nblk + jnp.clip(i + di, 0, nblk - 1), part))

        o_g, lse_g = pl.pallas_call(
            functools.partial(_win_attn_kernel, n_heads=n_heads, tq=tq, lead=4), grid=(dil, nblk),
            in_specs=[spec(0, 0), spec(1, -1), spec(1, 0), spec(1, 1),
                      spec(2, -1), spec(2, 0), spec(2, 1),
                      pl.BlockSpec((1, n_heads, tq, 3 * tq),
                                   lambda r, i, nblk=nblk: (_edge_variant(i, nblk), 0, 0, 0))],
            out_specs=[pl.BlockSpec((tq, width), lambda r, i, nblk=nblk: (r * nblk + i, 0)),
                       pl.BlockSpec((tq, n_heads), lambda r, i, nblk=nblk: (r * nblk + i, 0))],
            out_shape=[jax.ShapeDtypeStruct((S, width), F32),
                       jax.ShapeDtypeStruct((S, n_heads), F32)],
            compiler_params=_params("parallel", "parallel"),
            name=f"attn_b{g}")(qkv, qkv, qkv, qkv, qkv, qkv, qkv, bias)
        outs.append(by_position(o_g))
        lses.append(by_position(lse_g))
    row = pl.BlockSpec((tm, width), lambda i: (i, 0))
    o = pl.pallas_call(
        functools.partial(_merge_b_kernel, n_heads=n_heads), grid=(S // tm,),
        in_specs=[row, row, row, pl.BlockSpec((G, tm, n_heads), lambda i: (0, i, 0))],
        out_specs=row, out_shape=jax.ShapeDtypeStruct((S, width), BF16),
        compiler_params=_params("parallel"), name="merge_b")(*outs, jnp.stack(lses, 0))
    return matmul(o, w_out.astype(BF16), F32)


def _c_bias_kernel(rpb_ref, o_ref, *, rq, n_dr, n_dc):
    h = pl.program_id(0)
    W = GRID_W
    qc = lax.broadcasted_iota(jnp.int32, (W, 2 * W), 0)
    lane = lax.broadcasted_iota(jnp.int32, (W, 2 * W), 1)
    kc = lane & (W - 1)
    dc = kc - qc
    c0 = jnp.clip(qc - C_WIN_C // 2, 0, W - C_WIN_C)
    ok_c = (kc >= c0) & (kc < c0 + C_WIN_C)
    masked = jnp.full((W, 2 * W), MASKED, F32)
    base = h * (n_dr * n_dc)
    sub = []
    for a in range(n_dr):
        val = masked
        for b in range(n_dc):
            val = jnp.where(dc == b - (C_WIN_C - 1), rpb_ref[base + a * n_dc + b], val)
        sub.append(jnp.where(ok_c, val, MASKED))
    for v in range(3):
        for qr in range(rq):
            for jv in range(3 * rq // 2):
                halves = []
                for kr in (2 * jv, 2 * jv + 1):
                    lo, hi = ((rq, 3 * rq - 1), (qr, qr + C_WIN_R - 1), (0, C_WIN_R - 1))[v]
                    halves.append(sub[kr - qr + C_WIN_R - 1 - rq] if lo <= kr <= hi else masked)
                o_ref[v, 0, qr * W:(qr + 1) * W, jv * 2 * W:(jv + 1) * 2 * W] = jnp.where(
                    lane < W, halves[0], halves[1])


def mixer_c(h, w_in, w_out, rpb, heads_per_step=8):
    S, D = h.shape
    n_heads = D // HEAD_DIM
    rows = S // GRID_W
    rq = C_WIN_R // 2
    tq = rq * GRID_W
    nblk = rows // rq
    assert rows >= C_WIN_R and nblk >= 2 and 2 * GRID_W == HEAD_DIM
    n_dr, n_dc = 2 * C_WIN_R - 1, 2 * C_WIN_C - 1
    colscale = jnp.concatenate([jnp.full((D,), SCALE, F32), jnp.ones((2 * D,), F32)])
    qkv = matmul(h, (w_in * colscale).astype(BF16), BF16)
    bias = pl.pallas_call(
        functools.partial(_c_bias_kernel, rq=rq, n_dr=n_dr, n_dc=n_dc), grid=(n_heads,),
        in_specs=[pl.BlockSpec(memory_space=pltpu.SMEM)],
        out_specs=pl.BlockSpec((3, 1, tq, 3 * tq), lambda hh: (0, hh, 0, 0)),
        out_shape=jax.ShapeDtypeStruct((3, n_heads, tq, 3 * tq), F32),
        compiler_params=_params("parallel"), name="c_bias")(rpb.reshape(-1))
    hs = heads_per_step
    n_hg = n_heads // hs
    width = hs * HEAD_DIM

    def spec(part, di):
        return pl.BlockSpec(
            (tq, width), lambda hg, i: (jnp.clip(i + di, 0, nblk - 1), part * n_hg + hg))

    o = pl.pallas_call(
        functools.partial(_win_attn_kernel, n_heads=hs, tq=tq, lead=2), grid=(n_hg, nblk),
        in_specs=[spec(0, 0), spec(1, -1), spec(1, 0), spec(1, 1),
                  spec(2, -1), spec(2, 0), spec(2, 1),
                  pl.BlockSpec((1, hs, tq, 3 * tq),
                               lambda hg, i: (_edge_variant(i, nblk), hg, 0, 0))],
        out_specs=pl.BlockSpec((tq, width), lambda hg, i: (i, hg)),
        out_shape=jax.ShapeDtypeStruct((S, D), BF16),
        compiler_params=_params("parallel", "parallel"),
        name="attn_c")(qkv, qkv, qkv, qkv, qkv, qkv, qkv, bias)
    return matmul(o, w_out.astype(BF16), F32)


def kernel(x, rel_bias, norm_g, a_w_in, a_w_out, a_lambda, a_subln, b_w_in, b_w_out, c_w_in, c_w_out,
           c_rpb, d_w_in, d_w_out, d_qk_norm, ffn_w_gate, ffn_w_up, ffn_w_down):
    B, S, D = x.shape
    assert B == 1
    depth = norm_g.shape[0]
    xs = x.reshape(S, D)
    hn = prenorm(xs, norm_g[0, 0])
    ffn_w = (ffn_w_gate[0].astype(BF16), ffn_w_up[0].astype(BF16), ffn_w_down[0].astype(BF16))
    for i in range(depth):
        m, j = i % N_MIXERS, i // N_MIXERS
        if m == 0:
            y = mixer_a(hn, a_w_in[j], a_w_out[j], a_lambda[j], a_subln[j], rel_bias, i)
        elif m == 1:
            y = mixer_b(hn, b_w_in[j], b_w_out[j], rel_bias)
        elif m == 2:
            y = mixer_c(hn, c_w_in[j], c_w_out[j], c_rpb[j])
        else:
            y = mixer_d(hn, d_w_in[j], d_w_out[j], d_qk_norm[j])
        xs, hn = resid(xs, y, norm_g[i, 1], norm_g[i, 2])
        cast_next = (i + 1, ffn_w_gate, ffn_w_up, ffn_w_down) if i + 1 < depth else None
        y, ffn_w = ffn(hn, *ffn_w, cast_next)
        if i + 1 < depth:
            xs, hn = resid(xs, y, norm_g[i, 3], norm_g[i + 1, 0])
        else:
            xs = resid(xs, y, norm_g[i, 3])
    return xs.reshape(B, S, D)
```

```python
import functools
import math

import numpy as np
import jax
import jax.numpy as jnp
from jax import lax
from jax.experimental import pallas as pl
from jax.experimental.pallas import tpu as pltpu

F32 = jnp.float32
BF16 = jnp.bfloat16

HEAD_DIM = 128
SCALE = HEAD_DIM ** -0.5
LOG2E = math.log2(math.e)
GRID_W = 64
EPS = 1e-6
REL_BUCKETS = 32
REL_MAX_DIST = 1024
B_GROUPS = ((128, 1), (512, 4), (2048, 16))
C_WIN_R = 8
C_WIN_C = 16
D_KV_HEADS = 8
ROPE_THETA = 10000.0
N_MIXERS = 4

KEY_CHUNK = 256
SUBLANES = 8
ONES_ROWS = 2 * SUBLANES
MASKED = -1e30
VMEM_LIMIT_BYTES = 56 * 2 ** 20


def _params(*semantics):
    return pltpu.CompilerParams(dimension_semantics=semantics, vmem_limit_bytes=VMEM_LIMIT_BYTES)


def _rms(x, g):
    ms = jnp.mean(x * x, axis=-1, keepdims=True)
    return x * lax.rsqrt(ms + EPS) * g


def _dot_nt(a, b):
    return lax.dot_general(a, b, (((1,), (1,)), ((), ())), preferred_element_type=F32)


def _prenorm_kernel(x_ref, g_ref, h_ref):
    h_ref[...] = _rms(x_ref[...], g_ref[...]).astype(h_ref.dtype)


def prenorm(x, g, tm=256):
    S, D = x.shape
    row = pl.BlockSpec((tm, D), lambda i: (i, 0))
    vec = pl.BlockSpec((1, D), lambda i: (0, 0))
    return pl.pallas_call(
        _prenorm_kernel, grid=(S // tm,), in_specs=[row, vec], out_specs=row,
        out_shape=jax.ShapeDtypeStruct((S, D), BF16), compiler_params=_params("parallel"),
        name="prenorm")(x, g.reshape(1, D))


def _resid_kernel(x_ref, y_ref, g1_ref, g2_ref, xo_ref, h_ref):
    x = x_ref[...] + _rms(y_ref[...], g1_ref[...])
    xo_ref[...] = x
    h_ref[...] = _rms(x, g2_ref[...]).astype(h_ref.dtype)


def _resid_last_kernel(x_ref, y_ref, g1_ref, xo_ref):
    xo_ref[...] = x_ref[...] + _rms(y_ref[...], g1_ref[...])


def resid(x, y, g_post, g_next=None, tm=256):
    S, D = x.shape
    row = pl.BlockSpec((tm, D), lambda i: (i, 0))
    vec = pl.BlockSpec((1, D), lambda i: (0, 0))
    if g_next is None:
        return pl.pallas_call(
            _resid_last_kernel, grid=(S // tm,), in_specs=[row, row, vec], out_specs=row,
            out_shape=jax.ShapeDtypeStruct((S, D), F32), compiler_params=_params("parallel"),
            name="resid_last")(x, y, g_post.reshape(1, D))
    return pl.pallas_call(
        _resid_kernel, grid=(S // tm,), in_specs=[row, row, vec, vec], out_specs=[row, row],
        out_shape=[jax.ShapeDtypeStruct((S, D), F32), jax.ShapeDtypeStruct((S, D), BF16)],
        compiler_params=_params("parallel"),
        name="resid")(x, y, g_post.reshape(1, D), g_next.reshape(1, D))


def _mm_kernel(a_ref, b_ref, o_ref):
    o_ref[...] = jnp.dot(a_ref[...], b_ref[...], preferred_element_type=F32).astype(o_ref.dtype)


def _mm_acc_kernel(a_ref, b_ref, *rest, scaled):
    n_in = len(scaled) + sum(scaled)
    o_ref, acc_ref = rest[n_in], rest[-1]
    k = pl.program_id(2)

    @pl.when(k == 0)
    def _():
        acc_ref[...] = jnp.zeros_like(acc_ref)

    acc_ref[...] += jnp.dot(a_ref[...], b_ref[...], preferred_element_type=F32)

    @pl.when(k == pl.num_programs(2) - 1)
    def _():
        o_ref[...] = acc_ref[...].astype(o_ref.dtype)

    srcs = list(rest[:n_in])
    for has_scale, dst in zip(scaled, rest[n_in + 1:-1]):
        w = srcs.pop(0)[...]
        if has_scale:
            w = w * srcs.pop(0)[...]
        dst[...] = w.astype(dst.dtype)


def matmul(a, b, out_dtype, tm=1024, tn=None, tk=None, cols=None, riders=()):
    M, K = a.shape
    tk = K if tk is None else tk
    if tn is None:
        tn = 1024 if b.shape[1] % 1024 == 0 else 512
    n_tiles, col = (b.shape[1] // tn, lambda j: j) if cols is None else cols
    out_shape = jax.ShapeDtypeStruct((M, n_tiles * tn), out_dtype)
    if tk == K:
        assert not riders
        return pl.pallas_call(
            _mm_kernel, grid=(M // tm, n_tiles),
            in_specs=[pl.BlockSpec((tm, K), lambda i, j: (i, 0)),
                      pl.BlockSpec((K, tn), lambda i, j: (0, col(j)))],
            out_specs=pl.BlockSpec((tm, tn), lambda i, j: (i, j)),
            out_shape=out_shape, compiler_params=_params("parallel", "parallel"),
            name="matmul")(a, b)
    gi, gj, gk = M // tm, n_tiles, K // tk
    in_specs = [pl.BlockSpec((tm, tk), lambda i, j, k: (i, k)),
                pl.BlockSpec((tk, tn), lambda i, j, k: (k, col(j)))]
    out_specs = [pl.BlockSpec((tm, tn), lambda i, j, k: (i, j))]
    out_shapes = [out_shape]
    operands = [a, b]
    for w, layer, scale in riders:
        R, C = w.shape[1:]
        rb, cb = R // gi, C // (gj * gk)
        assert rb * gi == R and cb * gj * gk == C and rb % SUBLANES == 0 and cb % HEAD_DIM == 0
        in_specs.append(pl.BlockSpec((None, rb, cb), lambda i, j, k, layer=layer: (layer, i, j * gk + k)))
        operands.append(w)
        if scale is not None:
            in_specs.append(pl.BlockSpec((1, cb), lambda i, j, k: (0, j * gk + k)))
            operands.append(scale.reshape(1, C))
        out_specs.append(pl.BlockSpec((rb, cb), lambda i, j, k: (i, j * gk + k)))
        out_shapes.append(jax.ShapeDtypeStruct((R, C), BF16))
    res = pl.pallas_call(
        functools.partial(_mm_acc_kernel, scaled=tuple(s is not None for _, _, s in riders)),
        grid=(gi, gj, gk), in_specs=in_specs, out_specs=out_specs, out_shape=out_shapes,
        scratch_shapes=[pltpu.VMEM((tm, tn), F32)],
        compiler_params=_params("parallel", "parallel", "arbitrary"),
        name="matmul_acc")(*operands)
    return (res[0], tuple(res[1:])) if riders else res[0]


def _glu_kernel(a_ref, wg_ref, wu_ref, *rest):
    o_ref = rest[len(rest) // 2]
    a = a_ref[...]
    g = jnp.dot(a, wg_ref[...], preferred_element_type=F32)
    u = jnp.dot(a, wu_ref[...], preferred_element_type=F32)
    o_ref[...] = (g * (1.0 / (1.0 + jnp.exp(-g))) * u).astype(o_ref.dtype)
    n_cast = len(rest) // 2
    for src, dst in zip(rest[:n_cast], rest[n_cast + 1:]):
        dst[...] = src[...].astype(dst.dtype)


def glu_up(a, wg, wu, cast_next=None, tm=2048, tn=256):
    M, K = a.shape
    N = wg.shape[1]
    gi, gj = M // tm, N // tn
    wspec = pl.BlockSpec((K, tn), lambda i, j: (0, j))
    in_specs = [pl.BlockSpec((tm, K), lambda i, j: (i, 0)), wspec, wspec]
    out_specs = [pl.BlockSpec((tm, tn), lambda i, j: (i, j))]
    out_shape = [jax.ShapeDtypeStruct((M, N), BF16)]
    extra = []
    if cast_next is not None:
        layer, ng, nu, nd = cast_next
        D = nd.shape[2]
        up_spec = pl.BlockSpec((None, K // gi, tn), lambda i, j: (layer, i, j))
        down_spec = pl.BlockSpec((None, tn, D // gi), lambda i, j: (layer, j, i))
        in_specs += [up_spec, up_spec, down_spec]
        out_specs += [pl.BlockSpec((K // gi, tn), lambda i, j: (i, j)),
                      pl.BlockSpec((K // gi, tn), lambda i, j: (i, j)),
                      pl.BlockSpec((tn, D // gi), lambda i, j: (j, i))]
        out_shape += [jax.ShapeDtypeStruct((K, N), BF16), jax.ShapeDtypeStruct((K, N), BF16),
                      jax.ShapeDtypeStruct((N, D), BF16)]
        extra = [ng, nu, nd]
    res = pl.pallas_call(
        _glu_kernel, grid=(gi, gj), in_specs=in_specs, out_specs=out_specs, out_shape=out_shape,
        compiler_params=_params("parallel", "parallel"), name="glu_up")(a, wg, wu, *extra)
    return res[0], tuple(res[1:])


def ffn(h, w_gate, w_up, w_down, cast_next=None, riders=()):
    hid, nxt = glu_up(h, w_gate, w_up, cast_next)
    d_ff = hid.shape[1]
    tk = d_ff // 2 if (d_ff // 2) % HEAD_DIM == 0 else d_ff
    y = matmul(hid, w_down, F32, tm=1024, tn=512, tk=tk, riders=riders)
    return (y[0], nxt, y[1]) if riders else (y, nxt, ())


def _t5_thresholds():
    nb = REL_BUCKETS // 2
    max_exact = nb // 2
    n = np.arange(0, 4 * REL_MAX_DIST)
    nf = np.maximum(n, 1).astype(np.float32)
    large = max_exact + (np.log(nf / np.float32(max_exact))
                         / np.float32(math.log(REL_MAX_DIST / max_exact))
                         * np.float32(nb - max_exact)).astype(np.int32)
    bucket = np.where(n < max_exact, n, np.minimum(large, nb - 1))
    assert np.all(np.diff(bucket) >= 0) and bucket[-1] == nb - 1
    return tuple(int(np.argmax(bucket >= b)) for b in range(1, nb))


T5_THRESHOLDS = _t5_thresholds()


def _t5_tile_kernel(rb_ref, var_ref, o_ref, *, dil, half, n_heads, mult, keys_on_rows):
    h = pl.program_id(0)
    v = pl.program_id(1)
    off, col_lo, col_hi = var_ref[3 * v], var_ref[3 * v + 1], var_ref[3 * v + 2]
    rows, cols = o_ref.shape[2], o_ref.shape[3]
    i = lax.broadcasted_iota(jnp.int32, (rows, cols), 0)
    j = lax.broadcasted_iota(jnp.int32, (rows, cols), 1)
    nb = REL_BUCKETS // 2
    delta = (i - j if keys_on_rows else j - i) + off
    rel = delta * dil
    n = jnp.abs(rel)
    vneg = jnp.full((rows, cols), rb_ref[h], F32)
    vpos = jnp.full((rows, cols), rb_ref[nb * n_heads + h], F32)
    for b, t in enumerate(T5_THRESHOLDS, start=1):
        c = n >= t
        vneg = jnp.where(c, rb_ref[b * n_heads + h], vneg)
        vpos = jnp.where(c, rb_ref[(nb + b) * n_heads + h], vpos)
    val = jnp.where(rel > 0, vpos, vneg)
    if mult != 1.0:
        val = val * mult
    if half is not None:
        ok = (jnp.abs(delta) <= half) & (j >= col_lo) & (j < col_hi)
        val = jnp.where(ok, val, MASKED)
    o_ref[0, 0] = val


def t5_tiles(rel_bias, offs, rows, cols, dil=1, half=None, col_lo=None, col_hi=None, mult=1.0,
             keys_on_rows=False):
    n_heads = rel_bias.shape[1]
    nv = len(offs)
    col_lo = (0,) * nv if col_lo is None else col_lo
    col_hi = (cols,) * nv if col_hi is None else col_hi
    variants = jnp.asarray(np.stack([offs, col_lo, col_hi], axis=1).reshape(-1), jnp.int32)
    kern = functools.partial(_t5_tile_kernel, dil=dil, half=half, n_heads=n_heads, mult=mult,
                             keys_on_rows=keys_on_rows)
    smem = pl.BlockSpec(memory_space=pltpu.SMEM)
    return pl.pallas_call(
        kern, grid=(n_heads, nv), in_specs=[smem, smem],
        out_specs=pl.BlockSpec((1, 1, rows, cols), lambda h, v: (v, h, 0, 0)),
        out_shape=jax.ShapeDtypeStruct((nv, n_heads, rows, cols), F32),
        compiler_params=_params("parallel", "parallel"), name="t5_tiles")(rel_bias.reshape(-1), variants)


def _sublane_groups(x):
    return x.reshape(x.shape[0] // SUBLANES, SUBLANES, x.shape[1])


def _store_scores(s_sc, mp_sc, idx, s):
    s_sc[idx] = s
    mp_sc[idx] = jnp.max(_sublane_groups(s), axis=0)


def _online_step(m_sc, acc_sc, idx, s_sc, mp_sc, vt_chunk):
    Tk, Tq = s_sc.shape[1:]
    m_old = m_sc[idx]
    m_cur = jnp.max(mp_sc[idx], axis=0, keepdims=True)
    m_new = jnp.maximum(m_old, jnp.broadcast_to(m_cur, m_old.shape))
    m_sc[idx] = m_new
    alpha = jnp.exp2(m_old - m_new)
    pv = None
    for c in range(Tk // KEY_CHUNK):
        ks = slice(c * KEY_CHUNK, (c + 1) * KEY_CHUNK)
        p = jnp.exp2(_sublane_groups(s_sc[idx, ks, :]) - m_new[None])
        d = jnp.dot(vt_chunk(ks), p.reshape(KEY_CHUNK, Tq).astype(BF16), preferred_element_type=F32)
        pv = d if pv is None else pv + d
    acc_sc[idx] = (_sublane_groups(acc_sc[idx]) * alpha[None]).reshape(acc_sc.shape[1:]) + pv


def _flash_init(m_sc, acc_sc):
    m_sc[...] = jnp.full(m_sc.shape, -jnp.inf, F32)
    acc_sc[...] = jnp.zeros(acc_sc.shape, F32)


def _values_transposed(v, n_heads, Tk):
    S = v.shape[0]
    dv = v.shape[1] // n_heads
    v_t = v.reshape(S // Tk, Tk, n_heads, dv).transpose(2, 0, 3, 1)
    return jnp.concatenate([v_t, jnp.ones((n_heads, S // Tk, ONES_ROWS, Tk), v.dtype)], axis=2)


def _pipelined_blocks(nk, parts, scores, consume, buf0, buf1):
    assert nk % 2 == 0
    for part in parts:
        scores(0, *buf0, part)

    def step(kb, cur, nxt):
        for part in parts:
            scores(kb + 1, *nxt, part)
            consume(kb, *cur, part)

    def body(j, carry):
        step(2 * j, buf0, buf1)
        step(2 * j + 1, buf1, buf0)
        return carry

    lax.fori_loop(0, nk // 2 - 1, body, 0)
    step(nk - 2, buf0, buf1)
    for part in parts:
        consume(nk - 1, *buf1, part)


def _attn_a_kernel(q_ref, k_ref, vt_ref, band_ref, lam_ref, g_ref, o_ref, m_sc, acc_sc,
                   s0_sc, s1_sc, mp0_sc, mp1_sc, *, Tk, nk, q_step, k_step, e_lo, e_hi, lam_init):
    qb = pl.program_id(1)
    dv = 2 * HEAD_DIM
    _flash_init(m_sc, acc_sc)

    def scores(kb, s_sc, mp_sc, part):
        off = pl.multiple_of(kb * Tk, Tk)
        bidx = jnp.clip(kb * k_step - qb * q_step, e_lo, e_hi) - e_lo
        for m in part:
            sl = slice(m * HEAD_DIM, (m + 1) * HEAD_DIM)
            _store_scores(s_sc, mp_sc, m,
                          _dot_nt(k_ref[0, pl.ds(off, Tk), sl], q_ref[:, sl]) + band_ref[bidx, 0])

    def consume(kb, s_sc, mp_sc, part):
        for m in part:
            _online_step(m_sc, acc_sc, m, s_sc, mp_sc, lambda ks: vt_ref[0, kb, :, ks])

    _pipelined_blocks(nk, ((0, 1),), scores, consume, (s0_sc, mp0_sc), (s1_sc, mp1_sc))
    lp = lam_ref[...]
    lam = (jnp.exp(jnp.sum(lp[0:1] * lp[1:2], axis=-1, keepdims=True))
           - jnp.exp(jnp.sum(lp[2:3] * lp[3:4], axis=-1, keepdims=True)) + lam_init)
    acc0, acc1 = acc_sc[0], acc_sc[1]
    o_t = acc0[:dv] / acc0[dv:dv + 1] - lam * (acc1[:dv] / acc1[dv:dv + 1])
    o_ref[...] = (_rms(o_t.T, g_ref[...]) * (1.0 - lam_init)).astype(o_ref.dtype)


def mixer_a(h, w_in, w_out, lam_p, subln_g, rel_bias, layer_idx, T=512, Tk=1024):
    S, D = h.shape
    n_heads = D // (2 * HEAD_DIM)
    width = 2 * HEAD_DIM
    qkv = matmul(h, w_in, BF16)
    far = T5_THRESHOLDS[-1]
    unit = math.gcd(T, Tk)
    e_hi = -(-(far + T - 1) // unit)
    e_lo = -(-(far + Tk - 1) // unit)
    e_lo = -e_lo
    tiles = t5_tiles(rel_bias, [e * unit for e in range(e_lo, e_hi + 1)], Tk, T, mult=LOG2E,
                     keys_on_rows=True)
    n_tiles = e_hi - e_lo + 1
    nk = S // Tk
    v_t = _values_transposed(qkv[:, 2 * D:], n_heads, Tk)
    k_hm = qkv[:, D:2 * D].reshape(S, n_heads, width).transpose(1, 0, 2)
    lam_init = 0.8 - 0.6 * math.exp(-0.3 * layer_idx)
    kern = functools.partial(_attn_a_kernel, Tk=Tk, nk=nk, q_step=T // unit, k_step=Tk // unit,
                             e_lo=e_lo, e_hi=e_hi, lam_init=lam_init)
    once = pl.Buffered(1)
    o = pl.pallas_call(
        kern, grid=(n_heads, S // T),
        in_specs=[pl.BlockSpec((T, width), lambda hh, qb: (qb, hh)),
                  pl.BlockSpec((1, S, width), lambda hh, qb: (hh, 0, 0), pipeline_mode=once),
                  pl.BlockSpec((1, nk, width + ONES_ROWS, Tk), lambda hh, qb: (hh, 0, 0, 0),
                               pipeline_mode=once),
                  pl.BlockSpec((n_tiles, 1, Tk, T), lambda hh, qb: (0, hh, 0, 0), pipeline_mode=once),
                  pl.BlockSpec((4, HEAD_DIM), lambda hh, qb: (0, 0)),
                  pl.BlockSpec((1, width), lambda hh, qb: (0, 0))],
        out_specs=pl.BlockSpec((T, width), lambda hh, qb: (qb, hh)),
        out_shape=jax.ShapeDtypeStruct((S, D), BF16),
        scratch_shapes=[pltpu.VMEM((2, SUBLANES, T), F32), pltpu.VMEM((2, width + ONES_ROWS, T), F32),
                        pltpu.VMEM((2, Tk, T), F32), pltpu.VMEM((2, Tk, T), F32),
                        pltpu.VMEM((2, SUBLANES, T), F32), pltpu.VMEM((2, SUBLANES, T), F32)],
        compiler_params=_params("parallel", "arbitrary"),
        name="attn_a")(qkv, k_hm, v_t, tiles, lam_p, subln_g.reshape(1, width))
    return matmul(o, w_out, F32)


def _rope_kernel(x_ref, c_ref, sa_ref, sb_ref, g_ref, oq_ref, ok_ref, *, n_q, n_chunks):
    c, sa, sb = c_ref[...], sa_ref[...], sb_ref[...]
    for ch in range(n_chunks):
        sl = slice(ch * HEAD_DIM, (ch + 1) * HEAD_DIM)
        y = _rms(x_ref[:, sl], g_ref[0:1] if ch < n_q else g_ref[1:2])
        y = y * c + pltpu.roll(y, HEAD_DIM - 1, 1) * sa + pltpu.roll(y, 1, 1) * sb
        if ch < n_q:
            oq_ref[:, sl] = (y * (SCALE * LOG2E)).astype(oq_ref.dtype)
        else:
            ok_ref[ch - n_q] = y.astype(ok_ref.dtype)


def _rope_tables(S):
    pos = jnp.arange(S)
    n_freq = HEAD_DIM // 4
    freqs = ROPE_THETA ** (-jnp.arange(n_freq, dtype=F32) / n_freq)
    ang = jnp.concatenate([(pos // GRID_W).astype(F32)[:, None] * freqs,
                           (pos % GRID_W).astype(F32)[:, None] * freqs], axis=-1)
    cos, sin = jnp.cos(ang), jnp.sin(ang)
    zero = jnp.zeros_like(sin)
    c = jnp.stack([cos, cos], axis=-1).reshape(S, HEAD_DIM)
    sa = jnp.stack([-sin, zero], axis=-1).reshape(S, HEAD_DIM)
    sb = jnp.stack([zero, sin], axis=-1).reshape(S, HEAD_DIM)
    return c, sa, sb


def _attn_d_kernel(q_ref, k_ref, vt_ref, o_ref, m_sc, acc_sc, s0_sc, s1_sc, mp0_sc, mp1_sc,
                   *, Tk, nk, rep):
    _flash_init(m_sc, acc_sc)

    def scores(kb, s_sc, mp_sc, part):
        kblk = k_ref[0, pl.ds(pl.multiple_of(kb * Tk, Tk), Tk), :]
        for r in part:
            _store_scores(s_sc, mp_sc, r, _dot_nt(kblk, q_ref[:, r * HEAD_DIM:(r + 1) * HEAD_DIM]))

    def consume(kb, s_sc, mp_sc, part):
        for r in part:
            _online_step(m_sc, acc_sc, r, s_sc, mp_sc, lambda ks: vt_ref[0, kb, :, ks])

    parts = tuple(tuple(range(r, min(r + 2, rep))) for r in range(0, rep, 2))
    _pipelined_blocks(nk, parts, scores, consume, (s0_sc, mp0_sc), (s1_sc, mp1_sc))
    for r in range(rep):
        acc = acc_sc[r]
        o_t = acc[:HEAD_DIM] / acc[HEAD_DIM:HEAD_DIM + 1]
        o_ref[:, r * HEAD_DIM:(r + 1) * HEAD_DIM] = o_t.T.astype(o_ref.dtype)


def mixer_d(h, w_in, w_out, qk_g, T=512, Tk=1024, tm=256):
    S, D = h.shape
    n_q = D // HEAD_DIM
    n_kv = D_KV_HEADS
    rep = n_q // n_kv
    nqk = (n_q + n_kv) * HEAD_DIM
    tn = n_kv * HEAD_DIM
    assert nqk % tn == 0 and w_in.shape[1] == nqk + tn
    qk = matmul(h, w_in, F32, tn=tn, cols=(nqk // tn, lambda j: j))
    v = matmul(h, w_in, BF16, tn=tn, cols=(1, lambda j: j + nqk // tn))
    c, sa, sb = _rope_tables(S)
    tab = pl.BlockSpec((tm, HEAD_DIM), lambda i: (i, 0))
    q, k = pl.pallas_call(
        functools.partial(_rope_kernel, n_q=n_q, n_chunks=n_q + n_kv), grid=(S // tm,),
        in_specs=[pl.BlockSpec((tm, nqk), lambda i: (i, 0)), tab, tab, tab,
                  pl.BlockSpec((2, HEAD_DIM), lambda i: (0, 0))],
        out_specs=[pl.BlockSpec((tm, D), lambda i: (i, 0)),
                   pl.BlockSpec((n_kv, tm, HEAD_DIM), lambda i: (0, i, 0))],
        out_shape=[jax.ShapeDtypeStruct((S, D), BF16),
                   jax.ShapeDtypeStruct((n_kv, S, HEAD_DIM), BF16)],
        compiler_params=_params("parallel"), name="qk_norm_rope")(qk, c, sa, sb, qk_g)
    nk = S // Tk
    v_t = _values_transposed(v, n_kv, Tk)
    vt_rows = HEAD_DIM + ONES_ROWS
    kern = functools.partial(_attn_d_kernel, Tk=Tk, nk=nk, rep=rep)
    o = pl.pallas_call(
        kern, grid=(n_kv, S // T),
        in_specs=[pl.BlockSpec((T, rep * HEAD_DIM), lambda g, qb: (qb, g)),
                  pl.BlockSpec((1, S, HEAD_DIM), lambda g, qb: (g, 0, 0)),
                  pl.BlockSpec((1, nk, vt_rows, Tk), lambda g, qb: (g, 0, 0, 0))],
        out_specs=pl.BlockSpec((T, rep * HEAD_DIM), lambda g, qb: (qb, g)),
        out_shape=jax.ShapeDtypeStruct((S, D), BF16),
        scratch_shapes=[pltpu.VMEM((rep, SUBLANES, T), F32),
                        pltpu.VMEM((rep, vt_rows, T), F32),
                        pltpu.VMEM((rep, Tk, T), F32), pltpu.VMEM((rep, Tk, T), F32),
                        pltpu.VMEM((rep, SUBLANES, T), F32), pltpu.VMEM((rep, SUBLANES, T), F32)],
        compiler_params=_params("parallel", "arbitrary"), name="attn_d")(q, k, v_t)
    return matmul(o, w_out, F32)


def _lane_tiles(x):
    return [x[:, c * HEAD_DIM:(c + 1) * HEAD_DIM] for c in range(x.shape[1] // HEAD_DIM)]


def _win_attn_kernel(q_ref, kp_ref, kc_ref, kn_ref, vp_ref, vc_ref, vn_ref, bias_ref, o_ref,
                     *lse_ref, n_heads, tq, lead):
    lse_cols = []

    def scores(h):
        sl = slice(h * HEAD_DIM, (h + 1) * HEAD_DIM)
        q = q_ref[:, sl]
        ks = (kp_ref[:, sl], kc_ref[:, sl], kn_ref[:, sl])
        return [_dot_nt(q, ks[j]) + bias_ref[0, h, :, j * tq:(j + 1) * tq] for j in range(3)]

    def finish(h, ss):
        sl = slice(h * HEAD_DIM, (h + 1) * HEAD_DIM)
        vs = (vp_ref[:, sl], vc_ref[:, sl], vn_ref[:, sl])
        m = jnp.max(functools.reduce(jnp.maximum, [t for s in ss for t in _lane_tiles(s)]),
                    axis=-1, keepdims=True)
        ps = [jnp.exp(s - m) for s in ss]
        l = jnp.sum(functools.reduce(lambda a, b: a + b, [t for p in ps for t in _lane_tiles(p)]),
                    axis=-1, keepdims=True)
        acc = (jnp.dot(ps[0].astype(BF16), vs[0], preferred_element_type=F32)
               + jnp.dot(ps[1].astype(BF16), vs[1], preferred_element_type=F32)
               + jnp.dot(ps[2].astype(BF16), vs[2], preferred_element_type=F32))
        o_ref[:, sl] = (acc / l).astype(o_ref.dtype)
        lse_cols.append(m + jnp.log(l))

    pending = [scores(h) for h in range(min(lead, n_heads))]
    for h in range(n_heads):
        if h + lead < n_heads:
            pending.append(scores(h + lead))
        finish(h, pending.pop(0))
    if lse_ref:
        lse_ref[0][...] = jnp.concatenate(lse_cols, axis=1)


def _edge_variant(i, nblk):
    return jnp.where(i == 0, 0, jnp.where(i == nblk - 1, 2, 1))


def _merge_b_kernel(o0_ref, o1_ref, o2_ref, lse_ref, out_ref, *, n_heads):
    lse = lse_ref[...]
    w = jnp.exp(lse - jnp.max(lse, axis=0, keepdims=True))
    w = w / jnp.sum(w, axis=0, keepdims=True)
    for h in range(n_heads):
        sl = slice(h * HEAD_DIM, (h + 1) * HEAD_DIM)
        out_ref[:, sl] = (w[0, :, h:h + 1] * o0_ref[:, sl] + w[1, :, h:h + 1] * o1_ref[:, sl]
                          + w[2, :, h:h + 1] * o2_ref[:, sl]).astype(out_ref.dtype)


def mixer_b(h, w_in, w_out, rel_bias, tq=128, tm=256):
    S, D = h.shape
    G = len(B_GROUPS)
    n_heads = rel_bias.shape[1]
    width = n_heads * HEAD_DIM
    tn = width // 2
    outs, lses = [], []
    for g, (window, dil) in enumerate(B_GROUPS):
        half = window // 2 // dil
        assert half <= tq
        L = S // dil
        nblk = L // tq
        assert nblk >= 2

        def by_residue(a, dil=dil, L=L):
            return a if dil == 1 else a.reshape(L, dil, -1).transpose(1, 0, 2).reshape(S, -1)

        def by_position(a, dil=dil, L=L):
            return a if dil == 1 else a.reshape(dil, L, -1).transpose(1, 0, 2).reshape(S, -1)

        per = width // tn
        qkv = matmul(by_residue(h), w_in, BF16, tn=tn,
                     cols=(3 * per, lambda j, g=g, per=per: (j // per) * (G * per) + g * per + j % per))
        bias = t5_tiles(rel_bias, [-tq] * 3, tq, 3 * tq, dil=dil, half=half,
                        col_lo=(tq, 0, 0), col_hi=(3 * tq, 3 * tq, 2 * tq))

        def spec(part, di, nblk=nblk):
            return pl.BlockSpec(
                (tq, width), lambda r, i: (r * nblk + jnp.clip(i + di, 0, nblk - 1), part))

        o_g, lse_g = pl.pallas_call(
            functools.partial(_win_attn_kernel, n_heads=n_heads, tq=tq, lead=4), grid=(dil, nblk),
            in_specs=[spec(0, 0), spec(1, -1), spec(1, 0), spec(1, 1),
                      spec(2, -1), spec(2, 0), spec(2, 1),
                      pl.BlockSpec((1, n_heads, tq, 3 * tq),
                                   lambda r, i, nblk=nblk: (_edge_variant(i, nblk), 0, 0, 0))],
            out_specs=[pl.BlockSpec((tq, width), lambda r, i, nblk=nblk: (r * nblk + i, 0)),
                       pl.BlockSpec((tq, n_heads), lambda r, i, nblk=nblk: (r * nblk + i, 0))],
            out_shape=[jax.ShapeDtypeStruct((S, width), F32),
                       jax.ShapeDtypeStruct((S, n_heads), F32)],
            compiler_params=_params("parallel", "parallel"),
            name=f"attn_b{g}")(qkv, qkv, qkv, qkv, qkv, qkv, qkv, bias)
        outs.append(by_position(o_g))
        lses.append(by_position(lse_g))
    row = pl.BlockSpec((tm, width), lambda i: (i, 0))
    o = pl.pallas_call(
        functools.partial(_merge_b_kernel, n_heads=n_heads), grid=(S // tm,),
        in_specs=[row, row, row, pl.BlockSpec((G, tm, n_heads), lambda i: (0, i, 0))],
        out_specs=row, out_shape=jax.ShapeDtypeStruct((S, width), BF16),
        compiler_params=_params("parallel"), name="merge_b")(*outs, jnp.stack(lses, 0))
    return matmul(o, w_out, F32)


def _c_bias_kernel(rpb_ref, o_ref, *, rq, n_dr, n_dc):
    h = pl.program_id(0)
    W = GRID_W
    qc = lax.broadcasted_iota(jnp.int32, (W, 2 * W), 0)
    lane = lax.broadcasted_iota(jnp.int32, (W, 2 * W), 1)
    kc = lane & (W - 1)
    dc = kc - qc
    c0 = jnp.clip(qc - C_WIN_C // 2, 0, W - C_WIN_C)
    ok_c = (kc >= c0) & (kc < c0 + C_WIN_C)
    masked = jnp.full((W, 2 * W), MASKED, F32)
    base = h * (n_dr * n_dc)
    sub = []
    for a in range(n_dr):
        val = masked
        for b in range(n_dc):
            val = jnp.where(dc == b - (C_WIN_C - 1), rpb_ref[base + a * n_dc + b], val)
        sub.append(jnp.where(ok_c, val, MASKED))
    for v in range(3):
        for qr in range(rq):
            for jv in range(3 * rq // 2):
                halves = []
                for kr in (2 * jv, 2 * jv + 1):
                    lo, hi = ((rq, 3 * rq - 1), (qr, qr + C_WIN_R - 1), (0, C_WIN_R - 1))[v]
                    halves.append(sub[kr - qr + C_WIN_R - 1 - rq] if lo <= kr <= hi else masked)
                o_ref[v, 0, qr * W:(qr + 1) * W, jv * 2 * W:(jv + 1) * 2 * W] = jnp.where(
                    lane < W, halves[0], halves[1])


def mixer_c(h, w_in, w_out, rpb, heads_per_step=8):
    S, D = h.shape
    n_heads = D // HEAD_DIM
    rows = S // GRID_W
    rq = C_WIN_R // 2
    tq = rq * GRID_W
    nblk = rows // rq
    assert rows >= C_WIN_R and nblk >= 2 and 2 * GRID_W == HEAD_DIM
    n_dr, n_dc = 2 * C_WIN_R - 1, 2 * C_WIN_C - 1
    qkv = matmul(h, w_in, BF16)
    bias = pl.pallas_call(
        functools.partial(_c_bias_kernel, rq=rq, n_dr=n_dr, n_dc=n_dc), grid=(n_heads,),
        in_specs=[pl.BlockSpec(memory_space=pltpu.SMEM)],
        out_specs=pl.BlockSpec((3, 1, tq, 3 * tq), lambda hh: (0, hh, 0, 0)),
        out_shape=jax.ShapeDtypeStruct((3, n_heads, tq, 3 * tq), F32),
        compiler_params=_params("parallel"), name="c_bias")(rpb.reshape(-1))
    hs = heads_per_step
    n_hg = n_heads // hs
    width = hs * HEAD_DIM

    def spec(part, di):
        return pl.BlockSpec(
            (tq, width), lambda hg, i: (jnp.clip(i + di, 0, nblk - 1), part * n_hg + hg))

    o = pl.pallas_call(
        functools.partial(_win_attn_kernel, n_heads=hs, tq=tq, lead=2), grid=(n_hg, nblk),
        in_specs=[spec(0, 0), spec(1, -1), spec(1, 0), spec(1, 1),
                  spec(2, -1), spec(2, 0), spec(2, 1),
                  pl.BlockSpec((1, hs, tq, 3 * tq),
                               lambda hg, i: (_edge_variant(i, nblk), hg, 0, 0))],
        out_specs=pl.BlockSpec((tq, width), lambda hg, i: (i, hg)),
        out_shape=jax.ShapeDtypeStruct((S, D), BF16),
        compiler_params=_params("parallel", "parallel"),
        name="attn_c")(qkv, qkv, qkv, qkv, qkv, qkv, qkv, bias)
    return matmul(o, w_out, F32)


def kernel(x, rel_bias, norm_g, a_w_in, a_w_out, a_lambda, a_subln, b_w_in, b_w_out, c_w_in, c_w_out,
           c_rpb, d_w_in, d_w_out, d_qk_norm, ffn_w_gate, ffn_w_up, ffn_w_down):
    B, S, D = x.shape
    assert B == 1
    depth = norm_g.shape[0]
    xs = x.reshape(S, D)
    hn = prenorm(xs, norm_g[0, 0])

    def mixer_weights(layer):
        m, j = layer % N_MIXERS, layer // N_MIXERS
        w_in, w_out, q_cols, q_scale = (
            (a_w_in, a_w_out, D, SCALE * LOG2E), (b_w_in, b_w_out, b_w_in.shape[2] // 3, SCALE),
            (c_w_in, c_w_out, D, SCALE), (d_w_in, d_w_out, 0, None))[m]
        scale = None if q_scale is None else jnp.concatenate(
            [jnp.full((q_cols,), q_scale, F32), jnp.ones((w_in.shape[2] - q_cols,), F32)])
        return w_in, w_out, j, scale

    w_in, w_out, j, scale = mixer_weights(0)
    mix_w = ((w_in[j] if scale is None else w_in[j] * scale).astype(BF16), w_out[j].astype(BF16))
    ffn_w = (ffn_w_gate[0].astype(BF16), ffn_w_up[0].astype(BF16), ffn_w_down[0].astype(BF16))
    for i in range(depth):
        m, j = i % N_MIXERS, i // N_MIXERS
        if m == 0:
            y = mixer_a(hn, *mix_w, a_lambda[j], a_subln[j], rel_bias, i)
        elif m == 1:
            y = mixer_b(hn, *mix_w, rel_bias)
        elif m == 2:
            y = mixer_c(hn, *mix_w, c_rpb[j])
        else:
            y = mixer_d(hn, *mix_w, d_qk_norm[j])
        xs, hn = resid(xs, y, norm_g[i, 1], norm_g[i, 2])
        cast_next, riders = None, ()
        if i + 1 < depth:
            cast_next = (i + 1, ffn_w_gate, ffn_w_up, ffn_w_down)
            w_in, w_out, jn, scale = mixer_weights(i + 1)
            riders = ((w_in, jn, scale), (w_out, jn, None))
        y, ffn_w, mix_w = ffn(hn, *ffn_w, cast_next, riders)
        if i + 1 < depth:
            xs, hn = resid(xs, y, norm_g[i, 3], norm_g[i + 1, 0])
        else:
            xs = resid(xs, y, norm_g[i, 3])
    return xs.reshape(B, S, D)
```

```python
import functools
import math

import numpy as np
import jax
import jax.numpy as jnp
from jax import lax
from jax.experimental import pallas as pl
from jax.experimental.pallas import tpu as pltpu

F32 = jnp.float32
BF16 = jnp.bfloat16

HEAD_DIM = 128
SCALE = HEAD_DIM ** -0.5
LOG2E = math.log2(math.e)
GRID_W = 64
EPS = 1e-6
REL_BUCKETS = 32
REL_MAX_DIST = 1024
B_GROUPS = ((128, 1), (512, 4), (2048, 16))
C_WIN_R = 8
C_WIN_C = 16
D_KV_HEADS = 8
ROPE_THETA = 10000.0
N_MIXERS = 4

KEY_CHUNK = 256
SUBLANES = 8
ONES_ROWS = 2 * SUBLANES
MASKED = -1e30
VMEM_LIMIT_BYTES = 56 * 2 ** 20


def _params(*semantics):
    return pltpu.CompilerParams(dimension_semantics=semantics, vmem_limit_bytes=VMEM_LIMIT_BYTES)


def _rms(x, g):
    ms = jnp.mean(x * x, axis=-1, keepdims=True)
    return x * lax.rsqrt(ms + EPS) * g


def _dot_nt(a, b):
    return lax.dot_general(a, b, (((1,), (1,)), ((), ())), preferred_element_type=F32)


def _prenorm_kernel(x_ref, g_ref, h_ref):
    h_ref[...] = _rms(x_ref[...], g_ref[...]).astype(h_ref.dtype)


def prenorm(x, g, tm=256):
    S, D = x.shape
    row = pl.BlockSpec((tm, D), lambda i: (i, 0))
    vec = pl.BlockSpec((1, D), lambda i: (0, 0))
    return pl.pallas_call(
        _prenorm_kernel, grid=(S // tm,), in_specs=[row, vec], out_specs=row,
        out_shape=jax.ShapeDtypeStruct((S, D), BF16), compiler_params=_params("parallel"),
        name="prenorm")(x, g.reshape(1, D))


def _resid_kernel(x_ref, y_ref, g1_ref, g2_ref, xo_ref, h_ref):
    x = x_ref[...] + _rms(y_ref[...], g1_ref[...])
    xo_ref[...] = x
    h_ref[...] = _rms(x, g2_ref[...]).astype(h_ref.dtype)


def _resid_last_kernel(x_ref, y_ref, g1_ref, xo_ref):
    xo_ref[...] = x_ref[...] + _rms(y_ref[...], g1_ref[...])


def resid(x, y, g_post, g_next=None, tm=256):
    S, D = x.shape
    row = pl.BlockSpec((tm, D), lambda i: (i, 0))
    vec = pl.BlockSpec((1, D), lambda i: (0, 0))
    if g_next is None:
        return pl.pallas_call(
            _resid_last_kernel, grid=(S // tm,), in_specs=[row, row, vec], out_specs=row,
            out_shape=jax.ShapeDtypeStruct((S, D), F32), compiler_params=_params("parallel"),
            name="resid_last")(x, y, g_post.reshape(1, D))
    return pl.pallas_call(
        _resid_kernel, grid=(S // tm,), in_specs=[row, row, vec, vec], out_specs=[row, row],
        out_shape=[jax.ShapeDtypeStruct((S, D), F32), jax.ShapeDtypeStruct((S, D), BF16)],
        compiler_params=_params("parallel"),
        name="resid")(x, y, g_post.reshape(1, D), g_next.reshape(1, D))


def _mm_kernel(a_ref, b_ref, o_ref):
    o_ref[...] = jnp.dot(a_ref[...], b_ref[...], preferred_element_type=F32).astype(o_ref.dtype)


def _mm_acc_kernel(a_ref, b_ref, *rest, scaled):
    n_in = len(scaled) + sum(scaled)
    o_ref, acc_ref = rest[n_in], rest[-1]
    k = pl.program_id(2)

    @pl.when(k == 0)
    def _():
        acc_ref[...] = jnp.zeros_like(acc_ref)

    acc_ref[...] += jnp.dot(a_ref[...], b_ref[...], preferred_element_type=F32)

    @pl.when(k == pl.num_programs(2) - 1)
    def _():
        o_ref[...] = acc_ref[...].astype(o_ref.dtype)

    srcs = list(rest[:n_in])
    for has_scale, dst in zip(scaled, rest[n_in + 1:-1]):
        w = srcs.pop(0)[...]
        if has_scale:
            w = w * srcs.pop(0)[...]
        dst[...] = w.astype(dst.dtype)


def matmul(a, b, out_dtype, tm=1024, tn=None, tk=None, cols=None, riders=()):
    M, K = a.shape
    tk = K if tk is None else tk
    if tn is None:
        tn = 1024 if b.shape[1] % 1024 == 0 else 512
    n_tiles, col = (b.shape[1] // tn, lambda j: j) if cols is None else cols
    out_shape = jax.ShapeDtypeStruct((M, n_tiles * tn), out_dtype)
    if tk == K:
        assert not riders
        return pl.pallas_call(
            _mm_kernel, grid=(M // tm, n_tiles),
            in_specs=[pl.BlockSpec((tm, K), lambda i, j: (i, 0)),
                      pl.BlockSpec((K, tn), lambda i, j: (0, col(j)))],
            out_specs=pl.BlockSpec((tm, tn), lambda i, j: (i, j)),
            out_shape=out_shape, compiler_params=_params("parallel", "parallel"),
            name="matmul")(a, b)
    gi, gj, gk = M // tm, n_tiles, K // tk
    in_specs = [pl.BlockSpec((tm, tk), lambda i, j, k: (i, k)),
                pl.BlockSpec((tk, tn), lambda i, j, k: (k, col(j)))]
    out_specs = [pl.BlockSpec((tm, tn), lambda i, j, k: (i, j))]
    out_shapes = [out_shape]
    operands = [a, b]
    for w, layer, scale in riders:
        R, C = w.shape[1:]
        rb, cb = R // gi, C // (gj * gk)
        assert rb * gi == R and cb * gj * gk == C and rb % SUBLANES == 0 and cb % HEAD_DIM == 0
        in_specs.append(pl.BlockSpec((None, rb, cb), lambda i, j, k, layer=layer: (layer, i, j * gk + k)))
        operands.append(w)
        if scale is not None:
            in_specs.append(pl.BlockSpec((1, cb), lambda i, j, k: (0, j * gk + k)))
            operands.append(scale.reshape(1, C))
        out_specs.append(pl.BlockSpec((rb, cb), lambda i, j, k: (i, j * gk + k)))
        out_shapes.append(jax.ShapeDtypeStruct((R, C), BF16))
    res = pl.pallas_call(
        functools.partial(_mm_acc_kernel, scaled=tuple(s is not None for _, _, s in riders)),
        grid=(gi, gj, gk), in_specs=in_specs, out_specs=out_specs, out_shape=out_shapes,
        scratch_shapes=[pltpu.VMEM((tm, tn), F32)],
        compiler_params=_params("parallel", "parallel", "arbitrary"),
        name="matmul_acc")(*operands)
    return (res[0], tuple(res[1:])) if riders else res[0]


def _glu_kernel(a_ref, wg_ref, wu_ref, *rest):
    o_ref = rest[len(rest) // 2]
    a = a_ref[...]
    g = jnp.dot(a, wg_ref[...], preferred_element_type=F32)
    u = jnp.dot(a, wu_ref[...], preferred_element_type=F32)
    o_ref[...] = (g * (1.0 / (1.0 + jnp.exp(-g))) * u).astype(o_ref.dtype)
    n_cast = len(rest) // 2
    for src, dst in zip(rest[:n_cast], rest[n_cast + 1:]):
        dst[...] = src[...].astype(dst.dtype)


def glu_up(a, wg, wu, cast_next=None, tm=2048, tn=256):
    M, K = a.shape
    N = wg.shape[1]
    gi, gj = M // tm, N // tn
    wspec = pl.BlockSpec((K, tn), lambda i, j: (0, j))
    in_specs = [pl.BlockSpec((tm, K), lambda i, j: (i, 0)), wspec, wspec]
    out_specs = [pl.BlockSpec((tm, tn), lambda i, j: (i, j))]
    out_shape = [jax.ShapeDtypeStruct((M, N), BF16)]
    extra = []
    if cast_next is not None:
        layer, ng, nu, nd = cast_next
        D = nd.shape[2]
        up_spec = pl.BlockSpec((None, K // gi, tn), lambda i, j: (layer, i, j))
        down_spec = pl.BlockSpec((None, tn, D // gi), lambda i, j: (layer, j, i))
        in_specs += [up_spec, up_spec, down_spec]
        out_specs += [pl.BlockSpec((K // gi, tn), lambda i, j: (i, j)),
                      pl.BlockSpec((K // gi, tn), lambda i, j: (i, j)),
                      pl.BlockSpec((tn, D // gi), lambda i, j: (j, i))]
        out_shape += [jax.ShapeDtypeStruct((K, N), BF16), jax.ShapeDtypeStruct((K, N), BF16),
                      jax.ShapeDtypeStruct((N, D), BF16)]
        extra = [ng, nu, nd]
    res = pl.pallas_call(
        _glu_kernel, grid=(gi, gj), in_specs=in_specs, out_specs=out_specs, out_shape=out_shape,
        compiler_params=_params("parallel", "parallel"), name="glu_up")(a, wg, wu, *extra)
    return res[0], tuple(res[1:])


def ffn(h, w_gate, w_up, w_down, cast_next=None, riders=()):
    hid, nxt = glu_up(h, w_gate, w_up, cast_next)
    d_ff = hid.shape[1]
    tk = d_ff // 2 if (d_ff // 2) % HEAD_DIM == 0 else d_ff
    y = matmul(hid, w_down, F32, tm=1024, tn=512, tk=tk, riders=riders)
    return (y[0], nxt, y[1]) if riders else (y, nxt, ())


def _t5_thresholds():
    nb = REL_BUCKETS // 2
    max_exact = nb // 2
    n = np.arange(0, 4 * REL_MAX_DIST)
    nf = np.maximum(n, 1).astype(np.float32)
    large = max_exact + (np.log(nf / np.float32(max_exact))
                         / np.float32(math.log(REL_MAX_DIST / max_exact))
                         * np.float32(nb - max_exact)).astype(np.int32)
    bucket = np.where(n < max_exact, n, np.minimum(large, nb - 1))
    assert np.all(np.diff(bucket) >= 0) and bucket[-1] == nb - 1
    return tuple(int(np.argmax(bucket >= b)) for b in range(1, nb))


T5_THRESHOLDS = _t5_thresholds()


def _t5_tile_kernel(rb_ref, var_ref, o_ref, *, dil, half, n_heads, mult, keys_on_rows):
    h = pl.program_id(0)
    v = pl.program_id(1)
    off, col_lo, col_hi = var_ref[3 * v], var_ref[3 * v + 1], var_ref[3 * v + 2]
    rows, cols = o_ref.shape[2], o_ref.shape[3]
    i = lax.broadcasted_iota(jnp.int32, (rows, cols), 0)
    j = lax.broadcasted_iota(jnp.int32, (rows, cols), 1)
    nb = REL_BUCKETS // 2
    delta = (i - j if keys_on_rows else j - i) + off
    rel = delta * dil
    n = jnp.abs(rel)
    vneg = jnp.full((rows, cols), rb_ref[h], F32)
    vpos = jnp.full((rows, cols), rb_ref[nb * n_heads + h], F32)
    for b, t in enumerate(T5_THRESHOLDS, start=1):
        c = n >= t
        vneg = jnp.where(c, rb_ref[b * n_heads + h], vneg)
        vpos = jnp.where(c, rb_ref[(nb + b) * n_heads + h], vpos)
    val = jnp.where(rel > 0, vpos, vneg)
    if mult != 1.0:
        val = val * mult
    if half is not None:
        ok = (jnp.abs(delta) <= half) & (j >= col_lo) & (j < col_hi)
        val = jnp.where(ok, val, MASKED)
    o_ref[0, 0] = val


def t5_tiles(rel_bias, offs, rows, cols, dil=1, half=None, col_lo=None, col_hi=None, mult=1.0,
             keys_on_rows=False):
    n_heads = rel_bias.shape[1]
    nv = len(offs)
    col_lo = (0,) * nv if col_lo is None else col_lo
    col_hi = (cols,) * nv if col_hi is None else col_hi
    variants = jnp.asarray(np.stack([offs, col_lo, col_hi], axis=1).reshape(-1), jnp.int32)
    kern = functools.partial(_t5_tile_kernel, dil=dil, half=half, n_heads=n_heads, mult=mult,
                             keys_on_rows=keys_on_rows)
    smem = pl.BlockSpec(memory_space=pltpu.SMEM)
    return pl.pallas_call(
        kern, grid=(n_heads, nv), in_specs=[smem, smem],
        out_specs=pl.BlockSpec((1, 1, rows, cols), lambda h, v: (v, h, 0, 0)),
        out_shape=jax.ShapeDtypeStruct((nv, n_heads, rows, cols), F32),
        compiler_params=_params("parallel", "parallel"), name="t5_tiles")(rel_bias.reshape(-1), variants)


def _sublane_groups(x):
    return x.reshape(x.shape[0] // SUBLANES, SUBLANES, x.shape[1])


def _store_scores(s_sc, mp_sc, idx, s):
    s_sc[idx] = s
    mp_sc[idx] = jnp.max(_sublane_groups(s), axis=0)


def _online_step(m_sc, acc_sc, idx, s_sc, mp_sc, vt_chunk):
    Tk, Tq = s_sc.shape[1:]
    m_old = m_sc[idx]
    m_cur = jnp.max(mp_sc[idx], axis=0, keepdims=True)
    m_new = jnp.maximum(m_old, jnp.broadcast_to(m_cur, m_old.shape))
    m_sc[idx] = m_new
    alpha = jnp.exp2(m_old - m_new)
    pv = None
    for c in range(Tk // KEY_CHUNK):
        ks = slice(c * KEY_CHUNK, (c + 1) * KEY_CHUNK)
        p = jnp.exp2(_sublane_groups(s_sc[idx, ks, :]) - m_new[None])
        d = jnp.dot(vt_chunk(ks), p.reshape(KEY_CHUNK, Tq).astype(BF16), preferred_element_type=F32)
        pv = d if pv is None else pv + d
    acc_sc[idx] = (_sublane_groups(acc_sc[idx]) * alpha[None]).reshape(acc_sc.shape[1:]) + pv


def _flash_init(m_sc, acc_sc):
    m_sc[...] = jnp.full(m_sc.shape, -jnp.inf, F32)
    acc_sc[...] = jnp.zeros(acc_sc.shape, F32)


def _values_transposed(v, n_heads, Tk):
    S = v.shape[0]
    dv = v.shape[1] // n_heads
    v_t = v.reshape(S // Tk, Tk, n_heads, dv).transpose(2, 0, 3, 1)
    return jnp.concatenate([v_t, jnp.ones((n_heads, S // Tk, ONES_ROWS, Tk), v.dtype)], axis=2)


def _pipelined_blocks(nk, parts, scores, consume, buf0, buf1):
    assert nk % 2 == 0
    for part in parts:
        scores(0, *buf0, part)

    def step(kb, cur, nxt):
        for part in parts:
            scores(kb + 1, *nxt, part)
            consume(kb, *cur, part)

    def body(j, carry):
        step(2 * j, buf0, buf1)
        step(2 * j + 1, buf1, buf0)
        return carry

    lax.fori_loop(0, nk // 2 - 1, body, 0)
    step(nk - 2, buf0, buf1)
    for part in parts:
        consume(nk - 1, *buf1, part)


def _attn_a_kernel(q_ref, k_ref, vt_ref, band_ref, lam_ref, g_ref, o_ref, m_sc, acc_sc,
                   s0_sc, s1_sc, mp0_sc, mp1_sc, *, Tk, nk, q_step, k_step, e_lo, e_hi, lam_init):
    qb = pl.program_id(1)
    dv = 2 * HEAD_DIM
    _flash_init(m_sc, acc_sc)

    def scores(kb, s_sc, mp_sc, part):
        off = pl.multiple_of(kb * Tk, Tk)
        bidx = jnp.clip(kb * k_step - qb * q_step, e_lo, e_hi) - e_lo
        for m in part:
            sl = slice(m * HEAD_DIM, (m + 1) * HEAD_DIM)
            _store_scores(s_sc, mp_sc, m,
                          _dot_nt(k_ref[0, pl.ds(off, Tk), sl], q_ref[:, sl]) + band_ref[bidx, 0])

    def consume(kb, s_sc, mp_sc, part):
        for m in part:
            _online_step(m_sc, acc_sc, m, s_sc, mp_sc, lambda ks: vt_ref[0, kb, :, ks])

    _pipelined_blocks(nk, ((0, 1),), scores, consume, (s0_sc, mp0_sc), (s1_sc, mp1_sc))
    lp = lam_ref[...]
    lam = (jnp.exp(jnp.sum(lp[0:1] * lp[1:2], axis=-1, keepdims=True))
           - jnp.exp(jnp.sum(lp[2:3] * lp[3:4], axis=-1, keepdims=True)) + lam_init)
    acc0, acc1 = acc_sc[0], acc_sc[1]
    o_t = acc0[:dv] / acc0[dv:dv + 1] - lam * (acc1[:dv] / acc1[dv:dv + 1])
    o_ref[...] = (_rms(o_t.T, g_ref[...]) * (1.0 - lam_init)).astype(o_ref.dtype)


def mixer_a(h, w_in, w_out, lam_p, subln_g, rel_bias, layer_idx, T=512, Tk=1024):
    S, D = h.shape
    n_heads = D // (2 * HEAD_DIM)
    width = 2 * HEAD_DIM
    qkv = matmul(h, w_in, BF16)
    far = T5_THRESHOLDS[-1]
    unit = math.gcd(T, Tk)
    e_hi = -(-(far + T - 1) // unit)
    e_lo = -(-(far + Tk - 1) // unit)
    e_lo = -e_lo
    tiles = t5_tiles(rel_bias, [e * unit for e in range(e_lo, e_hi + 1)], Tk, T, mult=LOG2E,
                     keys_on_rows=True)
    n_tiles = e_hi - e_lo + 1
    nk = S // Tk
    v_t = _values_transposed(qkv[:, 2 * D:], n_heads, Tk)
    k_hm = qkv[:, D:2 * D].reshape(S, n_heads, width).transpose(1, 0, 2)
    lam_init = 0.8 - 0.6 * math.exp(-0.3 * layer_idx)
    kern = functools.partial(_attn_a_kernel, Tk=Tk, nk=nk, q_step=T // unit, k_step=Tk // unit,
                             e_lo=e_lo, e_hi=e_hi, lam_init=lam_init)
    once = pl.Buffered(1)
    o = pl.pallas_call(
        kern, grid=(n_heads, S // T),
        in_specs=[pl.BlockSpec((T, width), lambda hh, qb: (qb, hh)),
                  pl.BlockSpec((1, S, width), lambda hh, qb: (hh, 0, 0), pipeline_mode=once),
                  pl.BlockSpec((1, nk, width + ONES_ROWS, Tk), lambda hh, qb: (hh, 0, 0, 0),
                               pipeline_mode=once),
                  pl.BlockSpec((n_tiles, 1, Tk, T), lambda hh, qb: (0, hh, 0, 0), pipeline_mode=once),
                  pl.BlockSpec((4, HEAD_DIM), lambda hh, qb: (0, 0)),
                  pl.BlockSpec((1, width), lambda hh, qb: (0, 0))],
        out_specs=pl.BlockSpec((T, width), lambda hh, qb: (qb, hh)),
        out_shape=jax.ShapeDtypeStruct((S, D), BF16),
        scratch_shapes=[pltpu.VMEM((2, SUBLANES, T), F32), pltpu.VMEM((2, width + ONES_ROWS, T), F32),
                        pltpu.VMEM((2, Tk, T), F32), pltpu.VMEM((2, Tk, T), F32),
                        pltpu.VMEM((2, SUBLANES, T), F32), pltpu.VMEM((2, SUBLANES, T), F32)],
        compiler_params=_params("parallel", "arbitrary"),
        name="attn_a")(qkv, k_hm, v_t, tiles, lam_p, subln_g.reshape(1, width))
    return matmul(o, w_out, F32)


def _rope_kernel(x_ref, c_ref, sa_ref, sb_ref, g_ref, oq_ref, ok_ref, *, n_q, n_chunks):
    c, sa, sb = c_ref[...], sa_ref[...], sb_ref[...]
    for ch in range(n_chunks):
        sl = slice(ch * HEAD_DIM, (ch + 1) * HEAD_DIM)
        y = _rms(x_ref[:, sl], g_ref[0:1] if ch < n_q else g_ref[1:2])
        y = y * c + pltpu.roll(y, HEAD_DIM - 1, 1) * sa + pltpu.roll(y, 1, 1) * sb
        if ch < n_q:
            oq_ref[:, sl] = (y * (SCALE * LOG2E)).astype(oq_ref.dtype)
        else:
            ok_ref[ch - n_q] = y.astype(ok_ref.dtype)


def _rope_tables(S):
    pos = jnp.arange(S)
    n_freq = HEAD_DIM // 4
    freqs = ROPE_THETA ** (-jnp.arange(n_freq, dtype=F32) / n_freq)
    ang = jnp.concatenate([(pos // GRID_W).astype(F32)[:, None] * freqs,
                           (pos % GRID_W).astype(F32)[:, None] * freqs], axis=-1)
    cos, sin = jnp.cos(ang), jnp.sin(ang)
    zero = jnp.zeros_like(sin)
    c = jnp.stack([cos, cos], axis=-1).reshape(S, HEAD_DIM)
    sa = jnp.stack([-sin, zero], axis=-1).reshape(S, HEAD_DIM)
    sb = jnp.stack([zero, sin], axis=-1).reshape(S, HEAD_DIM)
    return c, sa, sb


def _attn_d_kernel(q_ref, k_ref, vt_ref, o_ref, m_sc, acc_sc, s0_sc, s1_sc, mp0_sc, mp1_sc,
                   *, Tk, nk, rep):
    _flash_init(m_sc, acc_sc)

    def scores(kb, s_sc, mp_sc, part):
        kblk = k_ref[0, pl.ds(pl.multiple_of(kb * Tk, Tk), Tk), :]
        for r in part:
            _store_scores(s_sc, mp_sc, r, _dot_nt(kblk, q_ref[:, r * HEAD_DIM:(r + 1) * HEAD_DIM]))

    def consume(kb, s_sc, mp_sc, part):
        for r in part:
            _online_step(m_sc, acc_sc, r, s_sc, mp_sc, lambda ks: vt_ref[0, kb, :, ks])

    parts = tuple(tuple(range(r, min(r + 2, rep))) for r in range(0, rep, 2))
    _pipelined_blocks(nk, parts, scores, consume, (s0_sc, mp0_sc), (s1_sc, mp1_sc))
    for r in range(rep):
        acc = acc_sc[r]
        o_t = acc[:HEAD_DIM] / acc[HEAD_DIM:HEAD_DIM + 1]
        o_ref[:, r * HEAD_DIM:(r + 1) * HEAD_DIM] = o_t.T.astype(o_ref.dtype)


def mixer_d(h, w_in, w_out, qk_g, T=512, Tk=1024, tm=256):
    S, D = h.shape
    n_q = D // HEAD_DIM
    n_kv = D_KV_HEADS
    rep = n_q // n_kv
    nqk = (n_q + n_kv) * HEAD_DIM
    tn = n_kv * HEAD_DIM
    assert nqk % tn == 0 and w_in.shape[1] == nqk + tn
    qk = matmul(h, w_in, F32, tn=tn, cols=(nqk // tn, lambda j: j))
    v = matmul(h, w_in, BF16, tn=tn, cols=(1, lambda j: j + nqk // tn))
    c, sa, sb = _rope_tables(S)
    tab = pl.BlockSpec((tm, HEAD_DIM), lambda i: (i, 0))
    q, k = pl.pallas_call(
        functools.partial(_rope_kernel, n_q=n_q, n_chunks=n_q + n_kv), grid=(S // tm,),
        in_specs=[pl.BlockSpec((tm, nqk), lambda i: (i, 0)), tab, tab, tab,
                  pl.BlockSpec((2, HEAD_DIM), lambda i: (0, 0))],
        out_specs=[pl.BlockSpec((tm, D), lambda i: (i, 0)),
                   pl.BlockSpec((n_kv, tm, HEAD_DIM), lambda i: (0, i, 0))],
        out_shape=[jax.ShapeDtypeStruct((S, D), BF16),
                   jax.ShapeDtypeStruct((n_kv, S, HEAD_DIM), BF16)],
        compiler_params=_params("parallel"), name="qk_norm_rope")(qk, c, sa, sb, qk_g)
    nk = S // Tk
    v_t = _values_transposed(v, n_kv, Tk)
    vt_rows = HEAD_DIM + ONES_ROWS
    kern = functools.partial(_attn_d_kernel, Tk=Tk, nk=nk, rep=rep)
    o = pl.pallas_call(
        kern, grid=(n_kv, S // T),
        in_specs=[pl.BlockSpec((T, rep * HEAD_DIM), lambda g, qb: (qb, g)),
                  pl.BlockSpec((1, S, HEAD_DIM), lambda g, qb: (g, 0, 0)),
                  pl.BlockSpec((1, nk, vt_rows, Tk), lambda g, qb: (g, 0, 0, 0))],
        out_specs=pl.BlockSpec((T, rep * HEAD_DIM), lambda g, qb: (qb, g)),
        out_shape=jax.ShapeDtypeStruct((S, D), BF16),
        scratch_shapes=[pltpu.VMEM((rep, SUBLANES, T), F32),
                        pltpu.VMEM((rep, vt_rows, T), F32),
                        pltpu.VMEM((rep, Tk, T), F32), pltpu.VMEM((rep, Tk, T), F32),
                        pltpu.VMEM((rep, SUBLANES, T), F32), pltpu.VMEM((rep, SUBLANES, T), F32)],
        compiler_params=_params("parallel", "arbitrary"), name="attn_d")(q, k, v_t)
    return matmul(o, w_out, F32)


def _lane_tiles(x):
    return [x[:, c * HEAD_DIM:(c + 1) * HEAD_DIM] for c in range(x.shape[1] // HEAD_DIM)]


def _win_attn_kernel(q_ref, kp_ref, kc_ref, kn_ref, vp_ref, vc_ref, vn_ref, bias_ref, o_ref,
                     *lse_ref, n_heads, tq, lead):
    lse_cols = []

    def scores(h):
        sl = slice(h * HEAD_DIM, (h + 1) * HEAD_DIM)
        q = q_ref[:, sl]
        ks = (kp_ref[:, sl], kc_ref[:, sl], kn_ref[:, sl])
        return [_dot_nt(q, ks[j]) + bias_ref[0, h, :, j * tq:(j + 1) * tq] for j in range(3)]

    def finish(h, ss):
        sl = slice(h * HEAD_DIM, (h + 1) * HEAD_DIM)
        vs = (vp_ref[:, sl], vc_ref[:, sl], vn_ref[:, sl])
        m = jnp.max(functools.reduce(jnp.maximum, [t for s in ss for t in _lane_tiles(s)]),
                    axis=-1, keepdims=True)
        ps = [jnp.exp(s - m) for s in ss]
        l = jnp.sum(functools.reduce(lambda a, b: a + b, [t for p in ps for t in _lane_tiles(p)]),
                    axis=-1, keepdims=True)
        acc = (jnp.dot(ps[0].astype(BF16), vs[0], preferred_element_type=F32)
               + jnp.dot(ps[1].astype(BF16), vs[1], preferred_element_type=F32)
               + jnp.dot(ps[2].astype(BF16), vs[2], preferred_element_type=F32))
        o_ref[:, sl] = (acc / l).astype(o_ref.dtype)
        lse_cols.append(m + jnp.log(l))

    pending = [scores(h) for h in range(min(lead, n_heads))]
    for h in range(n_heads):
        if h + lead < n_heads:
            pending.append(scores(h + lead))
        finish(h, pending.pop(0))
    if lse_ref:
        lse_ref[0][...] = jnp.concatenate(lse_cols, axis=1)


def _edge_variant(i, nblk):
    return jnp.where(i == 0, 0, jnp.where(i == nblk - 1, 2, 1))


def _merge_b_kernel(*refs, n_heads, dils):
    G = len(dils)
    o_refs, lse_ref, out_ref, scratch = refs[:G], refs[G], refs[G + 1], refs[G + 2:]
    rows = out_ref.shape[0]
    lse = lse_ref[...]
    w = jnp.exp(lse - jnp.max(lse, axis=0, keepdims=True))
    w = w / jnp.sum(w, axis=0, keepdims=True)
    strided = [g for g, dil in enumerate(dils) if dil > 1]
    for h in range(n_heads):
        sl = slice(h * HEAD_DIM, (h + 1) * HEAD_DIM)
        acc = None
        for g, (o_ref, dil) in enumerate(zip(o_refs, dils)):
            if dil == 1:
                o = o_ref[0, :, sl]
            else:
                sc = scratch[strided.index(g)]
                for r in range(dil):
                    sc[h, pl.ds(r, rows // dil, stride=dil), :] = o_ref[r, :, sl]
                o = sc[h]
            term = w[g, :, h:h + 1] * o
            acc = term if acc is None else acc + term
        out_ref[:, sl] = acc.astype(out_ref.dtype)


def mixer_b(h, w_in, w_out, rel_bias, tq=128, tm=256):
    S, D = h.shape
    G = len(B_GROUPS)
    n_heads = rel_bias.shape[1]
    width = n_heads * HEAD_DIM
    tn = width // 2
    outs, lses = [], []
    for g, (window, dil) in enumerate(B_GROUPS):
        half = window // 2 // dil
        assert half <= tq
        L = S // dil
        nblk = L // tq
        assert nblk >= 2

        def by_residue(a, dil=dil, L=L):
            return a if dil == 1 else a.reshape(L, dil, -1).transpose(1, 0, 2).reshape(S, -1)

        def by_position(a, dil=dil, L=L):
            return a if dil == 1 else a.reshape(dil, L, -1).transpose(1, 0, 2).reshape(S, -1)

        per = width // tn
        qkv = matmul(by_residue(h), w_in, BF16, tn=tn,
                     cols=(3 * per, lambda j, g=g, per=per: (j // per) * (G * per) + g * per + j % per))
        bias = t5_tiles(rel_bias, [-tq] * 3, tq, 3 * tq, dil=dil, half=half,
                        col_lo=(tq, 0, 0), col_hi=(3 * tq, 3 * tq, 2 * tq))

        def spec(part, di, nblk=nblk):
            return pl.BlockSpec(
                (tq, width), lambda r, i: (r * nblk + jnp.clip(i + di, 0, nblk - 1), part))

        o_g, lse_g = pl.pallas_call(
            functools.partial(_win_attn_kernel, n_heads=n_heads, tq=tq, lead=4), grid=(dil, nblk),
            in_specs=[spec(0, 0), spec(1, -1), spec(1, 0), spec(1, 1),
                      spec(2, -1), spec(2, 0), spec(2, 1),
                      pl.BlockSpec((1, n_heads, tq, 3 * tq),
                                   lambda r, i, nblk=nblk: (_edge_variant(i, nblk), 0, 0, 0))],
            out_specs=[pl.BlockSpec((tq, width), lambda r, i, nblk=nblk: (r * nblk + i, 0)),
                       pl.BlockSpec((tq, n_heads), lambda r, i, nblk=nblk: (r * nblk + i, 0))],
            out_shape=[jax.ShapeDtypeStruct((S, width), F32),
                       jax.ShapeDtypeStruct((S, n_heads), F32)],
            compiler_params=_params("parallel", "parallel"),
            name=f"attn_b{g}")(qkv, qkv, qkv, qkv, qkv, qkv, qkv, bias)
        outs.append(o_g.reshape(dil, L, width))
        lses.append(by_position(lse_g))
    dils = tuple(dil for _, dil in B_GROUPS)
    assert all(tm % (dil * SUBLANES) == 0 for dil in dils)
    o = pl.pallas_call(
        functools.partial(_merge_b_kernel, n_heads=n_heads, dils=dils), grid=(S // tm,),
        in_specs=[pl.BlockSpec((dil, tm // dil, width), lambda i: (0, i, 0)) for dil in dils]
        + [pl.BlockSpec((G, tm, n_heads), lambda i: (0, i, 0))],
        out_specs=pl.BlockSpec((tm, width), lambda i: (i, 0)),
        out_shape=jax.ShapeDtypeStruct((S, width), BF16),
        scratch_shapes=[pltpu.VMEM((n_heads, tm, HEAD_DIM), F32) for dil in dils if dil > 1],
        compiler_params=_params("parallel"), name="merge_b")(*outs, jnp.stack(lses, 0))
    return matmul(o, w_out, F32)


def _c_bias_kernel(rpb_ref, o_ref, *, rq, n_dr, n_dc):
    h = pl.program_id(0)
    W = GRID_W
    qc = lax.broadcasted_iota(jnp.int32, (W, 2 * W), 0)
    lane = lax.broadcasted_iota(jnp.int32, (W, 2 * W), 1)
    kc = lane & (W - 1)
    dc = kc - qc
    c0 = jnp.clip(qc - C_WIN_C // 2, 0, W - C_WIN_C)
    ok_c = (kc >= c0) & (kc < c0 + C_WIN_C)
    masked = jnp.full((W, 2 * W), MASKED, F32)
    base = h * (n_dr * n_dc)
    sub = []
    for a in range(n_dr):
        val = masked
        for b in range(n_dc):
            val = jnp.where(dc == b - (C_WIN_C - 1), rpb_ref[base + a * n_dc + b], val)
        sub.append(jnp.where(ok_c, val, MASKED))
    for v in range(3):
        for qr in range(rq):
            for jv in range(3 * rq // 2):
                halves = []
                for kr in (2 * jv, 2 * jv + 1):
                    lo, hi = ((rq, 3 * rq - 1), (qr, qr + C_WIN_R - 1), (0, C_WIN_R - 1))[v]
                    halves.append(sub[kr - qr + C_WIN_R - 1 - rq] if lo <= kr <= hi else masked)
                o_ref[v, 0, qr * W:(qr + 1) * W, jv * 2 * W:(jv + 1) * 2 * W] = jnp.where(
                    lane < W, halves[0], halves[1])


def mixer_c(h, w_in, w_out, rpb, heads_per_step=8):
    S, D = h.shape
    n_heads = D // HEAD_DIM
    rows = S // GRID_W
    rq = C_WIN_R // 2
    tq = rq * GRID_W
    nblk = rows // rq
    assert rows >= C_WIN_R and nblk >= 2 and 2 * GRID_W == HEAD_DIM
    n_dr, n_dc = 2 * C_WIN_R - 1, 2 * C_WIN_C - 1
    qkv = matmul(h, w_in, BF16)
    bias = pl.pallas_call(
        functools.partial(_c_bias_kernel, rq=rq, n_dr=n_dr, n_dc=n_dc), grid=(n_heads,),
        in_specs=[pl.BlockSpec(memory_space=pltpu.SMEM)],
        out_specs=pl.BlockSpec((3, 1, tq, 3 * tq), lambda hh: (0, hh, 0, 0)),
        out_shape=jax.ShapeDtypeStruct((3, n_heads, tq, 3 * tq), F32),
        compiler_params=_params("parallel"), name="c_bias")(rpb.reshape(-1))
    hs = heads_per_step
    n_hg = n_heads // hs
    width = hs * HEAD_DIM

    def spec(part, di):
        return pl.BlockSpec(
            (tq, width), lambda hg, i: (jnp.clip(i + di, 0, nblk - 1), part * n_hg + hg))

    o = pl.pallas_call(
        functools.partial(_win_attn_kernel, n_heads=hs, tq=tq, lead=2), grid=(n_hg, nblk),
        in_specs=[spec(0, 0), spec(1, -1), spec(1, 0), spec(1, 1),
                  spec(2, -1), spec(2, 0), spec(2, 1),
                  pl.BlockSpec((1, hs, tq, 3 * tq),
                               lambda hg, i: (_edge_variant(i, nblk), hg, 0, 0))],
        out_specs=pl.BlockSpec((tq, width), lambda hg, i: (i, hg)),
        out_shape=jax.ShapeDtypeStruct((S, D), BF16),
        compiler_params=_params("parallel", "parallel"),
        name="attn_c")(qkv, qkv, qkv, qkv, qkv, qkv, qkv, bias)
    return matmul(o, w_out, F32)


def kernel(x, rel_bias, norm_g, a_w_in, a_w_out, a_lambda, a_subln, b_w_in, b_w_out, c_w_in, c_w_out,
           c_rpb, d_w_in, d_w_out, d_qk_norm, ffn_w_gate, ffn_w_up, ffn_w_down):
    B, S, D = x.shape
    assert B == 1
    depth = norm_g.shape[0]
    xs = x.reshape(S, D)
    hn = prenorm(xs, norm_g[0, 0])

    def mixer_weights(layer):
        m, j = layer % N_MIXERS, layer // N_MIXERS
        w_in, w_out, q_cols, q_scale = (
            (a_w_in, a_w_out, D, SCALE * LOG2E), (b_w_in, b_w_out, b_w_in.shape[2] // 3, SCALE),
            (c_w_in, c_w_out, D, SCALE), (d_w_in, d_w_out, 0, None))[m]
        scale = None if q_scale is None else jnp.concatenate(
            [jnp.full((q_cols,), q_scale, F32), jnp.ones((w_in.shape[2] - q_cols,), F32)])
        return w_in, w_out, j, scale

    w_in, w_out, j, scale = mixer_weights(0)
    mix_w = ((w_in[j] if scale is None else w_in[j] * scale).astype(BF16), w_out[j].astype(BF16))
    ffn_w = (ffn_w_gate[0].astype(BF16), ffn_w_up[0].astype(BF16), ffn_w_down[0].astype(BF16))
    for i in range(depth):
        m, j = i % N_MIXERS, i // N_MIXERS
        if m == 0:
            y = mixer_a(hn, *mix_w, a_lambda[j], a_subln[j], rel_bias, i)
        elif m == 1:
            y = mixer_b(hn, *mix_w, rel_bias)
        elif m == 2:
            y = mixer_c(hn, *mix_w, c_rpb[j])
        else:
            y = mixer_d(hn, *mix_w, d_qk_norm[j])
        xs, hn = resid(xs, y, norm_g[i, 1], norm_g[i, 2])
        cast_next, riders = None, ()
        if i + 1 < depth:
            cast_next = (i + 1, ffn_w_gate, ffn_w_up, ffn_w_down)
            w_in, w_out, jn, scale = mixer_weights(i + 1)
            riders = ((w_in, jn, scale), (w_out, jn, None))
        y, ffn_w, mix_w = ffn(hn, *ffn_w, cast_next, riders)
        if i + 1 < depth:
            xs, hn = resid(xs, y, norm_g[i, 3], norm_g[i + 1, 0])
        else:
            xs = resid(xs, y, norm_g[i, 3])
    return xs.reshape(B, S, D)
```

```python
import functools
import math

import numpy as np
import jax
import jax.numpy as jnp
from jax import lax
from jax.experimental import pallas as pl
from jax.experimental.pallas import tpu as pltpu

F32 = jnp.float32
BF16 = jnp.bfloat16

HEAD_DIM = 128
SCALE = HEAD_DIM ** -0.5
LOG2E = math.log2(math.e)
GRID_W = 64
EPS = 1e-6
REL_BUCKETS = 32
REL_MAX_DIST = 1024
B_GROUPS = ((128, 1), (512, 4), (2048, 16))
C_WIN_R = 8
C_WIN_C = 16
D_KV_HEADS = 8
ROPE_THETA = 10000.0
N_MIXERS = 4

KEY_CHUNK = 256
SUBLANES = 8
ONES_ROWS = 2 * SUBLANES
MASKED = -1e30
VMEM_LIMIT_BYTES = 56 * 2 ** 20


def _params(*semantics):
    return pltpu.CompilerParams(dimension_semantics=semantics, vmem_limit_bytes=VMEM_LIMIT_BYTES)


def _rms(x, g):
    ms = jnp.mean(x * x, axis=-1, keepdims=True)
    return x * lax.rsqrt(ms + EPS) * g


def _dot_nt(a, b):
    return lax.dot_general(a, b, (((1,), (1,)), ((), ())), preferred_element_type=F32)


def _prenorm_kernel(x_ref, g_ref, h_ref):
    h_ref[...] = _rms(x_ref[...], g_ref[...]).astype(h_ref.dtype)


def prenorm(x, g, tm=256):
    S, D = x.shape
    row = pl.BlockSpec((tm, D), lambda i: (i, 0))
    vec = pl.BlockSpec((1, D), lambda i: (0, 0))
    return pl.pallas_call(
        _prenorm_kernel, grid=(S // tm,), in_specs=[row, vec], out_specs=row,
        out_shape=jax.ShapeDtypeStruct((S, D), BF16), compiler_params=_params("parallel"),
        name="prenorm")(x, g.reshape(1, D))


def _resid_kernel(x_ref, y_ref, g1_ref, g2_ref, xo_ref, h_ref, *rest, residues):
    x = x_ref[...] + _rms(y_ref[...], g1_ref[...])
    xo_ref[...] = x
    h = _rms(x, g2_ref[...])
    h_ref[...] = h.astype(h_ref.dtype)
    if residues:
        sc = rest[-1]
        rows = h.shape[0]
        for c in range(sc.shape[0]):
            sc[c] = h[:, c * HEAD_DIM:(c + 1) * HEAD_DIM]
        for o_ref, dil in zip(rest, residues):
            for r in range(dil):
                for c in range(sc.shape[0]):
                    o_ref[r, :, c * HEAD_DIM:(c + 1) * HEAD_DIM] = (
                        sc[c, pl.ds(r, rows // dil, stride=dil), :].astype(o_ref.dtype))


def _resid_last_kernel(x_ref, y_ref, g1_ref, xo_ref):
    xo_ref[...] = x_ref[...] + _rms(y_ref[...], g1_ref[...])


def resid(x, y, g_post, g_next=None, tm=256, residues=()):
    S, D = x.shape
    row = pl.BlockSpec((tm, D), lambda i: (i, 0))
    vec = pl.BlockSpec((1, D), lambda i: (0, 0))
    if g_next is None:
        return pl.pallas_call(
            _resid_last_kernel, grid=(S // tm,), in_specs=[row, row, vec], out_specs=row,
            out_shape=jax.ShapeDtypeStruct((S, D), F32), compiler_params=_params("parallel"),
            name="resid_last")(x, y, g_post.reshape(1, D))
    assert all(tm % (2 * SUBLANES * dil) == 0 for dil in residues)
    res = pl.pallas_call(
        functools.partial(_resid_kernel, residues=tuple(residues)), grid=(S // tm,),
        in_specs=[row, row, vec, vec],
        out_specs=[row, row] + [pl.BlockSpec((dil, tm // dil, D), lambda i: (0, i, 0)) for dil in residues],
        out_shape=[jax.ShapeDtypeStruct((S, D), F32), jax.ShapeDtypeStruct((S, D), BF16)]
        + [jax.ShapeDtypeStruct((dil, S // dil, D), BF16) for dil in residues],
        scratch_shapes=[pltpu.VMEM((D // HEAD_DIM, tm, HEAD_DIM), F32)] if residues else [],
        compiler_params=_params("parallel"),
        name="resid")(x, y, g_post.reshape(1, D), g_next.reshape(1, D))
    return res[0], res[1], {dil: r.reshape(S, D) for dil, r in zip(residues, res[2:])}


def _mm_kernel(a_ref, b_ref, o_ref):
    o_ref[...] = jnp.dot(a_ref[...], b_ref[...], preferred_element_type=F32).astype(o_ref.dtype)


def _mm_acc_kernel(a_ref, b_ref, *rest, scaled):
    n_in = len(scaled) + sum(scaled)
    o_ref, acc_ref = rest[n_in], rest[-1]
    k = pl.program_id(2)

    @pl.when(k == 0)
    def _():
        acc_ref[...] = jnp.zeros_like(acc_ref)

    acc_ref[...] += jnp.dot(a_ref[...], b_ref[...], preferred_element_type=F32)

    @pl.when(k == pl.num_programs(2) - 1)
    def _():
        o_ref[...] = acc_ref[...].astype(o_ref.dtype)

    srcs = list(rest[:n_in])
    for has_scale, dst in zip(scaled, rest[n_in + 1:-1]):
        w = srcs.pop(0)[...]
        if has_scale:
            w = w * srcs.pop(0)[...]
        dst[...] = w.astype(dst.dtype)


def matmul(a, b, out_dtype, tm=1024, tn=None, tk=None, cols=None, riders=()):
    M, K = a.shape
    tk = K if tk is None else tk
    if tn is None:
        tn = 1024 if b.shape[1] % 1024 == 0 else 512
    n_tiles, col = (b.shape[1] // tn, lambda j: j) if cols is None else cols
    out_shape = jax.ShapeDtypeStruct((M, n_tiles * tn), out_dtype)
    if tk == K:
        assert not riders
        return pl.pallas_call(
            _mm_kernel, grid=(M // tm, n_tiles),
            in_specs=[pl.BlockSpec((tm, K), lambda i, j: (i, 0)),
                      pl.BlockSpec((K, tn), lambda i, j: (0, col(j)))],
            out_specs=pl.BlockSpec((tm, tn), lambda i, j: (i, j)),
            out_shape=out_shape, compiler_params=_params("parallel", "parallel"),
            name="matmul")(a, b)
    gi, gj, gk = M // tm, n_tiles, K // tk
    in_specs = [pl.BlockSpec((tm, tk), lambda i, j, k: (i, k)),
                pl.BlockSpec((tk, tn), lambda i, j, k: (k, col(j)))]
    out_specs = [pl.BlockSpec((tm, tn), lambda i, j, k: (i, j))]
    out_shapes = [out_shape]
    operands = [a, b]
    for w, layer, scale in riders:
        R, C = w.shape[1:]
        rb, cb = R // gi, C // (gj * gk)
        assert rb * gi == R and cb * gj * gk == C and rb % SUBLANES == 0 and cb % HEAD_DIM == 0
        in_specs.append(pl.BlockSpec((None, rb, cb), lambda i, j, k, layer=layer: (layer, i, j * gk + k)))
        operands.append(w)
        if scale is not None:
            in_specs.append(pl.BlockSpec((1, cb), lambda i, j, k: (0, j * gk + k)))
            operands.append(scale.reshape(1, C))
        out_specs.append(pl.BlockSpec((rb, cb), lambda i, j, k: (i, j * gk + k)))
        out_shapes.append(jax.ShapeDtypeStruct((R, C), BF16))
    res = pl.pallas_call(
        functools.partial(_mm_acc_kernel, scaled=tuple(s is not None for _, _, s in riders)),
        grid=(gi, gj, gk), in_specs=in_specs, out_specs=out_specs, out_shape=out_shapes,
        scratch_shapes=[pltpu.VMEM((tm, tn), F32)],
        compiler_params=_params("parallel", "parallel", "arbitrary"),
        name="matmul_acc")(*operands)
    return (res[0], tuple(res[1:])) if riders else res[0]


def _glu_kernel(a_ref, wg_ref, wu_ref, *rest):
    o_ref = rest[len(rest) // 2]
    a = a_ref[...]
    g = jnp.dot(a, wg_ref[...], preferred_element_type=F32)
    u = jnp.dot(a, wu_ref[...], preferred_element_type=F32)
    o_ref[...] = (g * (1.0 / (1.0 + jnp.exp(-g))) * u).astype(o_ref.dtype)
    n_cast = len(rest) // 2
    for src, dst in zip(rest[:n_cast], rest[n_cast + 1:]):
        dst[...] = src[...].astype(dst.dtype)


def glu_up(a, wg, wu, cast_next=None, tm=2048, tn=256):
    M, K = a.shape
    N = wg.shape[1]
    gi, gj = M // tm, N // tn
    wspec = pl.BlockSpec((K, tn), lambda i, j: (0, j))
    in_specs = [pl.BlockSpec((tm, K), lambda i, j: (i, 0)), wspec, wspec]
    out_specs = [pl.BlockSpec((tm, tn), lambda i, j: (i, j))]
    out_shape = [jax.ShapeDtypeStruct((M, N), BF16)]
    extra = []
    if cast_next is not None:
        layer, ng, nu, nd = cast_next
        D = nd.shape[2]
        up_spec = pl.BlockSpec((None, K // gi, tn), lambda i, j: (layer, i, j))
        down_spec = pl.BlockSpec((None, tn, D // gi), lambda i, j: (layer, j, i))
        in_specs += [up_spec, up_spec, down_spec]
        out_specs += [pl.BlockSpec((K // gi, tn), lambda i, j: (i, j)),
                      pl.BlockSpec((K // gi, tn), lambda i, j: (i, j)),
                      pl.BlockSpec((tn, D // gi), lambda i, j: (j, i))]
        out_shape += [jax.ShapeDtypeStruct((K, N), BF16), jax.ShapeDtypeStruct((K, N), BF16),
                      jax.ShapeDtypeStruct((N, D), BF16)]
        extra = [ng, nu, nd]
    res = pl.pallas_call(
        _glu_kernel, grid=(gi, gj), in_specs=in_specs, out_specs=out_specs, out_shape=out_shape,
        compiler_params=_params("parallel", "parallel"), name="glu_up")(a, wg, wu, *extra)
    return res[0], tuple(res[1:])


def ffn(h, w_gate, w_up, w_down, cast_next=None, riders=()):
    hid, nxt = glu_up(h, w_gate, w_up, cast_next)
    d_ff = hid.shape[1]
    tk = d_ff // 2 if (d_ff // 2) % HEAD_DIM == 0 else d_ff
    y = matmul(hid, w_down, F32, tm=1024, tn=512, tk=tk, riders=riders)
    return (y[0], nxt, y[1]) if riders else (y, nxt, ())


def _t5_thresholds():
    nb = REL_BUCKETS // 2
    max_exact = nb // 2
    n = np.arange(0, 4 * REL_MAX_DIST)
    nf = np.maximum(n, 1).astype(np.float32)
    large = max_exact + (np.log(nf / np.float32(max_exact))
                         / np.float32(math.log(REL_MAX_DIST / max_exact))
                         * np.float32(nb - max_exact)).astype(np.int32)
    bucket = np.where(n < max_exact, n, np.minimum(large, nb - 1))
    assert np.all(np.diff(bucket) >= 0) and bucket[-1] == nb - 1
    return tuple(int(np.argmax(bucket >= b)) for b in range(1, nb))


T5_THRESHOLDS = _t5_thresholds()


def _t5_tile_kernel(rb_ref, var_ref, o_ref, *, dil, half, n_heads, mult, keys_on_rows):
    h = pl.program_id(0)
    v = pl.program_id(1)
    off, col_lo, col_hi = var_ref[3 * v], var_ref[3 * v + 1], var_ref[3 * v + 2]
    rows, cols = o_ref.shape[2], o_ref.shape[3]
    i = lax.broadcasted_iota(jnp.int32, (rows, cols), 0)
    j = lax.broadcasted_iota(jnp.int32, (rows, cols), 1)
    nb = REL_BUCKETS // 2
    delta = (i - j if keys_on_rows else j - i) + off
    rel = delta * dil
    n = jnp.abs(rel)
    vneg = jnp.full((rows, cols), rb_ref[h], F32)
    vpos = jnp.full((rows, cols), rb_ref[nb * n_heads + h], F32)
    for b, t in enumerate(T5_THRESHOLDS, start=1):
        c = n >= t
        vneg = jnp.where(c, rb_ref[b * n_heads + h], vneg)
        vpos = jnp.where(c, rb_ref[(nb + b) * n_heads + h], vpos)
    val = jnp.where(rel > 0, vpos, vneg)
    if mult != 1.0:
        val = val * mult
    if half is not None:
        ok = (jnp.abs(delta) <= half) & (j >= col_lo) & (j < col_hi)
        val = jnp.where(ok, val, MASKED)
    o_ref[0, 0] = val


def t5_tiles(rel_bias, offs, rows, cols, dil=1, half=None, col_lo=None, col_hi=None, mult=1.0,
             keys_on_rows=False):
    n_heads = rel_bias.shape[1]
    nv = len(offs)
    col_lo = (0,) * nv if col_lo is None else col_lo
    col_hi = (cols,) * nv if col_hi is None else col_hi
    variants = jnp.asarray(np.stack([offs, col_lo, col_hi], axis=1).reshape(-1), jnp.int32)
    kern = functools.partial(_t5_tile_kernel, dil=dil, half=half, n_heads=n_heads, mult=mult,
                             keys_on_rows=keys_on_rows)
    smem = pl.BlockSpec(memory_space=pltpu.SMEM)
    return pl.pallas_call(
        kern, grid=(n_heads, nv), in_specs=[smem, smem],
        out_specs=pl.BlockSpec((1, 1, rows, cols), lambda h, v: (v, h, 0, 0)),
        out_shape=jax.ShapeDtypeStruct((nv, n_heads, rows, cols), F32),
        compiler_params=_params("parallel", "parallel"), name="t5_tiles")(rel_bias.reshape(-1), variants)


def _sublane_groups(x):
    return x.reshape(x.shape[0] // SUBLANES, SUBLANES, x.shape[1])


def _store_scores(s_sc, mp_sc, idx, s):
    s_sc[idx] = s
    mp_sc[idx] = jnp.max(_sublane_groups(s), axis=0)


def _online_step(m_sc, acc_sc, idx, s_sc, mp_sc, vt_chunk):
    Tk, Tq = s_sc.shape[1:]
    m_old = m_sc[idx]
    m_cur = jnp.max(mp_sc[idx], axis=0, keepdims=True)
    m_new = jnp.maximum(m_old, jnp.broadcast_to(m_cur, m_old.shape))
    m_sc[idx] = m_new
    alpha = jnp.exp2(m_old - m_new)
    pv = None
    for c in range(Tk // KEY_CHUNK):
        ks = slice(c * KEY_CHUNK, (c + 1) * KEY_CHUNK)
        p = jnp.exp2(_sublane_groups(s_sc[idx, ks, :]) - m_new[None])
        d = jnp.dot(vt_chunk(ks), p.reshape(KEY_CHUNK, Tq).astype(BF16), preferred_element_type=F32)
        pv = d if pv is None else pv + d
    acc_sc[idx] = (_sublane_groups(acc_sc[idx]) * alpha[None]).reshape(acc_sc.shape[1:]) + pv


def _flash_init(m_sc, acc_sc):
    m_sc[...] = jnp.full(m_sc.shape, -jnp.inf, F32)
    acc_sc[...] = jnp.zeros(acc_sc.shape, F32)


def _values_transposed(v, n_heads, Tk):
    S = v.shape[0]
    dv = v.shape[1] // n_heads
    v_t = v.reshape(S // Tk, Tk, n_heads, dv).transpose(2, 0, 3, 1)
    return jnp.concatenate([v_t, jnp.ones((n_heads, S // Tk, ONES_ROWS, Tk), v.dtype)], axis=2)


def _pipelined_blocks(nk, parts, scores, consume, buf0, buf1):
    assert nk % 2 == 0
    for part in parts:
        scores(0, *buf0, part)

    def step(kb, cur, nxt):
        for part in parts:
            scores(kb + 1, *nxt, part)
            consume(kb, *cur, part)

    def body(j, carry):
        step(2 * j, buf0, buf1)
        step(2 * j + 1, buf1, buf0)
        return carry

    lax.fori_loop(0, nk // 2 - 1, body, 0)
    step(nk - 2, buf0, buf1)
    for part in parts:
        consume(nk - 1, *buf1, part)


def _attn_a_kernel(q_ref, k_ref, vt_ref, band_ref, lam_ref, g_ref, o_ref, m_sc, acc_sc,
                   s0_sc, s1_sc, mp0_sc, mp1_sc, *, Tk, nk, q_step, k_step, e_lo, e_hi, lam_init):
    qb = pl.program_id(1)
    dv = 2 * HEAD_DIM
    _flash_init(m_sc, acc_sc)

    def scores(kb, s_sc, mp_sc, part):
        off = pl.multiple_of(kb * Tk, Tk)
        bidx = jnp.clip(kb * k_step - qb * q_step, e_lo, e_hi) - e_lo
        for m in part:
            sl = slice(m * HEAD_DIM, (m + 1) * HEAD_DIM)
            _store_scores(s_sc, mp_sc, m,
                          _dot_nt(k_ref[0, pl.ds(off, Tk), sl], q_ref[:, sl]) + band_ref[bidx, 0])

    def consume(kb, s_sc, mp_sc, part):
        for m in part:
            _online_step(m_sc, acc_sc, m, s_sc, mp_sc, lambda ks: vt_ref[0, kb, :, ks])

    _pipelined_blocks(nk, ((0, 1),), scores, consume, (s0_sc, mp0_sc), (s1_sc, mp1_sc))
    lp = lam_ref[...]
    lam = (jnp.exp(jnp.sum(lp[0:1] * lp[1:2], axis=-1, keepdims=True))
           - jnp.exp(jnp.sum(lp[2:3] * lp[3:4], axis=-1, keepdims=True)) + lam_init)
    acc0, acc1 = acc_sc[0], acc_sc[1]
    o_t = acc0[:dv] / acc0[dv:dv + 1] - lam * (acc1[:dv] / acc1[dv:dv + 1])
    o_ref[...] = (_rms(o_t.T, g_ref[...]) * (1.0 - lam_init)).astype(o_ref.dtype)


def mixer_a(h, w_in, w_out, lam_p, subln_g, rel_bias, layer_idx, T=512, Tk=1024):
    S, D = h.shape
    n_heads = D // (2 * HEAD_DIM)
    width = 2 * HEAD_DIM
    qkv = matmul(h, w_in, BF16)
    far = T5_THRESHOLDS[-1]
    unit = math.gcd(T, Tk)
    e_hi = -(-(far + T - 1) // unit)
    e_lo = -(-(far + Tk - 1) // unit)
    e_lo = -e_lo
    tiles = t5_tiles(rel_bias, [e * unit for e in range(e_lo, e_hi + 1)], Tk, T, mult=LOG2E,
                     keys_on_rows=True)
    n_tiles = e_hi - e_lo + 1
    nk = S // Tk
    v_t = _values_transposed(qkv[:, 2 * D:], n_heads, Tk)
    k_hm = qkv[:, D:2 * D].reshape(S, n_heads, width).transpose(1, 0, 2)
    lam_init = 0.8 - 0.6 * math.exp(-0.3 * layer_idx)
    kern = functools.partial(_attn_a_kernel, Tk=Tk, nk=nk, q_step=T // unit, k_step=Tk // unit,
                             e_lo=e_lo, e_hi=e_hi, lam_init=lam_init)
    once = pl.Buffered(1)
    o = pl.pallas_call(
        kern, grid=(n_heads, S // T),
        in_specs=[pl.BlockSpec((T, width), lambda hh, qb: (qb, hh)),
                  pl.BlockSpec((1, S, width), lambda hh, qb: (hh, 0, 0), pipeline_mode=once),
                  pl.BlockSpec((1, nk, width + ONES_ROWS, Tk), lambda hh, qb: (hh, 0, 0, 0),
                               pipeline_mode=once),
                  pl.BlockSpec((n_tiles, 1, Tk, T), lambda hh, qb: (0, hh, 0, 0), pipeline_mode=once),
                  pl.BlockSpec((4, HEAD_DIM), lambda hh, qb: (0, 0)),
                  pl.BlockSpec((1, width), lambda hh, qb: (0, 0))],
        out_specs=pl.BlockSpec((T, width), lambda hh, qb: (qb, hh)),
        out_shape=jax.ShapeDtypeStruct((S, D), BF16),
        scratch_shapes=[pltpu.VMEM((2, SUBLANES, T), F32), pltpu.VMEM((2, width + ONES_ROWS, T), F32),
                        pltpu.VMEM((2, Tk, T), F32), pltpu.VMEM((2, Tk, T), F32),
                        pltpu.VMEM((2, SUBLANES, T), F32), pltpu.VMEM((2, SUBLANES, T), F32)],
        compiler_params=_params("parallel", "arbitrary"),
        name="attn_a")(qkv, k_hm, v_t, tiles, lam_p, subln_g.reshape(1, width))
    return matmul(o, w_out, F32)


def _rope_kernel(x_ref, c_ref, sa_ref, sb_ref, g_ref, oq_ref, ok_ref, *, n_q, n_chunks):
    c, sa, sb = c_ref[...], sa_ref[...], sb_ref[...]
    for ch in range(n_chunks):
        sl = slice(ch * HEAD_DIM, (ch + 1) * HEAD_DIM)
        y = _rms(x_ref[:, sl], g_ref[0:1] if ch < n_q else g_ref[1:2])
        y = y * c + pltpu.roll(y, HEAD_DIM - 1, 1) * sa + pltpu.roll(y, 1, 1) * sb
        if ch < n_q:
            oq_ref[:, sl] = (y * (SCALE * LOG2E)).astype(oq_ref.dtype)
        else:
            ok_ref[ch - n_q] = y.astype(ok_ref.dtype)


def _rope_tables(S):
    pos = jnp.arange(S)
    n_freq = HEAD_DIM // 4
    freqs = ROPE_THETA ** (-jnp.arange(n_freq, dtype=F32) / n_freq)
    ang = jnp.concatenate([(pos // GRID_W).astype(F32)[:, None] * freqs,
                           (pos % GRID_W).astype(F32)[:, None] * freqs], axis=-1)
    cos, sin = jnp.cos(ang), jnp.sin(ang)
    zero = jnp.zeros_like(sin)
    c = jnp.stack([cos, cos], axis=-1).reshape(S, HEAD_DIM)
    sa = jnp.stack([-sin, zero], axis=-1).reshape(S, HEAD_DIM)
    sb = jnp.stack([zero, sin], axis=-1).reshape(S, HEAD_DIM)
    return c, sa, sb


def _attn_d_kernel(q_ref, k_ref, vt_ref, o_ref, m_sc, acc_sc, s0_sc, s1_sc, mp0_sc, mp1_sc,
                   *, Tk, nk, rep):
    _flash_init(m_sc, acc_sc)

    def scores(kb, s_sc, mp_sc, part):
        kblk = k_ref[0, pl.ds(pl.multiple_of(kb * Tk, Tk), Tk), :]
        for r in part:
            _store_scores(s_sc, mp_sc, r, _dot_nt(kblk, q_ref[:, r * HEAD_DIM:(r + 1) * HEAD_DIM]))

    def consume(kb, s_sc, mp_sc, part):
        for r in part:
            _online_step(m_sc, acc_sc, r, s_sc, mp_sc, lambda ks: vt_ref[0, kb, :, ks])

    parts = tuple(tuple(range(r, min(r + 2, rep))) for r in range(0, rep, 2))
    _pipelined_blocks(nk, parts, scores, consume, (s0_sc, mp0_sc), (s1_sc, mp1_sc))
    for r in range(rep):
        acc = acc_sc[r]
        o_t = acc[:HEAD_DIM] / acc[HEAD_DIM:HEAD_DIM + 1]
        o_ref[:, r * HEAD_DIM:(r + 1) * HEAD_DIM] = o_t.T.astype(o_ref.dtype)


def mixer_d(h, w_in, w_out, qk_g, T=512, Tk=1024, tm=256):
    S, D = h.shape
    n_q = D // HEAD_DIM
    n_kv = D_KV_HEADS
    rep = n_q // n_kv
    nqk = (n_q + n_kv) * HEAD_DIM
    tn = n_kv * HEAD_DIM
    assert nqk % tn == 0 and w_in.shape[1] == nqk + tn
    qk = matmul(h, w_in, F32, tn=tn, cols=(nqk // tn, lambda j: j))
    v = matmul(h, w_in, BF16, tn=tn, cols=(1, lambda j: j + nqk // tn))
    c, sa, sb = _rope_tables(S)
    tab = pl.BlockSpec((tm, HEAD_DIM), lambda i: (i, 0))
    q, k = pl.pallas_call(
        functools.partial(_rope_kernel, n_q=n_q, n_chunks=n_q + n_kv), grid=(S // tm,),
        in_specs=[pl.BlockSpec((tm, nqk), lambda i: (i, 0)), tab, tab, tab,
                  pl.BlockSpec((2, HEAD_DIM), lambda i: (0, 0))],
        out_specs=[pl.BlockSpec((tm, D), lambda i: (i, 0)),
                   pl.BlockSpec((n_kv, tm, HEAD_DIM), lambda i: (0, i, 0))],
        out_shape=[jax.ShapeDtypeStruct((S, D), BF16),
                   jax.ShapeDtypeStruct((n_kv, S, HEAD_DIM), BF16)],
        compiler_params=_params("parallel"), name="qk_norm_rope")(qk, c, sa, sb, qk_g)
    nk = S // Tk
    v_t = _values_transposed(v, n_kv, Tk)
    vt_rows = HEAD_DIM + ONES_ROWS
    kern = functools.partial(_attn_d_kernel, Tk=Tk, nk=nk, rep=rep)
    o = pl.pallas_call(
        kern, grid=(n_kv, S // T),
        in_specs=[pl.BlockSpec((T, rep * HEAD_DIM), lambda g, qb: (qb, g)),
                  pl.BlockSpec((1, S, HEAD_DIM), lambda g, qb: (g, 0, 0)),
                  pl.BlockSpec((1, nk, vt_rows, Tk), lambda g, qb: (g, 0, 0, 0))],
        out_specs=pl.BlockSpec((T, rep * HEAD_DIM), lambda g, qb: (qb, g)),
        out_shape=jax.ShapeDtypeStruct((S, D), BF16),
        scratch_shapes=[pltpu.VMEM((rep, SUBLANES, T), F32),
                        pltpu.VMEM((rep, vt_rows, T), F32),
                        pltpu.VMEM((rep, Tk, T), F32), pltpu.VMEM((rep, Tk, T), F32),
                        pltpu.VMEM((rep, SUBLANES, T), F32), pltpu.VMEM((rep, SUBLANES, T), F32)],
        compiler_params=_params("parallel", "arbitrary"), name="attn_d")(q, k, v_t)
    return matmul(o, w_out, F32)


def _lane_tiles(x):
    return [x[:, c * HEAD_DIM:(c + 1) * HEAD_DIM] for c in range(x.shape[1] // HEAD_DIM)]


def _win_attn_kernel(q_ref, kp_ref, kc_ref, kn_ref, vp_ref, vc_ref, vn_ref, bias_ref, o_ref,
                     *lse_ref, n_heads, tq, lead):
    lse_cols = []

    def scores(h):
        sl = slice(h * HEAD_DIM, (h + 1) * HEAD_DIM)
        q = q_ref[:, sl]
        ks = (kp_ref[:, sl], kc_ref[:, sl], kn_ref[:, sl])
        return [_dot_nt(q, ks[j]) + bias_ref[0, h, :, j * tq:(j + 1) * tq] for j in range(3)]

    def finish(h, ss):
        sl = slice(h * HEAD_DIM, (h + 1) * HEAD_DIM)
        vs = (vp_ref[:, sl], vc_ref[:, sl], vn_ref[:, sl])
        m = jnp.max(functools.reduce(jnp.maximum, [t for s in ss for t in _lane_tiles(s)]),
                    axis=-1, keepdims=True)
        ps = [jnp.exp(s - m) for s in ss]
        l = jnp.sum(functools.reduce(lambda a, b: a + b, [t for p in ps for t in _lane_tiles(p)]),
                    axis=-1, keepdims=True)
        acc = (jnp.dot(ps[0].astype(BF16), vs[0], preferred_element_type=F32)
               + jnp.dot(ps[1].astype(BF16), vs[1], preferred_element_type=F32)
               + jnp.dot(ps[2].astype(BF16), vs[2], preferred_element_type=F32))
        o_ref[:, sl] = (acc / l).astype(o_ref.dtype)
        lse_cols.append(m + jnp.log(l))

    pending = [scores(h) for h in range(min(lead, n_heads))]
    for h in range(n_heads):
        if h + lead < n_heads:
            pending.append(scores(h + lead))
        finish(h, pending.pop(0))
    if lse_ref:
        lse_ref[0][...] = jnp.concatenate(lse_cols, axis=1)


def _edge_variant(i, nblk):
    return jnp.where(i == 0, 0, jnp.where(i == nblk - 1, 2, 1))


def _merge_b_kernel(*refs, n_heads, dils):
    G = len(dils)
    o_refs, lse_ref, out_ref, scratch = refs[:G], refs[G], refs[G + 1], refs[G + 2:]
    rows = out_ref.shape[0]
    lse = lse_ref[...]
    w = jnp.exp(lse - jnp.max(lse, axis=0, keepdims=True))
    w = w / jnp.sum(w, axis=0, keepdims=True)
    strided = [g for g, dil in enumerate(dils) if dil > 1]
    for h in range(n_heads):
        sl = slice(h * HEAD_DIM, (h + 1) * HEAD_DIM)
        acc = None
        for g, (o_ref, dil) in enumerate(zip(o_refs, dils)):
            if dil == 1:
                o = o_ref[0, :, sl]
            else:
                sc = scratch[strided.index(g)]
                for r in range(dil):
                    sc[h, pl.ds(r, rows // dil, stride=dil), :] = o_ref[r, :, sl]
                o = sc[h]
            term = w[g, :, h:h + 1] * o
            acc = term if acc is None else acc + term
        out_ref[:, sl] = acc.astype(out_ref.dtype)


def mixer_b(h, w_in, w_out, rel_bias, h_by_residue=None, tq=128, tm=256):
    S, D = h.shape
    G = len(B_GROUPS)
    n_heads = rel_bias.shape[1]
    width = n_heads * HEAD_DIM
    tn = width // 2
    outs, lses = [], []
    for g, (window, dil) in enumerate(B_GROUPS):
        half = window // 2 // dil
        assert half <= tq
        L = S // dil
        nblk = L // tq
        assert nblk >= 2

        def by_residue(a, dil=dil, L=L):
            return a if dil == 1 else a.reshape(L, dil, -1).transpose(1, 0, 2).reshape(S, -1)

        def by_position(a, dil=dil, L=L):
            return a if dil == 1 else a.reshape(dil, L, -1).transpose(1, 0, 2).reshape(S, -1)

        per = width // tn
        h_g = h_by_residue[dil] if h_by_residue and dil in h_by_residue else by_residue(h)
        qkv = matmul(h_g, w_in, BF16, tn=tn,
                     cols=(3 * per, lambda j, g=g, per=per: (j // per) * (G * per) + g * per + j % per))
        bias = t5_tiles(rel_bias, [-tq] * 3, tq, 3 * tq, dil=dil, half=half,
                        col_lo=(tq, 0, 0), col_hi=(3 * tq, 3 * tq, 2 * tq))

        def spec(part, di, nblk=nblk):
            return pl.BlockSpec(
                (tq, width), lambda r, i: (r * nblk + jnp.clip(i + di, 0, nblk - 1), part))

        o_g, lse_g = pl.pallas_call(
            functools.partial(_win_attn_kernel, n_heads=n_heads, tq=tq, lead=4), grid=(dil, nblk),
            in_specs=[spec(0, 0), spec(1, -1), spec(1, 0), spec(1, 1),
                      spec(2, -1), spec(2, 0), spec(2, 1),
                      pl.BlockSpec((1, n_heads, tq, 3 * tq),
                                   lambda r, i, nblk=nblk: (_edge_variant(i, nblk), 0, 0, 0))],
            out_specs=[pl.BlockSpec((tq, width), lambda r, i, nblk=nblk: (r * nblk + i, 0)),
                       pl.BlockSpec((tq, n_heads), lambda r, i, nblk=nblk: (r * nblk + i, 0))],
            out_shape=[jax.ShapeDtypeStruct((S, width), F32),
                       jax.ShapeDtypeStruct((S, n_heads), F32)],
            compiler_params=_params("parallel", "parallel"),
            name=f"attn_b{g}")(qkv, qkv, qkv, qkv, qkv, qkv, qkv, bias)
        outs.append(o_g.reshape(dil, L, width))
        lses.append(by_position(lse_g))
    dils = tuple(dil for _, dil in B_GROUPS)
    assert all(tm % (dil * SUBLANES) == 0 for dil in dils)
    o = pl.pallas_call(
        functools.partial(_merge_b_kernel, n_heads=n_heads, dils=dils), grid=(S // tm,),
        in_specs=[pl.BlockSpec((dil, tm // dil, width), lambda i: (0, i, 0)) for dil in dils]
        + [pl.BlockSpec((G, tm, n_heads), lambda i: (0, i, 0))],
        out_specs=pl.BlockSpec((tm, width), lambda i: (i, 0)),
        out_shape=jax.ShapeDtypeStruct((S, width), BF16),
        scratch_shapes=[pltpu.VMEM((n_heads, tm, HEAD_DIM), F32) for dil in dils if dil > 1],
        compiler_params=_params("parallel"), name="merge_b")(*outs, jnp.stack(lses, 0))
    return matmul(o, w_out, F32)


def _c_bias_kernel(rpb_ref, o_ref, *, rq, n_dr, n_dc):
    h = pl.program_id(0)
    W = GRID_W
    qc = lax.broadcasted_iota(jnp.int32, (W, 2 * W), 0)
    lane = lax.broadcasted_iota(jnp.int32, (W, 2 * W), 1)
    kc = lane & (W - 1)
    dc = kc - qc
    c0 = jnp.clip(qc - C_WIN_C // 2, 0, W - C_WIN_C)
    ok_c = (kc >= c0) & (kc < c0 + C_WIN_C)
    masked = jnp.full((W, 2 * W), MASKED, F32)
    base = h * (n_dr * n_dc)
    sub = []
    for a in range(n_dr):
        val = masked
        for b in range(n_dc):
            val = jnp.where(dc == b - (C_WIN_C - 1), rpb_ref[base + a * n_dc + b], val)
        sub.append(jnp.where(ok_c, val, MASKED))
    for v in range(3):
        for qr in range(rq):
            for jv in range(3 * rq // 2):
                halves = []
                for kr in (2 * jv, 2 * jv + 1):
                    lo, hi = ((rq, 3 * rq - 1), (qr, qr + C_WIN_R - 1), (0, C_WIN_R - 1))[v]
                    halves.append(sub[kr - qr + C_WIN_R - 1 - rq] if lo <= kr <= hi else masked)
                o_ref[v, 0, qr * W:(qr + 1) * W, jv * 2 * W:(jv + 1) * 2 * W] = jnp.where(
                    lane < W, halves[0], halves[1])


def mixer_c(h, w_in, w_out, rpb, heads_per_step=8):
    S, D = h.shape
    n_heads = D // HEAD_DIM
    rows = S // GRID_W
    rq = C_WIN_R // 2
    tq = rq * GRID_W
    nblk = rows // rq
    assert rows >= C_WIN_R and nblk >= 2 and 2 * GRID_W == HEAD_DIM
    n_dr, n_dc = 2 * C_WIN_R - 1, 2 * C_WIN_C - 1
    qkv = matmul(h, w_in, BF16)
    bias = pl.pallas_call(
        functools.partial(_c_bias_kernel, rq=rq, n_dr=n_dr, n_dc=n_dc), grid=(n_heads,),
        in_specs=[pl.BlockSpec(memory_space=pltpu.SMEM)],
        out_specs=pl.BlockSpec((3, 1, tq, 3 * tq), lambda hh: (0, hh, 0, 0)),
        out_shape=jax.ShapeDtypeStruct((3, n_heads, tq, 3 * tq), F32),
        compiler_params=_params("parallel"), name="c_bias")(rpb.reshape(-1))
    hs = heads_per_step
    n_hg = n_heads // hs
    width = hs * HEAD_DIM

    def spec(part, di):
        return pl.BlockSpec(
            (tq, width), lambda hg, i: (jnp.clip(i + di, 0, nblk - 1), part * n_hg + hg))

    o = pl.pallas_call(
        functools.partial(_win_attn_kernel, n_heads=hs, tq=tq, lead=2), grid=(n_hg, nblk),
        in_specs=[spec(0, 0), spec(1, -1), spec(1, 0), spec(1, 1),
                  spec(2, -1), spec(2, 0), spec(2, 1),
                  pl.BlockSpec((1, hs, tq, 3 * tq),
                               lambda hg, i: (_edge_variant(i, nblk), hg, 0, 0))],
        out_specs=pl.BlockSpec((tq, width), lambda hg, i: (i, hg)),
        out_shape=jax.ShapeDtypeStruct((S, D), BF16),
        compiler_params=_params("parallel", "parallel"),
        name="attn_c")(qkv, qkv, qkv, qkv, qkv, qkv, qkv, bias)
    return matmul(o, w_out, F32)


def kernel(x, rel_bias, norm_g, a_w_in, a_w_out, a_lambda, a_subln, b_w_in, b_w_out, c_w_in, c_w_out,
           c_rpb, d_w_in, d_w_out, d_qk_norm, ffn_w_gate, ffn_w_up, ffn_w_down):
    B, S, D = x.shape
    assert B == 1
    depth = norm_g.shape[0]
    xs = x.reshape(S, D)
    hn = prenorm(xs, norm_g[0, 0])
    hn_by_residue = None

    def mixer_weights(layer):
        m, j = layer % N_MIXERS, layer // N_MIXERS
        w_in, w_out, q_cols, q_scale = (
            (a_w_in, a_w_out, D, SCALE * LOG2E), (b_w_in, b_w_out, b_w_in.shape[2] // 3, SCALE),
            (c_w_in, c_w_out, D, SCALE), (d_w_in, d_w_out, 0, None))[m]
        scale = None if q_scale is None else jnp.concatenate(
            [jnp.full((q_cols,), q_scale, F32), jnp.ones((w_in.shape[2] - q_cols,), F32)])
        return w_in, w_out, j, scale

    w_in, w_out, j, scale = mixer_weights(0)
    mix_w = ((w_in[j] if scale is None else w_in[j] * scale).astype(BF16), w_out[j].astype(BF16))
    ffn_w = (ffn_w_gate[0].astype(BF16), ffn_w_up[0].astype(BF16), ffn_w_down[0].astype(BF16))
    for i in range(depth):
        m, j = i % N_MIXERS, i // N_MIXERS
        if m == 0:
            y = mixer_a(hn, *mix_w, a_lambda[j], a_subln[j], rel_bias, i)
        elif m == 1:
            y = mixer_b(hn, *mix_w, rel_bias, hn_by_residue)
        elif m == 2:
            y = mixer_c(hn, *mix_w, c_rpb[j])
        else:
            y = mixer_d(hn, *mix_w, d_qk_norm[j])
        xs, hn, _ = resid(xs, y, norm_g[i, 1], norm_g[i, 2])
        cast_next, riders = None, ()
        if i + 1 < depth:
            cast_next = (i + 1, ffn_w_gate, ffn_w_up, ffn_w_down)
            w_in, w_out, jn, scale = mixer_weights(i + 1)
            riders = ((w_in, jn, scale), (w_out, jn, None))
        y, ffn_w, mix_w = ffn(hn, *ffn_w, cast_next, riders)
        if i + 1 < depth:
            residues = tuple(d for _, d in B_GROUPS if d > 1) if (i + 1) % N_MIXERS == 1 else ()
            xs, hn, hn_by_residue = resid(xs, y, norm_g[i, 3], norm_g[i + 1, 0], residues=residues)
        else:
            xs = resid(xs, y, norm_g[i, 3])
    return xs.reshape(B, S, D)
```

```python
import functools
import math

import numpy as np
import jax
import jax.numpy as jnp
from jax import lax
from jax.experimental import pallas as pl
from jax.experimental.pallas import tpu as pltpu

F32 = jnp.float32
BF16 = jnp.bfloat16

HEAD_DIM = 128
SCALE = HEAD_DIM ** -0.5
LOG2E = math.log2(math.e)
GRID_W = 64
EPS = 1e-6
REL_BUCKETS = 32
REL_MAX_DIST = 1024
B_GROUPS = ((128, 1), (512, 4), (2048, 16))
C_WIN_R = 8
C_WIN_C = 16
D_KV_HEADS = 8
ROPE_THETA = 10000.0
N_MIXERS = 4

KEY_CHUNK = 256
SUBLANES = 8
ONES_ROWS = 2 * SUBLANES
MASKED = -1e30
VMEM_LIMIT_BYTES = 56 * 2 ** 20


def _params(*semantics):
    return pltpu.CompilerParams(dimension_semantics=semantics, vmem_limit_bytes=VMEM_LIMIT_BYTES)


def _rms(x, g):
    ms = jnp.mean(x * x, axis=-1, keepdims=True)
    return x * lax.rsqrt(ms + EPS) * g


def _dot_nt(a, b):
    return lax.dot_general(a, b, (((1,), (1,)), ((), ())), preferred_element_type=F32)


def _prenorm_kernel(x_ref, g_ref, h_ref):
    h_ref[...] = _rms(x_ref[...], g_ref[...]).astype(h_ref.dtype)


def prenorm(x, g, tm=256):
    S, D = x.shape
    row = pl.BlockSpec((tm, D), lambda i: (i, 0))
    vec = pl.BlockSpec((1, D), lambda i: (0, 0))
    return pl.pallas_call(
        _prenorm_kernel, grid=(S // tm,), in_specs=[row, vec], out_specs=row,
        out_shape=jax.ShapeDtypeStruct((S, D), BF16), compiler_params=_params("parallel"),
        name="prenorm")(x, g.reshape(1, D))


def _resid_kernel(x_ref, y_ref, g1_ref, g2_ref, xo_ref, h_ref, *rest, residues):
    x = x_ref[...] + _rms(y_ref[...], g1_ref[...])
    xo_ref[...] = x
    h = _rms(x, g2_ref[...])
    h_ref[...] = h.astype(h_ref.dtype)
    if residues:
        sc = rest[-1]
        rows = h.shape[0]
        for c in range(sc.shape[0]):
            sc[c] = h[:, c * HEAD_DIM:(c + 1) * HEAD_DIM]
        for o_ref, dil in zip(rest, residues):
            for r in range(dil):
                for c in range(sc.shape[0]):
                    o_ref[r, :, c * HEAD_DIM:(c + 1) * HEAD_DIM] = (
                        sc[c, pl.ds(r, rows // dil, stride=dil), :].astype(o_ref.dtype))


def _resid_last_kernel(x_ref, y_ref, g1_ref, xo_ref):
    xo_ref[...] = x_ref[...] + _rms(y_ref[...], g1_ref[...])


def resid(x, y, g_post, g_next=None, tm=256, residues=()):
    S, D = x.shape
    row = pl.BlockSpec((tm, D), lambda i: (i, 0))
    vec = pl.BlockSpec((1, D), lambda i: (0, 0))
    if g_next is None:
        return pl.pallas_call(
            _resid_last_kernel, grid=(S // tm,), in_specs=[row, row, vec], out_specs=row,
            out_shape=jax.ShapeDtypeStruct((S, D), F32), compiler_params=_params("parallel"),
            name="resid_last")(x, y, g_post.reshape(1, D))
    assert all(tm % (2 * SUBLANES * dil) == 0 for dil in residues)
    res = pl.pallas_call(
        functools.partial(_resid_kernel, residues=tuple(residues)), grid=(S // tm,),
        in_specs=[row, row, vec, vec],
        out_specs=[row, row] + [pl.BlockSpec((dil, tm // dil, D), lambda i: (0, i, 0)) for dil in residues],
        out_shape=[jax.ShapeDtypeStruct((S, D), F32), jax.ShapeDtypeStruct((S, D), BF16)]
        + [jax.ShapeDtypeStruct((dil, S // dil, D), BF16) for dil in residues],
        scratch_shapes=[pltpu.VMEM((D // HEAD_DIM, tm, HEAD_DIM), F32)] if residues else [],
        compiler_params=_params("parallel"),
        name="resid")(x, y, g_post.reshape(1, D), g_next.reshape(1, D))
    return res[0], res[1], {dil: r.reshape(S, D) for dil, r in zip(residues, res[2:])}


def _mm_kernel(a_ref, b_ref, o_ref):
    o_ref[...] = jnp.dot(a_ref[...], b_ref[...], preferred_element_type=F32).astype(o_ref.dtype)


def _mm_acc_kernel(a_ref, b_ref, *rest, scaled):
    n_in = len(scaled) + sum(scaled)
    o_ref, acc_ref = rest[n_in], rest[-1]
    k = pl.program_id(2)

    @pl.when(k == 0)
    def _():
        acc_ref[...] = jnp.zeros_like(acc_ref)

    acc_ref[...] += jnp.dot(a_ref[...], b_ref[...], preferred_element_type=F32)

    @pl.when(k == pl.num_programs(2) - 1)
    def _():
        o_ref[...] = acc_ref[...].astype(o_ref.dtype)

    srcs = list(rest[:n_in])
    for has_scale, dst in zip(scaled, rest[n_in + 1:-1]):
        w = srcs.pop(0)[...]
        if has_scale:
            w = w * srcs.pop(0)[...]
        dst[...] = w.astype(dst.dtype)


def matmul(a, b, out_dtype, tm=1024, tn=None, tk=None, cols=None, riders=()):
    M, K = a.shape
    tk = K if tk is None else tk
    if tn is None:
        tn = 1024 if b.shape[1] % 1024 == 0 else 512
    n_tiles, col = (b.shape[1] // tn, lambda j: j) if cols is None else cols
    out_shape = jax.ShapeDtypeStruct((M, n_tiles * tn), out_dtype)
    if tk == K:
        assert not riders
        return pl.pallas_call(
            _mm_kernel, grid=(M // tm, n_tiles),
            in_specs=[pl.BlockSpec((tm, K), lambda i, j: (i, 0)),
                      pl.BlockSpec((K, tn), lambda i, j: (0, col(j)))],
            out_specs=pl.BlockSpec((tm, tn), lambda i, j: (i, j)),
            out_shape=out_shape, compiler_params=_params("parallel", "parallel"),
            name="matmul")(a, b)
    gi, gj, gk = M // tm, n_tiles, K // tk
    in_specs = [pl.BlockSpec((tm, tk), lambda i, j, k: (i, k)),
                pl.BlockSpec((tk, tn), lambda i, j, k: (k, col(j)))]
    out_specs = [pl.BlockSpec((tm, tn), lambda i, j, k: (i, j))]
    out_shapes = [out_shape]
    operands = [a, b]
    for w, layer, scale in riders:
        R, C = w.shape[1:]
        rb, cb = R // gi, C // (gj * gk)
        assert rb * gi == R and cb * gj * gk == C and rb % SUBLANES == 0 and cb % HEAD_DIM == 0
        in_specs.append(pl.BlockSpec((None, rb, cb), lambda i, j, k, layer=layer: (layer, i, j * gk + k)))
        operands.append(w)
        if scale is not None:
            in_specs.append(pl.BlockSpec((1, cb), lambda i, j, k: (0, j * gk + k)))
            operands.append(scale.reshape(1, C))
        out_specs.append(pl.BlockSpec((rb, cb), lambda i, j, k: (i, j * gk + k)))
        out_shapes.append(jax.ShapeDtypeStruct((R, C), BF16))
    res = pl.pallas_call(
        functools.partial(_mm_acc_kernel, scaled=tuple(s is not None for _, _, s in riders)),
        grid=(gi, gj, gk), in_specs=in_specs, out_specs=out_specs, out_shape=out_shapes,
        scratch_shapes=[pltpu.VMEM((tm, tn), F32)],
        compiler_params=_params("parallel", "parallel", "arbitrary"),
        name="matmul_acc")(*operands)
    return (res[0], tuple(res[1:])) if riders else res[0]


def _glu_kernel(a_ref, wg_ref, wu_ref, *rest):
    o_ref = rest[len(rest) // 2]
    a = a_ref[...]
    g = jnp.dot(a, wg_ref[...], preferred_element_type=F32)
    u = jnp.dot(a, wu_ref[...], preferred_element_type=F32)
    o_ref[...] = (g * (1.0 / (1.0 + jnp.exp(-g))) * u).astype(o_ref.dtype)
    n_cast = len(rest) // 2
    for src, dst in zip(rest[:n_cast], rest[n_cast + 1:]):
        dst[...] = src[...].astype(dst.dtype)


def glu_up(a, wg, wu, cast_next=None, tm=2048, tn=256):
    M, K = a.shape
    N = wg.shape[1]
    gi, gj = M // tm, N // tn
    wspec = pl.BlockSpec((K, tn), lambda i, j: (0, j))
    in_specs = [pl.BlockSpec((tm, K), lambda i, j: (i, 0)), wspec, wspec]
    out_specs = [pl.BlockSpec((tm, tn), lambda i, j: (i, j))]
    out_shape = [jax.ShapeDtypeStruct((M, N), BF16)]
    extra = []
    if cast_next is not None:
        layer, ng, nu, nd = cast_next
        D = nd.shape[2]
        up_spec = pl.BlockSpec((None, K // gi, tn), lambda i, j: (layer, i, j))
        down_spec = pl.BlockSpec((None, tn, D // gi), lambda i, j: (layer, j, i))
        in_specs += [up_spec, up_spec, down_spec]
        out_specs += [pl.BlockSpec((K // gi, tn), lambda i, j: (i, j)),
                      pl.BlockSpec((K // gi, tn), lambda i, j: (i, j)),
                      pl.BlockSpec((tn, D // gi), lambda i, j: (j, i))]
        out_shape += [jax.ShapeDtypeStruct((K, N), BF16), jax.ShapeDtypeStruct((K, N), BF16),
                      jax.ShapeDtypeStruct((N, D), BF16)]
        extra = [ng, nu, nd]
    res = pl.pallas_call(
        _glu_kernel, grid=(gi, gj), in_specs=in_specs, out_specs=out_specs, out_shape=out_shape,
        compiler_params=_params("parallel", "parallel"), name="glu_up")(a, wg, wu, *extra)
    return res[0], tuple(res[1:])


def ffn(h, w_gate, w_up, w_down, cast_next=None, riders=()):
    hid, nxt = glu_up(h, w_gate, w_up, cast_next)
    d_ff = hid.shape[1]
    tk = d_ff // 2 if (d_ff // 2) % HEAD_DIM == 0 else d_ff
    y = matmul(hid, w_down, F32, tm=1024, tn=512, tk=tk, riders=riders)
    return (y[0], nxt, y[1]) if riders else (y, nxt, ())


def _t5_thresholds():
    nb = REL_BUCKETS // 2
    max_exact = nb // 2
    n = np.arange(0, 4 * REL_MAX_DIST)
    nf = np.maximum(n, 1).astype(np.float32)
    large = max_exact + (np.log(nf / np.float32(max_exact))
                         / np.float32(math.log(REL_MAX_DIST / max_exact))
                         * np.float32(nb - max_exact)).astype(np.int32)
    bucket = np.where(n < max_exact, n, np.minimum(large, nb - 1))
    assert np.all(np.diff(bucket) >= 0) and bucket[-1] == nb - 1
    return tuple(int(np.argmax(bucket >= b)) for b in range(1, nb))


T5_THRESHOLDS = _t5_thresholds()


def _t5_tile_kernel(rb_ref, var_ref, o_ref, *, dil, half, n_heads, mult, keys_on_rows):
    h = pl.program_id(0)
    v = pl.program_id(1)
    off, col_lo, col_hi, far = (var_ref[4 * v + t] for t in range(4))
    rows, cols = o_ref.shape[2], o_ref.shape[3]
    nb = REL_BUCKETS // 2

    @pl.when(far == 0)
    def _():
        i = lax.broadcasted_iota(jnp.int32, (rows, cols), 0)
        j = lax.broadcasted_iota(jnp.int32, (rows, cols), 1)
        delta = (i - j if keys_on_rows else j - i) + off
        rel = delta * dil
        n = jnp.abs(rel)
        vneg = jnp.full((rows, cols), rb_ref[h], F32)
        vpos = jnp.full((rows, cols), rb_ref[nb * n_heads + h], F32)
        for b, t in enumerate(T5_THRESHOLDS, start=1):
            c = n >= t
            vneg = jnp.where(c, rb_ref[b * n_heads + h], vneg)
            vpos = jnp.where(c, rb_ref[(nb + b) * n_heads + h], vpos)
        val = jnp.where(rel > 0, vpos, vneg)
        if mult != 1.0:
            val = val * mult
        if half is not None:
            ok = (jnp.abs(delta) <= half) & (j >= col_lo) & (j < col_hi)
            val = jnp.where(ok, val, MASKED)
        o_ref[0, 0] = val

    @pl.when(far != 0)
    def _():
        last = jnp.where(far > 0, 2 * nb - 1, nb - 1)
        val = jnp.full((rows, cols), rb_ref[last * n_heads + h], F32)
        o_ref[0, 0] = val * mult if mult != 1.0 else val


def t5_tiles(rel_bias, offs, rows, cols, dil=1, half=None, col_lo=None, col_hi=None, mult=1.0,
             keys_on_rows=False):
    n_heads = rel_bias.shape[1]
    nv = len(offs)
    col_lo = (0,) * nv if col_lo is None else col_lo
    col_hi = (cols,) * nv if col_hi is None else col_hi
    key_hi, qry_hi = (rows, cols) if keys_on_rows else (cols, rows)
    limit = T5_THRESHOLDS[-1]
    far = [0 if half is not None else
           1 if dil * (off - (qry_hi - 1)) >= limit else -1 if dil * (off + key_hi - 1) <= -limit else 0
           for off in offs]
    variants = jnp.asarray(np.stack([offs, col_lo, col_hi, far], axis=1).reshape(-1), jnp.int32)
    kern = functools.partial(_t5_tile_kernel, dil=dil, half=half, n_heads=n_heads, mult=mult,
                             keys_on_rows=keys_on_rows)
    smem = pl.BlockSpec(memory_space=pltpu.SMEM)
    return pl.pallas_call(
        kern, grid=(n_heads, nv), in_specs=[smem, smem],
        out_specs=pl.BlockSpec((1, 1, rows, cols), lambda h, v: (v, h, 0, 0)),
        out_shape=jax.ShapeDtypeStruct((nv, n_heads, rows, cols), F32),
        compiler_params=_params("parallel", "parallel"), name="t5_tiles")(rel_bias.reshape(-1), variants)


def _sublane_groups(x):
    return x.reshape(x.shape[0] // SUBLANES, SUBLANES, x.shape[1])


def _store_scores(s_sc, mp_sc, idx, s):
    s_sc[idx] = s
    mp_sc[idx] = jnp.max(_sublane_groups(s), axis=0)


def _online_step(m_sc, acc_sc, idx, s_sc, mp_sc, vt_chunk):
    Tk, Tq = s_sc.shape[1:]
    m_old = m_sc[idx]
    m_cur = jnp.max(mp_sc[idx], axis=0, keepdims=True)
    m_new = jnp.maximum(m_old, jnp.broadcast_to(m_cur, m_old.shape))
    m_sc[idx] = m_new
    alpha = jnp.exp2(m_old - m_new)
    pv = None
    for c in range(Tk // KEY_CHUNK):
        ks = slice(c * KEY_CHUNK, (c + 1) * KEY_CHUNK)
        p = jnp.exp2(_sublane_groups(s_sc[idx, ks, :]) - m_new[None])
        d = jnp.dot(vt_chunk(ks), p.reshape(KEY_CHUNK, Tq).astype(BF16), preferred_element_type=F32)
        pv = d if pv is None else pv + d
    acc_sc[idx] = (_sublane_groups(acc_sc[idx]) * alpha[None]).reshape(acc_sc.shape[1:]) + pv


def _flash_init(m_sc, acc_sc):
    m_sc[...] = jnp.full(m_sc.shape, -jnp.inf, F32)
    acc_sc[...] = jnp.zeros(acc_sc.shape, F32)


def _values_transposed(v, n_heads, Tk):
    S = v.shape[0]
    dv = v.shape[1] // n_heads
    v_t = v.reshape(S // Tk, Tk, n_heads, dv).transpose(2, 0, 3, 1)
    return jnp.concatenate([v_t, jnp.ones((n_heads, S // Tk, ONES_ROWS, Tk), v.dtype)], axis=2)


def _pipelined_blocks(nk, parts, scores, consume, buf0, buf1):
    assert nk % 2 == 0
    for part in parts:
        scores(0, *buf0, part)

    def step(kb, cur, nxt):
        for part in parts:
            scores(kb + 1, *nxt, part)
            consume(kb, *cur, part)

    def body(j, carry):
        step(2 * j, buf0, buf1)
        step(2 * j + 1, buf1, buf0)
        return carry

    lax.fori_loop(0, nk // 2 - 1, body, 0)
    step(nk - 2, buf0, buf1)
    for part in parts:
        consume(nk - 1, *buf1, part)


def _attn_a_kernel(q_ref, k_ref, vt_ref, band_ref, lam_ref, g_ref, o_ref, m_sc, acc_sc,
                   s0_sc, s1_sc, mp0_sc, mp1_sc, *, Tk, nk, q_step, k_step, e_lo, e_hi, lam_init):
    qb = pl.program_id(1)
    dv = 2 * HEAD_DIM
    _flash_init(m_sc, acc_sc)

    def scores(kb, s_sc, mp_sc, part):
        off = pl.multiple_of(kb * Tk, Tk)
        bidx = jnp.clip(kb * k_step - qb * q_step, e_lo, e_hi) - e_lo
        for m in part:
            sl = slice(m * HEAD_DIM, (m + 1) * HEAD_DIM)
            _store_scores(s_sc, mp_sc, m,
                          _dot_nt(k_ref[0, pl.ds(off, Tk), sl], q_ref[:, sl]) + band_ref[bidx, 0])

    def consume(kb, s_sc, mp_sc, part):
        for m in part:
            _online_step(m_sc, acc_sc, m, s_sc, mp_sc, lambda ks: vt_ref[0, kb, :, ks])

    _pipelined_blocks(nk, ((0, 1),), scores, consume, (s0_sc, mp0_sc), (s1_sc, mp1_sc))
    lp = lam_ref[...]
    lam = (jnp.exp(jnp.sum(lp[0:1] * lp[1:2], axis=-1, keepdims=True))
           - jnp.exp(jnp.sum(lp[2:3] * lp[3:4], axis=-1, keepdims=True)) + lam_init)
    acc0, acc1 = acc_sc[0], acc_sc[1]
    o_t = acc0[:dv] / acc0[dv:dv + 1] - lam * (acc1[:dv] / acc1[dv:dv + 1])
    o_ref[...] = (_rms(o_t.T, g_ref[...]) * (1.0 - lam_init)).astype(o_ref.dtype)


def mixer_a(h, w_in, w_out, lam_p, subln_g, rel_bias, layer_idx, T=512, Tk=1024):
    S, D = h.shape
    n_heads = D // (2 * HEAD_DIM)
    width = 2 * HEAD_DIM
    qkv = matmul(h, w_in, BF16)
    far = T5_THRESHOLDS[-1]
    unit = math.gcd(T, Tk)
    e_hi = -(-(far + T - 1) // unit)
    e_lo = -(-(far + Tk - 1) // unit)
    e_lo = -e_lo
    tiles = t5_tiles(rel_bias, [e * unit for e in range(e_lo, e_hi + 1)], Tk, T, mult=LOG2E,
                     keys_on_rows=True)
    n_tiles = e_hi - e_lo + 1
    nk = S // Tk
    v_t = _values_transposed(qkv[:, 2 * D:], n_heads, Tk)
    k_hm = qkv[:, D:2 * D].reshape(S, n_heads, width).transpose(1, 0, 2)
    lam_init = 0.8 - 0.6 * math.exp(-0.3 * layer_idx)
    kern = functools.partial(_attn_a_kernel, Tk=Tk, nk=nk, q_step=T // unit, k_step=Tk // unit,
                             e_lo=e_lo, e_hi=e_hi, lam_init=lam_init)
    once = pl.Buffered(1)
    o = pl.pallas_call(
        kern, grid=(n_heads, S // T),
        in_specs=[pl.BlockSpec((T, width), lambda hh, qb: (qb, hh)),
                  pl.BlockSpec((1, S, width), lambda hh, qb: (hh, 0, 0), pipeline_mode=once),
                  pl.BlockSpec((1, nk, width + ONES_ROWS, Tk), lambda hh, qb: (hh, 0, 0, 0),
                               pipeline_mode=once),
                  pl.BlockSpec((n_tiles, 1, Tk, T), lambda hh, qb: (0, hh, 0, 0), pipeline_mode=once),
                  pl.BlockSpec((4, HEAD_DIM), lambda hh, qb: (0, 0)),
                  pl.BlockSpec((1, width), lambda hh, qb: (0, 0))],
        out_specs=pl.BlockSpec((T, width), lambda hh, qb: (qb, hh)),
        out_shape=jax.ShapeDtypeStruct((S, D), BF16),
        scratch_shapes=[pltpu.VMEM((2, SUBLANES, T), F32), pltpu.VMEM((2, width + ONES_ROWS, T), F32),
                        pltpu.VMEM((2, Tk, T), F32), pltpu.VMEM((2, Tk, T), F32),
                        pltpu.VMEM((2, SUBLANES, T), F32), pltpu.VMEM((2, SUBLANES, T), F32)],
        compiler_params=_params("parallel", "arbitrary"),
        name="attn_a")(qkv, k_hm, v_t, tiles, lam_p, subln_g.reshape(1, width))
    return matmul(o, w_out, F32)


def _rope_kernel(x_ref, c_ref, sa_ref, sb_ref, g_ref, oq_ref, ok_ref, *, n_q, n_chunks):
    c, sa, sb = c_ref[...], sa_ref[...], sb_ref[...]
    for ch in range(n_chunks):
        sl = slice(ch * HEAD_DIM, (ch + 1) * HEAD_DIM)
        y = _rms(x_ref[:, sl], g_ref[0:1] if ch < n_q else g_ref[1:2])
        y = y * c + pltpu.roll(y, HEAD_DIM - 1, 1) * sa + pltpu.roll(y, 1, 1) * sb
        if ch < n_q:
            oq_ref[:, sl] = (y * (SCALE * LOG2E)).astype(oq_ref.dtype)
        else:
            ok_ref[ch - n_q] = y.astype(ok_ref.dtype)


def _rope_tables(S):
    pos = jnp.arange(S)
    n_freq = HEAD_DIM // 4
    freqs = ROPE_THETA ** (-jnp.arange(n_freq, dtype=F32) / n_freq)
    ang = jnp.concatenate([(pos // GRID_W).astype(F32)[:, None] * freqs,
                           (pos % GRID_W).astype(F32)[:, None] * freqs], axis=-1)
    cos, sin = jnp.cos(ang), jnp.sin(ang)
    zero = jnp.zeros_like(sin)
    c = jnp.stack([cos, cos], axis=-1).reshape(S, HEAD_DIM)
    sa = jnp.stack([-sin, zero], axis=-1).reshape(S, HEAD_DIM)
    sb = jnp.stack([zero, sin], axis=-1).reshape(S, HEAD_DIM)
    return c, sa, sb


def _attn_d_kernel(q_ref, k_ref, vt_ref, o_ref, m_sc, acc_sc, s0_sc, s1_sc, mp0_sc, mp1_sc,
                   *, Tk, nk, rep):
    _flash_init(m_sc, acc_sc)

    def scores(kb, s_sc, mp_sc, part):
        kblk = k_ref[0, pl.ds(pl.multiple_of(kb * Tk, Tk), Tk), :]
        for r in part:
            _store_scores(s_sc, mp_sc, r, _dot_nt(kblk, q_ref[:, r * HEAD_DIM:(r + 1) * HEAD_DIM]))

    def consume(kb, s_sc, mp_sc, part):
        for r in part:
            _online_step(m_sc, acc_sc, r, s_sc, mp_sc, lambda ks: vt_ref[0, kb, :, ks])

    parts = tuple(tuple(range(r, min(r + 2, rep))) for r in range(0, rep, 2))
    _pipelined_blocks(nk, parts, scores, consume, (s0_sc, mp0_sc), (s1_sc, mp1_sc))
    for r in range(rep):
        acc = acc_sc[r]
        o_t = acc[:HEAD_DIM] / acc[HEAD_DIM:HEAD_DIM + 1]
        o_ref[:, r * HEAD_DIM:(r + 1) * HEAD_DIM] = o_t.T.astype(o_ref.dtype)


def mixer_d(h, w_in, w_out, qk_g, T=512, Tk=1024, tm=256):
    S, D = h.shape
    n_q = D // HEAD_DIM
    n_kv = D_KV_HEADS
    rep = n_q // n_kv
    nqk = (n_q + n_kv) * HEAD_DIM
    tn = n_kv * HEAD_DIM
    assert nqk % tn == 0 and w_in.shape[1] == nqk + tn
    qk = matmul(h, w_in, F32, tn=tn, cols=(nqk // tn, lambda j: j))
    v = matmul(h, w_in, BF16, tn=tn, cols=(1, lambda j: j + nqk // tn))
    c, sa, sb = _rope_tables(S)
    tab = pl.BlockSpec((tm, HEAD_DIM), lambda i: (i, 0))
    q, k = pl.pallas_call(
        functools.partial(_rope_kernel, n_q=n_q, n_chunks=n_q + n_kv), grid=(S // tm,),
        in_specs=[pl.BlockSpec((tm, nqk), lambda i: (i, 0)), tab, tab, tab,
                  pl.BlockSpec((2, HEAD_DIM), lambda i: (0, 0))],
        out_specs=[pl.BlockSpec((tm, D), lambda i: (i, 0)),
                   pl.BlockSpec((n_kv, tm, HEAD_DIM), lambda i: (0, i, 0))],
        out_shape=[jax.ShapeDtypeStruct((S, D), BF16),
                   jax.ShapeDtypeStruct((n_kv, S, HEAD_DIM), BF16)],
        compiler_params=_params("parallel"), name="qk_norm_rope")(qk, c, sa, sb, qk_g)
    nk = S // Tk
    v_t = _values_transposed(v, n_kv, Tk)
    vt_rows = HEAD_DIM + ONES_ROWS
    kern = functools.partial(_attn_d_kernel, Tk=Tk, nk=nk, rep=rep)
    o = pl.pallas_call(
        kern, grid=(n_kv, S // T),
        in_specs=[pl.BlockSpec((T, rep * HEAD_DIM), lambda g, qb: (qb, g)),
                  pl.BlockSpec((1, S, HEAD_DIM), lambda g, qb: (g, 0, 0)),
                  pl.BlockSpec((1, nk, vt_rows, Tk), lambda g, qb: (g, 0, 0, 0))],
        out_specs=pl.BlockSpec((T, rep * HEAD_DIM), lambda g, qb: (qb, g)),
        out_shape=jax.ShapeDtypeStruct((S, D), BF16),
        scratch_shapes=[pltpu.VMEM((rep, SUBLANES, T), F32),
                        pltpu.VMEM((rep, vt_rows, T), F32),
                        pltpu.VMEM((rep, Tk, T), F32), pltpu.VMEM((rep, Tk, T), F32),
                        pltpu.VMEM((rep, SUBLANES, T), F32), pltpu.VMEM((rep, SUBLANES, T), F32)],
        compiler_params=_params("parallel", "arbitrary"), name="attn_d")(q, k, v_t)
    return matmul(o, w_out, F32)


def _lane_tiles(x):
    return [x[:, c * HEAD_DIM:(c + 1) * HEAD_DIM] for c in range(x.shape[1] // HEAD_DIM)]


def _win_attn_kernel(q_ref, kp_ref, kc_ref, kn_ref, vp_ref, vc_ref, vn_ref, bias_ref, o_ref,
                     *lse_ref, n_heads, tq, lead):
    lse_cols = []

    def scores(h):
        sl = slice(h * HEAD_DIM, (h + 1) * HEAD_DIM)
        q = q_ref[:, sl]
        ks = (kp_ref[:, sl], kc_ref[:, sl], kn_ref[:, sl])
        return [_dot_nt(q, ks[j]) + bias_ref[0, h, :, j * tq:(j + 1) * tq] for j in range(3)]

    def finish(h, ss):
        sl = slice(h * HEAD_DIM, (h + 1) * HEAD_DIM)
        vs = (vp_ref[:, sl], vc_ref[:, sl], vn_ref[:, sl])
        m = jnp.max(functools.reduce(jnp.maximum, [t for s in ss for t in _lane_tiles(s)]),
                    axis=-1, keepdims=True)
        ps = [jnp.exp(s - m) for s in ss]
        l = jnp.sum(functools.reduce(lambda a, b: a + b, [t for p in ps for t in _lane_tiles(p)]),
                    axis=-1, keepdims=True)
        acc = (jnp.dot(ps[0].astype(BF16), vs[0], preferred_element_type=F32)
               + jnp.dot(ps[1].astype(BF16), vs[1], preferred_element_type=F32)
               + jnp.dot(ps[2].astype(BF16), vs[2], preferred_element_type=F32))
        o_ref[:, sl] = (acc / l).astype(o_ref.dtype)
        lse_cols.append(m + jnp.log(l))

    pending = [scores(h) for h in range(min(lead, n_heads))]
    for h in range(n_heads):
        if h + lead < n_heads:
            pending.append(scores(h + lead))
        finish(h, pending.pop(0))
    if lse_ref:
        lse_ref[0][...] = jnp.concatenate(lse_cols, axis=1)


def _edge_variant(i, nblk):
    return jnp.where(i == 0, 0, jnp.where(i == nblk - 1, 2, 1))


def _merge_b_kernel(*refs, n_heads, dils):
    G = len(dils)
    o_refs, lse_ref, out_ref, scratch = refs[:G], refs[G], refs[G + 1], refs[G + 2:]
    rows = out_ref.shape[0]
    lse = lse_ref[...]
    w = jnp.exp(lse - jnp.max(lse, axis=0, keepdims=True))
    w = w / jnp.sum(w, axis=0, keepdims=True)
    strided = [g for g, dil in enumerate(dils) if dil > 1]
    for h in range(n_heads):
        sl = slice(h * HEAD_DIM, (h + 1) * HEAD_DIM)
        acc = None
        for g, (o_ref, dil) in enumerate(zip(o_refs, dils)):
            if dil == 1:
                o = o_ref[0, :, sl]
            else:
                sc = scratch[strided.index(g)]
                for r in range(dil):
                    sc[h, pl.ds(r, rows // dil, stride=dil), :] = o_ref[r, :, sl]
                o = sc[h]
            term = w[g, :, h:h + 1] * o
            acc = term if acc is None else acc + term
        out_ref[:, sl] = acc.astype(out_ref.dtype)


def mixer_b(h, w_in, w_out, rel_bias, h_by_residue=None, tq=128, tm=256):
    S, D = h.shape
    G = len(B_GROUPS)
    n_heads = rel_bias.shape[1]
    width = n_heads * HEAD_DIM
    tn = width // 2
    outs, lses = [], []
    for g, (window, dil) in enumerate(B_GROUPS):
        half = window // 2 // dil
        assert half <= tq
        L = S // dil
        nblk = L // tq
        assert nblk >= 2

        def by_residue(a, dil=dil, L=L):
            return a if dil == 1 else a.reshape(L, dil, -1).transpose(1, 0, 2).reshape(S, -1)

        def by_position(a, dil=dil, L=L):
            return a if dil == 1 else a.reshape(dil, L, -1).transpose(1, 0, 2).reshape(S, -1)

        per = width // tn
        h_g = h_by_residue[dil] if h_by_residue and dil in h_by_residue else by_residue(h)
        qkv = matmul(h_g, w_in, BF16, tn=tn,
                     cols=(3 * per, lambda j, g=g, per=per: (j // per) * (G * per) + g * per + j % per))
        bias = t5_tiles(rel_bias, [-tq] * 3, tq, 3 * tq, dil=dil, half=half,
                        col_lo=(tq, 0, 0), col_hi=(3 * tq, 3 * tq, 2 * tq))

        def spec(part, di, nblk=nblk):
            return pl.BlockSpec(
                (tq, width), lambda r, i: (r * nblk + jnp.clip(i + di, 0, nblk - 1), part))

        o_g, lse_g = pl.pallas_call(
            functools.partial(_win_attn_kernel, n_heads=n_heads, tq=tq, lead=4), grid=(dil, nblk),
            in_specs=[spec(0, 0), spec(1, -1), spec(1, 0), spec(1, 1),
                      spec(2, -1), spec(2, 0), spec(2, 1),
                      pl.BlockSpec((1, n_heads, tq, 3 * tq),
                                   lambda r, i, nblk=nblk: (_edge_variant(i, nblk), 0, 0, 0))],
            out_specs=[pl.BlockSpec((tq, width), lambda r, i, nblk=nblk: (r * nblk + i, 0)),
                       pl.BlockSpec((tq, n_heads), lambda r, i, nblk=nblk: (r * nblk + i, 0))],
            out_shape=[jax.ShapeDtypeStruct((S, width), F32),
                       jax.ShapeDtypeStruct((S, n_heads), F32)],
            compiler_params=_params("parallel", "parallel"),
            name=f"attn_b{g}")(qkv, qkv, qkv, qkv, qkv, qkv, qkv, bias)
        outs.append(o_g.reshape(dil, L, width))
        lses.append(by_position(lse_g))
    dils = tuple(dil for _, dil in B_GROUPS)
    assert all(tm % (dil * SUBLANES) == 0 for dil in dils)
    o = pl.pallas_call(
        functools.partial(_merge_b_kernel, n_heads=n_heads, dils=dils), grid=(S // tm,),
        in_specs=[pl.BlockSpec((dil, tm // dil, width), lambda i: (0, i, 0)) for dil in dils]
        + [pl.BlockSpec((G, tm, n_heads), lambda i: (0, i, 0))],
        out_specs=pl.BlockSpec((tm, width), lambda i: (i, 0)),
        out_shape=jax.ShapeDtypeStruct((S, width), BF16),
        scratch_shapes=[pltpu.VMEM((n_heads, tm, HEAD_DIM), F32) for dil in dils if dil > 1],
        compiler_params=_params("parallel"), name="merge_b")(*outs, jnp.stack(lses, 0))
    return matmul(o, w_out, F32)


def _c_bias_kernel(rpb_ref, o_ref, *, rq, n_dr, n_dc):
    h = pl.program_id(0)
    W = GRID_W
    qc = lax.broadcasted_iota(jnp.int32, (W, 2 * W), 0)
    lane = lax.broadcasted_iota(jnp.int32, (W, 2 * W), 1)
    kc = lane & (W - 1)
    dc = kc - qc
    c0 = jnp.clip(qc - C_WIN_C // 2, 0, W - C_WIN_C)
    ok_c = (kc >= c0) & (kc < c0 + C_WIN_C)
    masked = jnp.full((W, 2 * W), MASKED, F32)
    base = h * (n_dr * n_dc)
    sub = []
    for a in range(n_dr):
        val = masked
        for b in range(n_dc):
            val = jnp.where(dc == b - (C_WIN_C - 1), rpb_ref[base + a * n_dc + b], val)
        sub.append(jnp.where(ok_c, val, MASKED))
    for v in range(3):
        for qr in range(rq):
            for jv in range(3 * rq // 2):
                halves = []
                for kr in (2 * jv, 2 * jv + 1):
                    lo, hi = ((rq, 3 * rq - 1), (qr, qr + C_WIN_R - 1), (0, C_WIN_R - 1))[v]
                    halves.append(sub[kr - qr + C_WIN_R - 1 - rq] if lo <= kr <= hi else masked)
                o_ref[v, 0, qr * W:(qr + 1) * W, jv * 2 * W:(jv + 1) * 2 * W] = jnp.where(
                    lane < W, halves[0], halves[1])


def mixer_c(h, w_in, w_out, rpb, heads_per_step=8):
    S, D = h.shape
    n_heads = D // HEAD_DIM
    rows = S // GRID_W
    rq = C_WIN_R // 2
    tq = rq * GRID_W
    nblk = rows // rq
    assert rows >= C_WIN_R and nblk >= 2 and 2 * GRID_W == HEAD_DIM
    n_dr, n_dc = 2 * C_WIN_R - 1, 2 * C_WIN_C - 1
    qkv = matmul(h, w_in, BF16)
    bias = pl.pallas_call(
        functools.partial(_c_bias_kernel, rq=rq, n_dr=n_dr, n_dc=n_dc), grid=(n_heads,),
        in_specs=[pl.BlockSpec(memory_space=pltpu.SMEM)],
        out_specs=pl.BlockSpec((3, 1, tq, 3 * tq), lambda hh: (0, hh, 0, 0)),
        out_shape=jax.ShapeDtypeStruct((3, n_heads, tq, 3 * tq), F32),
        compiler_params=_params("parallel"), name="c_bias")(rpb.reshape(-1))
    hs = heads_per_step
    n_hg = n_heads // hs
    width = hs * HEAD_DIM

    def spec(part, di):
        return pl.BlockSpec(
            (tq, width), lambda hg, i: (jnp.clip(i + di, 0, nblk - 1), part * n_hg + hg))

    o = pl.pallas_call(
        functools.partial(_win_attn_kernel, n_heads=hs, tq=tq, lead=2), grid=(n_hg, nblk),
        in_specs=[spec(0, 0), spec(1, -1), spec(1, 0), spec(1, 1),
                  spec(2, -1), spec(2, 0), spec(2, 1),
                  pl.BlockSpec((1, hs, tq, 3 * tq),
                               lambda hg, i: (_edge_variant(i, nblk), hg, 0, 0))],
        out_specs=pl.BlockSpec((tq, width), lambda hg, i: (i, hg)),
        out_shape=jax.ShapeDtypeStruct((S, D), BF16),
        compiler_params=_params("parallel", "parallel"),
        name="attn_c")(qkv, qkv, qkv, qkv, qkv, qkv, qkv, bias)
    return matmul(o, w_out, F32)


def kernel(x, rel_bias, norm_g, a_w_in, a_w_out, a_lambda, a_subln, b_w_in, b_w_out, c_w_in, c_w_out,
           c_rpb, d_w_in, d_w_out, d_qk_norm, ffn_w_gate, ffn_w_up, ffn_w_down):
    B, S, D = x.shape
    assert B == 1
    depth = norm_g.shape[0]
    xs = x.reshape(S, D)
    hn = prenorm(xs, norm_g[0, 0])
    hn_by_residue = None

    def mixer_weights(layer):
        m, j = layer % N_MIXERS, layer // N_MIXERS
        w_in, w_out, q_cols, q_scale = (
            (a_w_in, a_w_out, D, SCALE * LOG2E), (b_w_in, b_w_out, b_w_in.shape[2] // 3, SCALE),
            (c_w_in, c_w_out, D, SCALE), (d_w_in, d_w_out, 0, None))[m]
        scale = None if q_scale is None else jnp.concatenate(
            [jnp.full((q_cols,), q_scale, F32), jnp.ones((w_in.shape[2] - q_cols,), F32)])
        return w_in, w_out, j, scale

    w_in, w_out, j, scale = mixer_weights(0)
    mix_w = ((w_in[j] if scale is None else w_in[j] * scale).astype(BF16), w_out[j].astype(BF16))
    ffn_w = (ffn_w_gate[0].astype(BF16), ffn_w_up[0].astype(BF16), ffn_w_down[0].astype(BF16))
    for i in range(depth):
        m, j = i % N_MIXERS, i // N_MIXERS
        if m == 0:
            y = mixer_a(hn, *mix_w, a_lambda[j], a_subln[j], rel_bias, i)
        elif m == 1:
            y = mixer_b(hn, *mix_w, rel_bias, hn_by_residue)
        elif m == 2:
            y = mixer_c(hn, *mix_w, c_rpb[j])
        else:
            y = mixer_d(hn, *mix_w, d_qk_norm[j])
        xs, hn, _ = resid(xs, y, norm_g[i, 1], norm_g[i, 2])
        cast_next, riders = None, ()
        if i + 1 < depth:
            cast_next = (i + 1, ffn_w_gate, ffn_w_up, ffn_w_down)
            w_in, w_out, jn, scale = mixer_weights(i + 1)
            riders = ((w_in, jn, scale), (w_out, jn, None))
        y, ffn_w, mix_w = ffn(hn, *ffn_w, cast_next, riders)
        if i + 1 < depth:
            residues = tuple(d for _, d in B_GROUPS if d > 1) if (i + 1) % N_MIXERS == 1 else ()
            xs, hn, hn_by_residue = resid(xs, y, norm_g[i, 3], norm_g[i + 1, 0], residues=residues)
        else:
            xs = resid(xs, y, norm_g[i, 3])
    return xs.reshape(B, S, D)
```
